```python
import math
import jax, jax.numpy as jnp
from jax import lax
import numpy as np

D_MODEL = 2048
BATCH = 4
SEQ = 2048
DEPTH = 4

HEAD_DIM = 64
N_Q_HEADS = D_MODEL // 2 // HEAD_DIM
N_KV_HEADS = N_Q_HEADS // 4
ATTN_WIDTH = N_Q_HEADS * HEAD_DIM
KV_WIDTH = N_KV_HEADS * HEAD_DIM
WINDOW = 128
ATTN_BLOCK = 128
ROPE_THETA = 10000.0
POOL_WINDOWS = (2, 4, 8, 16)
POOL_WIDTH = D_MODEL // 4
POOL_GROUP = POOL_WIDTH // len(POOL_WINDOWS)
SSM_WIDTH = D_MODEL // 4
SSM_GROUP = 16
SSM_N_GROUPS = SSM_WIDTH // SSM_GROUP
SSM_STATE = 64
MIX_WIDTH = ATTN_WIDTH + POOL_WIDTH + SSM_WIDTH
IN_WIDTH = ATTN_WIDTH + 2 * KV_WIDTH + POOL_WIDTH + SSM_WIDTH
D_FF = 5504
CONV_WIDTH = 3
LN_EPS = 1e-5
DEEPNORM_ALPHA = (2 * DEPTH) ** 0.25
DEEPNORM_BETA = (8 * DEPTH) ** -0.25

kernel_name = 'hybrid_swa_pool_s5_convffn'

F32 = jnp.float32


def layer_norm(x, g, b):
    xf = x.astype(F32)
    mu = xf.mean(-1, keepdims=True)
    var = jnp.square(xf - mu).mean(-1, keepdims=True)
    return ((xf - mu) * lax.rsqrt(var + LN_EPS) * g.astype(F32) + b.astype(F32)).astype(x.dtype)


def rope(t, pos):
    half = HEAD_DIM // 2
    inv = ROPE_THETA ** (-jnp.arange(half, dtype=F32) / half)
    ang = pos.astype(F32)[:, None] * inv[None, :]
    cos = jnp.cos(ang)[None, :, None, :]
    sin = jnp.sin(ang)[None, :, None, :]
    tf = t.astype(F32)
    t1, t2 = tf[..., :half], tf[..., half:]
    return jnp.concatenate([t1 * cos - t2 * sin, t2 * cos + t1 * sin], -1).astype(t.dtype)


def sliding_window_attention(q, k, v, sinks):
    bsz, s_len = q.shape[0], q.shape[1]
    nb = s_len // ATTN_BLOCK
    grp = N_Q_HEADS // N_KV_HEADS
    qb = q.reshape(bsz, nb, ATTN_BLOCK, N_KV_HEADS, grp, HEAD_DIM).astype(F32)

    def band(t):
        tb = t.reshape(bsz, nb, ATTN_BLOCK, N_KV_HEADS, HEAD_DIM)
        prev = jnp.concatenate([jnp.zeros_like(tb[:, :1]), tb[:, :-1]], axis=1)
        return jnp.concatenate([prev, tb], axis=2).astype(F32)

    kw, vw = band(k), band(v)
    s = jnp.einsum('bnqkgd,bnjkd->bnkgqj', qb, kw) * (HEAD_DIM ** -0.5)
    i = jnp.arange(ATTN_BLOCK)[:, None]
    j = jnp.arange(2 * ATTN_BLOCK)[None, :]
    dist = i + ATTN_BLOCK - j
    in_win = (dist >= 0) & (dist < WINDOW)
    blk = jnp.arange(nb)[:, None, None]
    valid = in_win[None] & ((blk > 0) | (j[None] >= ATTN_BLOCK))
    s = jnp.where(valid[None, :, None, None], s, -1e30)
    sink = sinks.astype(F32).reshape(N_KV_HEADS, grp)[None, None, :, :, None, None]
    m = jnp.maximum(s.max(-1, keepdims=True), sink)
    p = jnp.exp(s - m)
    denom = p.sum(-1, keepdims=True) + jnp.exp(sink - m)
    o = jnp.einsum('bnkgqj,bnjkd->bnqkgd', p / denom, vw)
    return o.reshape(bsz, s_len, ATTN_WIDTH).astype(q.dtype)


def multiscale_pool(u, pool_w, pool_scale):
    uf = u.astype(F32)
    s_len = u.shape[1]
    t = jnp.arange(s_len, dtype=F32)[None, :, None]
    outs = []
    for gi, w in enumerate(POOL_WINDOWS):
        ug = uf[..., gi * POOL_GROUP:(gi + 1) * POOL_GROUP]
        c = jnp.cumsum(ug, axis=1)
        c_shift = jnp.pad(c, ((0, 0), (w, 0), (0, 0)))[:, :s_len]
        mean = (c - c_shift) / jnp.minimum(t + 1.0, float(w))
        outs.append(jnp.einsum('bsc,cd->bsd', mean - ug, pool_w[gi].astype(F32)))
    return (jnp.concatenate(outs, -1) * pool_scale.astype(F32)).astype(u.dtype)


def s5_ssm(u, lam_re, lam_im, log_dt, b_re, b_im, c_re, c_im, d, glu_w):
    bsz, s_len = u.shape[0], u.shape[1]
    uf = u.astype(F32).reshape(bsz, s_len, SSM_N_GROUPS, SSM_GROUP)
    lr, li = lam_re.astype(F32), lam_im.astype(F32)
    dt = jnp.exp(log_dt.astype(F32))[:, None]
    mag = jnp.exp(lr * dt)
    ab_re, ab_im = mag * jnp.cos(li * dt), mag * jnp.sin(li * dt)
    nr, ni = ab_re - 1.0, ab_im
    den = lr * lr + li * li
    zr = (nr * lr + ni * li) / den
    zi = (ni * lr - nr * li) / den
    br, bi = b_re.astype(F32), b_im.astype(F32)
    bbr = zr[..., None] * br - zi[..., None] * bi
    bbi = zr[..., None] * bi + zi[..., None] * br
    xr = jnp.einsum('gph,bsgh->bsgp', bbr, uf)
    xi = jnp.einsum('gph,bsgh->bsgp', bbi, uf)
    ar = jnp.broadcast_to(ab_re, xr.shape)
    ai = jnp.broadcast_to(ab_im, xi.shape)

    def combine(e1, e2):
        a1r, a1i, x1r, x1i = e1
        a2r, a2i, x2r, x2i = e2
        return (a2r * a1r - a2i * a1i, a2r * a1i + a2i * a1r,
                a2r * x1r - a2i * x1i + x2r, a2r * x1i + a2i * x1r + x2i)

    _, _, sr, si = lax.associative_scan(combine, (ar, ai, xr, xi), axis=1)
    y = (jnp.einsum('ghp,bsgp->bsgh', c_re.astype(F32), sr)
         - jnp.einsum('ghp,bsgp->bsgh', c_im.astype(F32), si)
         + d.astype(F32) * uf)
    y = jax.nn.gelu(y.reshape(bsz, s_len, SSM_WIDTH))
    ab = jnp.einsum('bsc,ce->bse', y, glu_w.astype(F32))
    out = ab[..., :SSM_WIDTH] * jax.nn.sigmoid(ab[..., SSM_WIDTH:])
    return out.astype(u.dtype)


def conv_glu_ffn(x, w_up, conv_w, conv_b, w_down):
    s_len = x.shape[1]
    h = jnp.einsum('bsd,df->bsf', x, w_up)
    hp = jnp.pad(h, ((0, 0), (CONV_WIDTH - 1, 0), (0, 0)))
    hc = conv_b + hp[:, 0:s_len] * conv_w[0]
    for tap in range(1, CONV_WIDTH):
        hc = hc + hp[:, tap:tap + s_len] * conv_w[tap]
    val, gate = hc[..., :D_FF], hc[..., D_FF:]
    act = (jax.nn.silu(gate.astype(F32)) * val.astype(F32)).astype(x.dtype)
    return jnp.einsum('bsf,fd->bsd', act, w_down)


def setup_inputs(seed: int = 0) -> dict:
    key = jax.random.key(seed)
    ks = jax.random.split(key, 32)
    L = DEPTH
    nrm = lambda k, shape, std: jax.random.normal(k, shape, F32) * std
    lam_im_base = math.pi * jnp.arange(SSM_STATE, dtype=F32)
    return {
        'x': nrm(ks[0], (BATCH, SEQ, D_MODEL), 1.0),
        'w_in': nrm(ks[1], (L, D_MODEL, IN_WIDTH), D_MODEL ** -0.5),
        'attn_sinks': nrm(ks[2], (L, N_Q_HEADS), 0.5),
        'pool_w': nrm(ks[3], (L, len(POOL_WINDOWS), POOL_GROUP, POOL_GROUP), POOL_GROUP ** -0.5),
        'pool_scale': 1.0 + nrm(ks[4], (L, POOL_WIDTH), 0.02),
        'ssm_lam_re': -0.5 + nrm(ks[5], (L, SSM_N_GROUPS, SSM_STATE), 0.01),
        'ssm_lam_im': lam_im_base + nrm(ks[6], (L, SSM_N_GROUPS, SSM_STATE), 0.01),
        'ssm_log_dt': jax.random.uniform(ks[7], (L, SSM_N_GROUPS), F32, math.log(1e-3), math.log(1e-1)),
        'ssm_b_re': nrm(ks[8], (L, SSM_N_GROUPS, SSM_STATE, SSM_GROUP), (2 * SSM_GROUP) ** -0.5),
        'ssm_b_im': nrm(ks[9], (L, SSM_N_GROUPS, SSM_STATE, SSM_GROUP), (2 * SSM_GROUP) ** -0.5),
        'ssm_c_re': nrm(ks[10], (L, SSM_N_GROUPS, SSM_GROUP, SSM_STATE), (2 * SSM_STATE) ** -0.5),
        'ssm_c_im': nrm(ks[11], (L, SSM_N_GROUPS, SSM_GROUP, SSM_STATE), (2 * SSM_STATE) ** -0.5),
        'ssm_d': nrm(ks[12], (L, SSM_N_GROUPS, SSM_GROUP), 1.0),
        'ssm_glu_w': nrm(ks[13], (L, SSM_WIDTH, 2 * SSM_WIDTH), SSM_WIDTH ** -0.5),
        'w_out': nrm(ks[14], (L, MIX_WIDTH, D_MODEL), MIX_WIDTH ** -0.5 * DEEPNORM_BETA),
        'ln1_g': 1.0 + nrm(ks[15], (L, D_MODEL), 0.02),
        'ln1_b': nrm(ks[16], (L, D_MODEL), 0.02),
        'ffn_w_up': nrm(ks[17], (L, D_MODEL, 2 * D_FF), D_MODEL ** -0.5),
        'ffn_conv_w': nrm(ks[18], (L, CONV_WIDTH, 2 * D_FF), CONV_WIDTH ** -0.5),
        'ffn_conv_b': nrm(ks[19], (L, 2 * D_FF), 0.01),
        'ffn_w_down': nrm(ks[20], (L, D_FF, D_MODEL), D_FF ** -0.5 * DEEPNORM_BETA),
        'ln2_g': 1.0 + nrm(ks[21], (L, D_MODEL), 0.02),
        'ln2_b': nrm(ks[22], (L, D_MODEL), 0.02),
    }


def reference(x, w_in, attn_sinks, pool_w, pool_scale, ssm_lam_re, ssm_lam_im, ssm_log_dt,
              ssm_b_re, ssm_b_im, ssm_c_re, ssm_c_im, ssm_d, ssm_glu_w, w_out, ln1_g, ln1_b,
              ffn_w_up, ffn_conv_w, ffn_conv_b, ffn_w_down, ln2_g, ln2_b):
    bsz, s_len = x.shape[0], x.shape[1]
    pos = jnp.arange(s_len)
    o_k = ATTN_WIDTH
    o_v = o_k + KV_WIDTH
    o_p = o_v + KV_WIDTH
    o_s = o_p + POOL_WIDTH
    for l in range(DEPTH):
        h = jnp.einsum('bsd,de->bse', x, w_in[l])
        q = rope(h[..., :o_k].reshape(bsz, s_len, N_Q_HEADS, HEAD_DIM), pos)
        k = rope(h[..., o_k:o_v].reshape(bsz, s_len, N_KV_HEADS, HEAD_DIM), pos)
        v = h[..., o_v:o_p].reshape(bsz, s_len, N_KV_HEADS, HEAD_DIM)
        y_attn = sliding_window_attention(q, k, v, attn_sinks[l])
        y_pool = multiscale_pool(h[..., o_p:o_s], pool_w[l], pool_scale[l])
        y_ssm = s5_ssm(h[..., o_s:], ssm_lam_re[l], ssm_lam_im[l], ssm_log_dt[l],
                       ssm_b_re[l], ssm_b_im[l], ssm_c_re[l], ssm_c_im[l], ssm_d[l],
                       ssm_glu_w[l])
        mix = jnp.concatenate([y_attn, y_pool, y_ssm], -1)
        mix = jnp.einsum('bse,ed->bsd', mix, w_out[l])
        x = layer_norm(DEEPNORM_ALPHA * x + mix, ln1_g[l], ln1_b[l])
        f = conv_glu_ffn(x, ffn_w_up[l], ffn_conv_w[l], ffn_conv_b[l], ffn_w_down[l])
        x = layer_norm(DEEPNORM_ALPHA * x + f, ln2_g[l], ln2_b[l])
    return x
```

```python
import functools
import math

import jax
import jax.numpy as jnp
from jax import lax
from jax.experimental import pallas as pl
from jax.experimental.pallas import tpu as pltpu

F32 = jnp.float32
BF16 = jnp.bfloat16

D_MODEL = 2048
SEQ = 2048
HEAD_DIM = 64
N_Q_HEADS = 16
N_KV_HEADS = 4
Q_PER_KV = N_Q_HEADS // N_KV_HEADS
ATTN_WIDTH = N_Q_HEADS * HEAD_DIM
KV_WIDTH = N_KV_HEADS * HEAD_DIM
ATTN_BLOCK = 128
ROPE_THETA = 10000.0
POOL_WINDOWS = (2, 4, 8, 16)
POOL_GROUP = 128
POOL_WIDTH = 512
SSM_WIDTH = 512
SSM_GROUP = 16
SSM_N_GROUPS = 32
SSM_STATE = 64
SSM_CHUNK = 16
SSM_CW = SSM_CHUNK * SSM_GROUP
ROPE_WIDTH = ATTN_WIDTH + KV_WIDTH
IN_WIDTH = ATTN_WIDTH + 2 * KV_WIDTH + POOL_WIDTH + SSM_WIDTH
D_FF = 5504
FF_PAD = 5632
LN_EPS = 1e-5

LANES = 128
VMEM_LIMIT = 56 * 1024 * 1024


def _params(*sem):
    return pltpu.CompilerParams(dimension_semantics=sem, vmem_limit_bytes=VMEM_LIMIT)


IN_TM = 512


def _inproj_kernel(x_ref, w_ref, cos_ref, sa_ref, sb_ref, o_ref):
    acc = jnp.dot(x_ref[...].astype(BF16), w_ref[...], preferred_element_type=F32)

    @pl.when(pl.program_id(1) == 0)
    def _():
        cos, sa, sb = cos_ref[...], sa_ref[...], sb_ref[...]
        for c in range(ROPE_WIDTH // LANES):
            a = acc[:, c * LANES:(c + 1) * LANES]
            r = a * cos + pltpu.roll(a, LANES - 32, 1) * sa + pltpu.roll(a, 32, 1) * sb
            o_ref[:, c * LANES:(c + 1) * LANES] = r.astype(BF16)

    @pl.when(pl.program_id(1) != 0)
    def _():
        o_ref[...] = acc.astype(BF16)


def _in_proj(x, w, cos, sa, sb):
    m = x.shape[0]
    nseq = SEQ // IN_TM
    return pl.pallas_call(
        _inproj_kernel,
        out_shape=jax.ShapeDtypeStruct((m, IN_WIDTH), BF16),
        grid=(m // IN_TM, IN_WIDTH // ROPE_WIDTH),
        in_specs=[
            pl.BlockSpec((IN_TM, D_MODEL), lambda i, j: (i, 0)),
            pl.BlockSpec((D_MODEL, ROPE_WIDTH), lambda i, j: (0, j)),
            pl.BlockSpec((IN_TM, LANES), lambda i, j: (i % nseq, 0)),
            pl.BlockSpec((IN_TM, LANES), lambda i, j: (i % nseq, 0)),
            pl.BlockSpec((IN_TM, LANES), lambda i, j: (i % nseq, 0)),
        ],
        out_specs=pl.BlockSpec((IN_TM, ROPE_WIDTH), lambda i, j: (i, j)),
        compiler_params=_params("parallel", "arbitrary"),
        name="in_proj",
    )(x, w, cos, sa, sb)


def _rope_tables():
    half = HEAD_DIM // 2
    inv = ROPE_THETA ** (-jnp.arange(half, dtype=F32) / half)
    ang = jnp.arange(SEQ, dtype=F32)[:, None] * inv[None, :]
    cos, sin = jnp.cos(ang), jnp.sin(ang)
    zero = jnp.zeros_like(sin)
    reps = LANES // HEAD_DIM
    cos_t = jnp.tile(jnp.concatenate([cos, cos], -1), (1, reps))
    sa_t = jnp.tile(jnp.concatenate([-sin, zero], -1), (1, reps))
    sb_t = jnp.tile(jnp.concatenate([zero, sin], -1), (1, reps))
    return cos_t, sa_t, sb_t


ATTN_TQ = 512
ATTN_QB = ATTN_TQ // ATTN_BLOCK


def _attn_kernel(sink_ref, q_ref, kc_ref, vc_ref, kp_ref, vp_ref, o_ref, kbuf, vbuf):
    kbuf[0:ATTN_BLOCK, :] = kp_ref[...]
    kbuf[ATTN_BLOCK:, :] = kc_ref[...]
    vbuf[0:ATTN_BLOCK, :] = vp_ref[...]
    vbuf[ATTN_BLOCK:, :] = vc_ref[...]
    seq_start = pl.program_id(1) == 0

    row = lax.broadcasted_iota(jnp.int32, (ATTN_BLOCK, 2 * ATTN_BLOCK), 0)
    col = lax.broadcasted_iota(jnp.int32, (ATTN_BLOCK, 2 * ATTN_BLOCK), 1)
    dist = row + ATTN_BLOCK - col
    band = (dist >= 0) & (dist < ATTN_BLOCK)
    band_first = band & (jnp.logical_not(seq_start) | (col >= ATTN_BLOCK))

    for qb in range(ATTN_QB):
        valid = band_first if qb == 0 else band
        r0 = qb * ATTN_BLOCK
        for g in range(N_KV_HEADS):
            kg = kbuf[r0:r0 + 2 * ATTN_BLOCK, g * HEAD_DIM:(g + 1) * HEAD_DIM]
            vg = vbuf[r0:r0 + 2 * ATTN_BLOCK, g * HEAD_DIM:(g + 1) * HEAD_DIM]
            for hh in range(Q_PER_KV):
                h = g * Q_PER_KV + hh
                qh = q_ref[r0:r0 + ATTN_BLOCK, h * HEAD_DIM:(h + 1) * HEAD_DIM]
                s = lax.dot_general(qh, kg, (((1,), (1,)), ((), ())), preferred_element_type=F32)
                s = jnp.where(valid, s * (HEAD_DIM ** -0.5), -1e30)
                sink = sink_ref[h]
                m = jnp.maximum(s.max(-1, keepdims=True), sink)
                p = jnp.exp(s - m)
                denom = p.sum(-1, keepdims=True) + jnp.exp(sink - m)
                o = jnp.dot(p.astype(BF16), vg, preferred_element_type=F32) / denom
                o_ref[r0:r0 + ATTN_BLOCK, h * HEAD_DIM:(h + 1) * HEAD_DIM] = o.astype(BF16)


def _attention(h, sinks):
    m = h.shape[0]
    nq = SEQ // ATTN_TQ
    kcol = ATTN_WIDTH // KV_WIDTH
    vcol = kcol + 1

    def cur(col):
        return lambda b, i: (b * nq + i, col)

    def prev(col):
        return lambda b, i: (jnp.maximum((b * nq + i) * ATTN_QB - 1, 0), col)

    return pl.pallas_call(
        _attn_kernel,
        out_shape=jax.ShapeDtypeStruct((m, ATTN_WIDTH), BF16),
        grid=(m // SEQ, nq),
        in_specs=[
            pl.BlockSpec(memory_space=pltpu.SMEM),
            pl.BlockSpec((ATTN_TQ, ATTN_WIDTH), lambda b, i: (b * nq + i, 0)),
            pl.BlockSpec((ATTN_TQ, KV_WIDTH), cur(kcol)),
            pl.BlockSpec((ATTN_TQ, KV_WIDTH), cur(vcol)),
            pl.BlockSpec((ATTN_BLOCK, KV_WIDTH), prev(kcol)),
            pl.BlockSpec((ATTN_BLOCK, KV_WIDTH), prev(vcol)),
        ],
        out_specs=pl.BlockSpec((ATTN_TQ, ATTN_WIDTH), lambda b, i: (b * nq + i, 0)),
        scratch_shapes=[pltpu.VMEM((ATTN_TQ + ATTN_BLOCK, KV_WIDTH), BF16),
                        pltpu.VMEM((ATTN_TQ + ATTN_BLOCK, KV_WIDTH), BF16)],
        compiler_params=_params("parallel", "arbitrary"),
        name="swa_attention",
    )(sinks, h, h, h, h, h)


def _shift_rows(x, d, rows):
    return jnp.where(rows >= d, pltpu.roll(x, d, 0), 0.0)


def _pool_kernel(u_ref, w_ref, scale_ref, o_ref):
    gi = pl.program_id(1)
    u = u_ref[...].astype(F32)
    rows = lax.broadcasted_iota(jnp.int32, u.shape, 0)
    t1 = lax.broadcasted_iota(jnp.int32, (SEQ, 1), 0).astype(F32) + 1.0
    for idx, w in enumerate(POOL_WINDOWS):
        @pl.when(gi == idx)
        def _(w=w):
            s, d = u, 1
            while d < w:
                s = s + _shift_rows(s, d, rows)
                d *= 2
            mean = s / jnp.minimum(t1, float(w))
            y = jnp.dot((mean - u).astype(BF16), w_ref[0], preferred_element_type=F32)
            o_ref[...] = (y * scale_ref[...]).astype(BF16)


def _pool(h, pool_w, pool_scale):
    m = h.shape[0]
    ucol = (ATTN_WIDTH + 2 * KV_WIDTH) // POOL_GROUP
    return pl.pallas_call(
        _pool_kernel,
        out_shape=jax.ShapeDtypeStruct((m, POOL_WIDTH), BF16),
        grid=(m // SEQ, len(POOL_WINDOWS)),
        in_specs=[
            pl.BlockSpec((SEQ, POOL_GROUP), lambda b, g: (b, ucol + g)),
            pl.BlockSpec((1, POOL_GROUP, POOL_GROUP), lambda b, g: (g, 0, 0)),
            pl.BlockSpec((1, POOL_GROUP), lambda b, g: (0, g)),
        ],
        out_specs=pl.BlockSpec((SEQ, POOL_GROUP), lambda b, g: (b, g)),
        compiler_params=_params("parallel", "arbitrary"),
        name="multiscale_pool",
    )(h, pool_w, pool_scale)


SSM_SCAN_STEPS = 7


def _ssm_kernel(u_ref, km_ref, em_ref, cm_ref, ar_ref, ai_ref, o_ref):
    u = u_ref[0]
    y = jnp.dot(u, km_ref[0], preferred_element_type=F32)
    e = jnp.dot(u, em_ref[0], preferred_element_type=F32)
    nchunk = SEQ // SSM_CHUNK
    rows = lax.broadcasted_iota(jnp.int32, e.shape, 0) % nchunk
    for k in range(SSM_SCAN_STEPS):
        s = _shift_rows(e, 1 << k, rows)
        e = e + ar_ref[0, k:k + 1, :] * s + ai_ref[0, k:k + 1, :] * pltpu.roll(s, SSM_STATE, 1)
    zp = _shift_rows(e, 1, rows)
    y = y + jnp.dot(zp.astype(BF16), cm_ref[0], preferred_element_type=F32)
    o_ref[0] = jax.nn.gelu(y).astype(BF16)


def _ssm(u_g, km, em, cm, ar, ai):
    g, rows, _ = u_g.shape
    spec3 = lambda a, b: pl.BlockSpec((1, a, b), lambda i: (i, 0, 0))
    return pl.pallas_call(
        _ssm_kernel,
        out_shape=jax.ShapeDtypeStruct((g, rows, SSM_CW), BF16),
        grid=(g,),
        in_specs=[spec3(rows, SSM_CW), spec3(SSM_CW, SSM_CW), spec3(SSM_CW, 2 * SSM_STATE),
                  spec3(2 * SSM_STATE, SSM_CW), spec3(8, 2 * SSM_STATE), spec3(8, 2 * SSM_STATE)],
        out_specs=spec3(rows, SSM_CW),
        compiler_params=_params("parallel"),
        name="s5_chunked_ssm",
    )(u_g, km, em, cm, ar, ai)


def _cmul(xr, xi, yr, yi):
    return xr * yr - xi * yi, xr * yi + xi * yr


def _ssm_matrices(lam_re, lam_im, log_dt, b_re, b_im, c_re, c_im, d):
    hi = lax.Precision.HIGHEST
    lr, li = lam_re.astype(F32), lam_im.astype(F32)
    dt = jnp.exp(log_dt.astype(F32))[:, None]
    mag = jnp.exp(lr * dt)
    ar, ai = mag * jnp.cos(li * dt), mag * jnp.sin(li * dt)
    nr, ni = ar - 1.0, ai
    den = lr * lr + li * li
    zr = (nr * lr + ni * li) / den
    zi = (ni * lr - nr * li) / den
    br, bi = b_re.astype(F32), b_im.astype(F32)
    bbr = zr[..., None] * br - zi[..., None] * bi
    bbi = zr[..., None] * bi + zi[..., None] * br
    cr, ci = c_re.astype(F32), c_im.astype(F32)

    pr, pi = [jnp.ones_like(ar)], [jnp.zeros_like(ar)]
    for _ in range(SSM_CHUNK):
        nr_, ni_ = _cmul(pr[-1], pi[-1], ar, ai)
        pr.append(nr_)
        pi.append(ni_)
    pr, pi = jnp.stack(pr), jnp.stack(pi)

    T, G, H, P = SSM_CHUNK, SSM_N_GROUPS, SSM_GROUP, SSM_STATE
    car = cr[None] * pr[:, :, None, :] - ci[None] * pi[:, :, None, :]
    cai = cr[None] * pi[:, :, None, :] + ci[None] * pr[:, :, None, :]
    kall = (jnp.einsum('tghp,gpk->tghk', car[:T], bbr, precision=hi)
            - jnp.einsum('tghp,gpk->tghk', cai[:T], bbi, precision=hi))
    lag = jnp.arange(T)[None, :] - jnp.arange(T)[:, None]
    kt = jnp.where((lag >= 0)[:, :, None, None, None], kall[jnp.clip(lag, 0, T - 1)], 0.0)
    eye_t, eye_h = jnp.eye(T, dtype=F32), jnp.eye(H, dtype=F32)
    dterm = eye_t[:, :, None, None, None] * (d.astype(F32)[None, None, :, :, None] * eye_h[None, None, None])
    km = (kt + dterm).transpose(2, 0, 4, 1, 3).reshape(G, T * H, T * H)

    er = pr[T - 1::-1][:T, :, :, None] * bbr[None] - pi[T - 1::-1][:T, :, :, None] * bbi[None]
    ei = pr[T - 1::-1][:T, :, :, None] * bbi[None] + pi[T - 1::-1][:T, :, :, None] * bbr[None]
    em = jnp.concatenate([er.transpose(1, 0, 3, 2), ei.transpose(1, 0, 3, 2)], -1).reshape(G, T * H, 2 * P)

    cm = jnp.concatenate([car[1:].transpose(1, 3, 0, 2), -cai[1:].transpose(1, 3, 0, 2)], 1).reshape(G, 2 * P, T * H)

    sr, si = [pr[T]], [pi[T]]
    for _ in range(SSM_SCAN_STEPS - 1):
        nr_, ni_ = _cmul(sr[-1], si[-1], sr[-1], si[-1])
        sr.append(nr_)
        si.append(ni_)
    sr.append(jnp.zeros_like(ar))
    si.append(jnp.zeros_like(ar))
    sr, si = jnp.stack(sr, 1), jnp.stack(si, 1)
    ar_t = jnp.concatenate([sr, sr], -1)
    ai_t = jnp.concatenate([-si, si], -1)
    return km.astype(BF16), em.astype(BF16), cm.astype(BF16), ar_t, ai_t


GLU_TM = 1024


def _glu_kernel(y_ref, w_ref, o_ref):
    ab = jnp.dot(y_ref[...], w_ref[...], preferred_element_type=F32)
    o_ref[...] = (ab[:, :SSM_WIDTH] * jax.nn.sigmoid(ab[:, SSM_WIDTH:])).astype(BF16)


def _glu(y, w):
    m = y.shape[0]
    return pl.pallas_call(
        _glu_kernel,
        out_shape=jax.ShapeDtypeStruct((m, SSM_WIDTH), BF16),
        grid=(m // GLU_TM,),
        in_specs=[pl.BlockSpec((GLU_TM, SSM_WIDTH), lambda i: (i, 0)),
                  pl.BlockSpec((SSM_WIDTH, 2 * SSM_WIDTH), lambda i: (0, 0))],
        out_specs=pl.BlockSpec((GLU_TM, SSM_WIDTH), lambda i: (i, 0)),
        compiler_params=_params("parallel"),
        name="ssm_glu",
    )(y, w)


def _deepnorm_ln(x, branch, g, b, alpha):
    y = alpha * x + branch
    mu = y.mean(-1, keepdims=True)
    yc = y - mu
    var = jnp.square(yc).mean(-1, keepdims=True)
    return yc * lax.rsqrt(var + LN_EPS) * g + b


OUT_TM = 512


def _outproj_kernel(alpha, ya_ref, yp_ref, ys_ref, wa_ref, wp_ref, ws_ref, x_ref, g_ref, b_ref, o_ref):
    acc = jnp.dot(ya_ref[...], wa_ref[...], preferred_element_type=F32)
    acc += jnp.dot(yp_ref[...], wp_ref[...], preferred_element_type=F32)
    acc += jnp.dot(ys_ref[...], ws_ref[...], preferred_element_type=F32)
    o_ref[...] = _deepnorm_ln(x_ref[...], acc, g_ref[...], b_ref[...], alpha)


def _out_proj(ya, yp, ys, w, x, g, b, alpha):
    m = x.shape[0]
    row = lambda width: pl.BlockSpec((OUT_TM, width), lambda i: (i, 0))
    vec = pl.BlockSpec((1, D_MODEL), lambda i: (0, 0))
    return pl.pallas_call(
        functools.partial(_outproj_kernel, alpha),
        out_shape=jax.ShapeDtypeStruct((m, D_MODEL), F32),
        grid=(m // OUT_TM,),
        in_specs=[
            row(ATTN_WIDTH), row(POOL_WIDTH), row(SSM_WIDTH),
            pl.BlockSpec((ATTN_WIDTH, D_MODEL), lambda i: (0, 0)),
            pl.BlockSpec((POOL_WIDTH, D_MODEL), lambda i: (ATTN_WIDTH // POOL_WIDTH, 0)),
            pl.BlockSpec((SSM_WIDTH, D_MODEL), lambda i: ((ATTN_WIDTH + POOL_WIDTH) // SSM_WIDTH, 0)),
            row(D_MODEL), vec, vec,
        ],
        out_specs=row(D_MODEL),
        compiler_params=_params("parallel"),
        name="out_proj_ln",
    )(ya, yp, ys, w, w, w, x, g, b)


UP_TM = 1024
UP_TN = 512
HALO = 8


def _ffn_up_kernel(x_ref, xp_ref, wv_ref, wg_ref, cwv_ref, cwg_ref, cbv_ref, cbg_ref, o_ref, xe_ref):
    @pl.when(pl.program_id(1) == 0)
    def _():
        xe_ref[0:HALO, :] = xp_ref[...].astype(BF16)
        xe_ref[HALO:, :] = x_ref[...].astype(BF16)

    xe = xe_ref[...]
    seq_start = (pl.program_id(0) % (SEQ // UP_TM)) == 0
    rows = lax.broadcasted_iota(jnp.int32, (UP_TM, 1), 0)

    def conv(w_ref, cw_ref, cb_ref):
        hx = jnp.dot(xe, w_ref[...], preferred_element_type=F32)
        h0 = hx[HALO:, :]
        h1 = pltpu.roll(hx, 1, 0)[HALO:, :]
        h2 = pltpu.roll(hx, 2, 0)[HALO:, :]
        h1 = jnp.where(seq_start & (rows < 1), 0.0, h1)
        h2 = jnp.where(seq_start & (rows < 2), 0.0, h2)
        return cb_ref[...] + h2 * cw_ref[0:1, :] + h1 * cw_ref[1:2, :] + h0 * cw_ref[2:3, :]

    val = conv(wv_ref, cwv_ref, cbv_ref)
    gate = conv(wg_ref, cwg_ref, cbg_ref)
    o_ref[...] = (jax.nn.silu(gate) * val).astype(BF16)


def _ffn_up(x, wv, wg, cwv, cwg, cbv, cbg):
    m = x.shape[0]
    wspec = pl.BlockSpec((D_MODEL, UP_TN), lambda i, j: (0, j))
    cwspec = pl.BlockSpec((3, UP_TN), lambda i, j: (0, j))
    cbspec = pl.BlockSpec((1, UP_TN), lambda i, j: (0, j))
    return pl.pallas_call(
        _ffn_up_kernel,
        out_shape=jax.ShapeDtypeStruct((m, FF_PAD), BF16),
        grid=(m // UP_TM, FF_PAD // UP_TN),
        in_specs=[
            pl.BlockSpec((UP_TM, D_MODEL), lambda i, j: (i, 0)),
            pl.BlockSpec((HALO, D_MODEL), lambda i, j: (jnp.maximum(i * (UP_TM // HALO) - 1, 0), 0)),
            wspec, wspec, cwspec, cwspec, cbspec, cbspec,
        ],
        out_specs=pl.BlockSpec((UP_TM, UP_TN), lambda i, j: (i, j)),
        scratch_shapes=[pltpu.VMEM((HALO + UP_TM, D_MODEL), BF16)],
        compiler_params=_params("parallel", "arbitrary"),
        name="ffn_up_conv_gate",
    )(x, x, wv, wg, cwv, cwg, cbv, cbg)


DOWN_TM = 512
DOWN_TK = 512


def _ffn_down_kernel(alpha, a_ref, w_ref, x_ref, g_ref, b_ref, o_ref, acc_ref):
    k = pl.program_id(1)

    @pl.when(k == 0)
    def _():
        acc_ref[...] = jnp.zeros_like(acc_ref)

    acc_ref[...] += jnp.dot(a_ref[...], w_ref[...], preferred_element_type=F32)

    @pl.when(k == pl.num_programs(1) - 1)
    def _():
        o_ref[...] = _deepnorm_ln(x_ref[...], acc_ref[...], g_ref[...], b_ref[...], alpha)


def _ffn_down(act, w, x, g, b, alpha):
    m = x.shape[0]
    vec = pl.BlockSpec((1, D_MODEL), lambda i, k: (0, 0))
    return pl.pallas_call(
        functools.partial(_ffn_down_kernel, alpha),
        out_shape=jax.ShapeDtypeStruct((m, D_MODEL), F32),
        grid=(m // DOWN_TM, FF_PAD // DOWN_TK),
        in_specs=[
            pl.BlockSpec((DOWN_TM, DOWN_TK), lambda i, k: (i, k)),
            pl.BlockSpec((DOWN_TK, D_MODEL), lambda i, k: (k, 0)),
            pl.BlockSpec((DOWN_TM, D_MODEL), lambda i, k: (i, 0)),
            vec, vec,
        ],
        out_specs=pl.BlockSpec((DOWN_TM, D_MODEL), lambda i, k: (i, 0)),
        scratch_shapes=[pltpu.VMEM((DOWN_TM, D_MODEL), F32)],
        compiler_params=_params("parallel", "arbitrary"),
        name="ffn_down_ln",
    )(act, w, x, g, b)


def _pad_cols(a, width):
    return jnp.pad(a, ((0, 0), (0, width - a.shape[1])))


def kernel(x, w_in, attn_sinks, pool_w, pool_scale, ssm_lam_re, ssm_lam_im, ssm_log_dt, ssm_b_re, ssm_b_im,
           ssm_c_re, ssm_c_im, ssm_d, ssm_glu_w, w_out, ln1_g, ln1_b, ffn_w_up, ffn_conv_w, ffn_conv_b,
           ffn_w_down, ln2_g, ln2_b):
    bsz, s_len, _ = x.shape
    assert s_len == SEQ and x.shape[2] == D_MODEL
    depth = w_in.shape[0]
    alpha = (2 * depth) ** 0.25
    m = bsz * s_len
    nchunks = m // SSM_CHUNK
    cos_t, sa_t, sb_t = _rope_tables()
    xf = x.reshape(m, D_MODEL).astype(F32)

    for l in range(depth):
        h = _in_proj(xf, w_in[l].astype(BF16), cos_t, sa_t, sb_t)
        y_attn = _attention(h, attn_sinks[l].astype(F32))
        y_pool = _pool(h, pool_w[l].astype(BF16), pool_scale[l].astype(F32).reshape(1, POOL_WIDTH))

        km, em, cm, ar_t, ai_t = _ssm_matrices(ssm_lam_re[l], ssm_lam_im[l], ssm_log_dt[l], ssm_b_re[l],
                                               ssm_b_im[l], ssm_c_re[l], ssm_c_im[l], ssm_d[l])
        u = h[:, IN_WIDTH - SSM_WIDTH:]
        u_g = u.reshape(nchunks, SSM_CHUNK, SSM_N_GROUPS, SSM_GROUP).transpose(2, 0, 1, 3)
        u_g = u_g.reshape(SSM_N_GROUPS, nchunks, SSM_CW)
        y_g = _ssm(u_g, km, em, cm, ar_t, ai_t)
        y_s = y_g.reshape(SSM_N_GROUPS, nchunks, SSM_CHUNK, SSM_GROUP).transpose(1, 2, 0, 3).reshape(m, SSM_WIDTH)
        y_ssm = _glu(y_s, ssm_glu_w[l].astype(BF16))

        x1 = _out_proj(y_attn, y_pool, y_ssm, w_out[l].astype(BF16), xf,
                       ln1_g[l].astype(F32).reshape(1, D_MODEL), ln1_b[l].astype(F32).reshape(1, D_MODEL), alpha)

        wv = _pad_cols(ffn_w_up[l, :, :D_FF], FF_PAD).astype(BF16)
        wg = _pad_cols(ffn_w_up[l, :, D_FF:], FF_PAD).astype(BF16)
        cw = ffn_conv_w[l].astype(F32)
        cb = ffn_conv_b[l].astype(F32).reshape(1, 2 * D_FF)
        act = _ffn_up(x1, wv, wg, _pad_cols(cw[:, :D_FF], FF_PAD), _pad_cols(cw[:, D_FF:], FF_PAD),
                      _pad_cols(cb[:, :D_FF], FF_PAD), _pad_cols(cb[:, D_FF:], FF_PAD))
        wd = jnp.pad(ffn_w_down[l], ((0, FF_PAD - D_FF), (0, 0))).astype(BF16)
        xf = _ffn_down(act, wd, x1, ln2_g[l].astype(F32).reshape(1, D_MODEL),
                       ln2_b[l].astype(F32).reshape(1, D_MODEL), alpha)

    return xf.reshape(bsz, s_len, D_MODEL).astype(x.dtype)
```

```python
import functools

import jax
import jax.numpy as jnp
from jax import lax
from jax.experimental import pallas as pl
from jax.experimental.pallas import tpu as pltpu

F32 = jnp.float32
BF16 = jnp.bfloat16

D_MODEL = 2048
SEQ = 2048
HEAD_DIM = 64
N_Q_HEADS = 16
N_KV_HEADS = 4
ATTN_WIDTH = N_Q_HEADS * HEAD_DIM
KV_WIDTH = N_KV_HEADS * HEAD_DIM
ATTN_BLOCK = 128
ROPE_THETA = 10000.0
POOL_WINDOWS = (2, 4, 8, 16)
POOL_GROUP = 128
POOL_WIDTH = 512
SSM_WIDTH = 512
SSM_GROUP = 16
SSM_N_GROUPS = 32
SSM_STATE = 64
SSM_CHUNK = 16
LN_EPS = 1e-5
D_FF = 5504

LANES = 128
SSM_BUNDLE = LANES // SSM_GROUP
SSM_N_BUNDLES = SSM_N_GROUPS // SSM_BUNDLE
SSM_BW = SSM_CHUNK * LANES
SSM_SW = SSM_BUNDLE * 2 * SSM_STATE
HALF_WIDTH = ATTN_WIDTH + KV_WIDTH
KV2_WIDTH = 2 * KV_WIDTH
VMEM_LIMIT = 56 * 1024 * 1024


def _params(*sem):
    return pltpu.CompilerParams(dimension_semantics=sem, vmem_limit_bytes=VMEM_LIMIT)


def _resident(block, index_map):
    return pl.BlockSpec(block, index_map, pipeline_mode=pl.Buffered(1))


def _cast_kernel(x_ref, o_ref):
    o_ref[...] = x_ref[...].astype(BF16)


def _cast_bf16(w, block_rows):
    rows, cols = w.shape
    return pl.pallas_call(
        _cast_kernel,
        out_shape=jax.ShapeDtypeStruct((rows, cols), BF16),
        grid=(rows // block_rows,),
        in_specs=[pl.BlockSpec((block_rows, cols), lambda i: (i, 0))],
        out_specs=pl.BlockSpec((block_rows, cols), lambda i: (i, 0)),
        compiler_params=_params("parallel"),
        name="cast_bf16",
    )(w)


IN_TM = 512
CAST_ROWS = 256


def _cast_weight_once(w_ref, wbf_ref):
    @pl.when(pl.program_id(0) == 0)
    def _():
        for r in range(0, w_ref.shape[0], CAST_ROWS):
            wbf_ref[r:r + CAST_ROWS, :] = w_ref[r:r + CAST_ROWS, :].astype(BF16)


def _dup_heads(pair):
    lo = lax.broadcasted_iota(jnp.int32, pair.shape, 1) < HEAD_DIM
    swapped = pltpu.roll(pair, HEAD_DIM, 1)
    return jnp.where(lo, pair, swapped), jnp.where(lo, swapped, pair)


def _inproj_qk_kernel(x_ref, w_ref, cos_ref, sa_ref, sb_ref, q_ref, k_ref, wbf_ref):
    _cast_weight_once(w_ref, wbf_ref)
    acc = jnp.dot(x_ref[...].astype(BF16), wbf_ref[...], preferred_element_type=F32)
    cos, sa, sb = cos_ref[...], sa_ref[...], sb_ref[...]
    nq = ATTN_WIDTH // LANES
    for c in range(HALF_WIDTH // LANES):
        a = acc[:, c * LANES:(c + 1) * LANES]
        r = a * cos + pltpu.roll(a, LANES - 32, 1) * sa + pltpu.roll(a, 32, 1) * sb
        if c < nq:
            q_ref[:, c * LANES:(c + 1) * LANES] = (r * (HEAD_DIM ** -0.5)).astype(BF16)
        else:
            ka, kb = _dup_heads(r)
            g = 2 * (c - nq)
            k_ref[:, g * LANES:(g + 1) * LANES] = ka.astype(BF16)
            k_ref[:, (g + 1) * LANES:(g + 2) * LANES] = kb.astype(BF16)


def _inproj_vps_kernel(x_ref, w_ref, v_ref, p_ref, s_ref, wbf_ref):
    _cast_weight_once(w_ref, wbf_ref)
    acc = jnp.dot(x_ref[...].astype(BF16), wbf_ref[...], preferred_element_type=F32)
    for c in range(KV_WIDTH // LANES):
        va, vb = _dup_heads(acc[:, c * LANES:(c + 1) * LANES])
        v_ref[:, 2 * c * LANES:(2 * c + 1) * LANES] = va.astype(BF16)
        v_ref[:, (2 * c + 1) * LANES:(2 * c + 2) * LANES] = vb.astype(BF16)
    p_ref[...] = acc[:, KV_WIDTH:KV_WIDTH + POOL_WIDTH].astype(BF16)
    for b in range(SSM_N_BUNDLES):
        c0 = KV_WIDTH + POOL_WIDTH + b * LANES
        s_ref[b] = acc[:, c0:c0 + LANES].astype(BF16)


def _in_proj(x, w_in2d, layer, cos, sa, sb):
    m = x.shape[0]
    nseq = SEQ // IN_TM
    x_spec = pl.BlockSpec((IN_TM, D_MODEL), lambda i: (i, 0))
    tab = pl.BlockSpec((IN_TM, LANES), lambda i: (i % nseq, 0))
    row = lambda width: pl.BlockSpec((IN_TM, width), lambda i: (i, 0))
    wscratch = [pltpu.VMEM((D_MODEL, HALF_WIDTH), BF16)]
    q, k2 = pl.pallas_call(
        _inproj_qk_kernel,
        out_shape=(jax.ShapeDtypeStruct((m, ATTN_WIDTH), BF16), jax.ShapeDtypeStruct((m, KV2_WIDTH), BF16)),
        grid=(m // IN_TM,),
        in_specs=[x_spec, _resident((D_MODEL, HALF_WIDTH), lambda i: (layer, 0)), tab, tab, tab],
        out_specs=(row(ATTN_WIDTH), row(KV2_WIDTH)),
        scratch_shapes=wscratch,
        compiler_params=_params("arbitrary"),
        name="in_proj_qk",
    )(x, w_in2d, cos, sa, sb)
    v2, pu, su = pl.pallas_call(
        _inproj_vps_kernel,
        out_shape=(jax.ShapeDtypeStruct((m, KV2_WIDTH), BF16), jax.ShapeDtypeStruct((m, POOL_WIDTH), BF16),
                   jax.ShapeDtypeStruct((SSM_N_BUNDLES, m, LANES), BF16)),
        grid=(m // IN_TM,),
        in_specs=[x_spec, _resident((D_MODEL, HALF_WIDTH), lambda i: (layer, 1))],
        out_specs=(row(KV2_WIDTH), row(POOL_WIDTH),
                   pl.BlockSpec((SSM_N_BUNDLES, IN_TM, LANES), lambda i: (0, i, 0))),
        scratch_shapes=wscratch,
        compiler_params=_params("arbitrary"),
        name="in_proj_vps",
    )(x, w_in2d)
    return q, k2, v2, pu, su


def _rope_tables():
    half = HEAD_DIM // 2
    inv = ROPE_THETA ** (-jnp.arange(half, dtype=F32) / half)
    ang = jnp.arange(SEQ, dtype=F32)[:, None] * inv[None, :]
    cos, sin = jnp.cos(ang), jnp.sin(ang)
    zero = jnp.zeros_like(sin)
    reps = LANES // HEAD_DIM
    cos_t = jnp.tile(jnp.concatenate([cos, cos], -1), (1, reps))
    sa_t = jnp.tile(jnp.concatenate([-sin, zero], -1), (1, reps))
    sb_t = jnp.tile(jnp.concatenate([zero, sin], -1), (1, reps))
    return cos_t, sa_t, sb_t


ATTN_TQ = 512
ATTN_QB = ATTN_TQ // ATTN_BLOCK
KEYS = 2 * ATTN_BLOCK


def _attn_kernel(sink_ref, q_ref, kc_ref, vc_ref, kp_ref, vp_ref, o_ref, kbuf, vbuf):
    kbuf[0:ATTN_BLOCK, :] = kp_ref[...]
    kbuf[ATTN_BLOCK:, :] = kc_ref[...]
    vbuf[0:ATTN_BLOCK, :] = vp_ref[...]
    vbuf[ATTN_BLOCK:, :] = vc_ref[...]
    seq_start = pl.program_id(1) == 0

    row = lax.broadcasted_iota(jnp.int32, (ATTN_BLOCK, KEYS), 0)
    col = lax.broadcasted_iota(jnp.int32, (ATTN_BLOCK, KEYS), 1)
    dist = row + ATTN_BLOCK - col
    band = (dist >= 0) & (dist < ATTN_BLOCK)
    band_first = band & (jnp.logical_not(seq_start) | (col >= ATTN_BLOCK))
    lo_kv = lax.broadcasted_iota(jnp.int32, (KEYS, LANES), 1) < HEAD_DIM
    lo_out = lax.broadcasted_iota(jnp.int32, (ATTN_BLOCK, LANES), 1) < HEAD_DIM
    zero_kv = jnp.zeros((KEYS, LANES), BF16)

    def split_heads(x):
        return jnp.concatenate([jnp.where(lo_kv, x, zero_kv), jnp.where(lo_kv, zero_kv, x)], axis=0)

    for qb in range(ATTN_QB):
        valid = band_first if qb == 0 else band
        r0 = qb * ATTN_BLOCK
        for g in range(N_KV_HEADS):
            k2 = split_heads(kbuf[r0:r0 + KEYS, g * LANES:(g + 1) * LANES])
            v2 = split_heads(vbuf[r0:r0 + KEYS, g * LANES:(g + 1) * LANES])
            for pr in range(2):
                c0 = (2 * g + pr) * LANES
                qp = q_ref[r0:r0 + ATTN_BLOCK, c0:c0 + LANES]
                s2 = lax.dot_general(qp, k2, (((1,), (1,)), ((), ())), preferred_element_type=F32)
                ps, dens = [], []
                for hh in range(2):
                    sink = sink_ref[4 * g + 2 * pr + hh]
                    s = jnp.where(valid, s2[:, hh * KEYS:(hh + 1) * KEYS], -1e30)
                    mx = jnp.maximum(s.max(-1, keepdims=True), sink)
                    p = jnp.exp(s - mx)
                    dens.append(p.sum(-1, keepdims=True) + jnp.exp(sink - mx))
                    ps.append(p.astype(BF16))
                o = jnp.dot(jnp.concatenate(ps, axis=1), v2, preferred_element_type=F32)
                o = o / jnp.where(lo_out, dens[0], dens[1])
                o_ref[r0:r0 + ATTN_BLOCK, c0:c0 + LANES] = o.astype(BF16)


def _attention(q, k2, v2, sinks):
    m = q.shape[0]
    nq = SEQ // ATTN_TQ
    cur = lambda b, i: (b * nq + i, 0)
    prev = lambda b, i: (jnp.maximum((b * nq + i) * ATTN_QB - 1, 0), 0)
    return pl.pallas_call(
        _attn_kernel,
        out_shape=jax.ShapeDtypeStruct((m, ATTN_WIDTH), BF16),
        grid=(m // SEQ, nq),
        in_specs=[
            pl.BlockSpec(memory_space=pltpu.SMEM),
            pl.BlockSpec((ATTN_TQ, ATTN_WIDTH), cur),
            pl.BlockSpec((ATTN_TQ, KV2_WIDTH), cur),
            pl.BlockSpec((ATTN_TQ, KV2_WIDTH), cur),
            pl.BlockSpec((ATTN_BLOCK, KV2_WIDTH), prev),
            pl.BlockSpec((ATTN_BLOCK, KV2_WIDTH), prev),
        ],
        out_specs=pl.BlockSpec((ATTN_TQ, ATTN_WIDTH), cur),
        scratch_shapes=[pltpu.VMEM((ATTN_TQ + ATTN_BLOCK, KV2_WIDTH), BF16),
                        pltpu.VMEM((ATTN_TQ + ATTN_BLOCK, KV2_WIDTH), BF16)],
        compiler_params=_params("parallel", "arbitrary"),
        name="swa_attention",
    )(sinks, q, k2, v2, k2, v2)


def _shift_rows(x, d, rows):
    return jnp.where(rows >= d, pltpu.roll(x, d, 0), 0.0)


def _pool_kernel(u_ref, w_ref, scale_ref, o_ref):
    gi = pl.program_id(1)
    u = u_ref[...].astype(F32)
    rows = lax.broadcasted_iota(jnp.int32, u.shape, 0)
    t1 = lax.broadcasted_iota(jnp.int32, (SEQ, 1), 0).astype(F32) + 1.0
    for idx, w in enumerate(POOL_WINDOWS):
        @pl.when(gi == idx)
        def _(w=w):
            s, d = u, 1
            while d < w:
                s = s + _shift_rows(s, d, rows)
                d *= 2
            mean = s / jnp.minimum(t1, float(w))
            y = jnp.dot((mean - u).astype(BF16), w_ref[0].astype(BF16), preferred_element_type=F32)
            o_ref[...] = (y * scale_ref[...]).astype(BF16)


def _pool(pu, pool_w3d, layer, pool_scale_l):
    m = pu.shape[0]
    return pl.pallas_call(
        _pool_kernel,
        out_shape=jax.ShapeDtypeStruct((m, POOL_WIDTH), BF16),
        grid=(m // SEQ, len(POOL_WINDOWS)),
        in_specs=[
            pl.BlockSpec((SEQ, POOL_GROUP), lambda b, g: (b, g)),
            pl.BlockSpec((1, POOL_GROUP, POOL_GROUP), lambda b, g: (layer * len(POOL_WINDOWS) + g, 0, 0)),
            pl.BlockSpec((1, POOL_GROUP), lambda b, g: (0, g)),
        ],
        out_specs=pl.BlockSpec((SEQ, POOL_GROUP), lambda b, g: (b, g)),
        compiler_params=_params("parallel", "arbitrary"),
        name="multiscale_pool",
    )(pu, pool_w3d, pool_scale_l)


SSM_SCAN_STEPS = 7
SSM_COLS = 2 * LANES


def _ssm_kernel(u_ref, bd_ref, em_ref, cm_ref, ar_ref, ai_ref, o_ref, km_ref):
    T = SSM_CHUNK
    for i in range(T):
        for j in range(min(i | 1, T - 1) + 1):
            blk = bd_ref[0, i - j] if j <= i else jnp.zeros((LANES, LANES), BF16)
            km_ref[j * LANES:(j + 1) * LANES, i * LANES:(i + 1) * LANES] = blk

    u = u_ref[0]
    e = jnp.dot(u, em_ref[0], preferred_element_type=F32)
    nchunk = SEQ // SSM_CHUNK
    rows = lax.broadcasted_iota(jnp.int32, (e.shape[0], LANES), 0) % nchunk
    zprev = []
    for g in range(SSM_BUNDLE):
        eg = e[:, g * LANES:(g + 1) * LANES]
        for k in range(SSM_SCAN_STEPS):
            s = _shift_rows(eg, 1 << k, rows)
            sl = slice(g * LANES, (g + 1) * LANES)
            eg = eg + ar_ref[0, k:k + 1, sl] * s + ai_ref[0, k:k + 1, sl] * pltpu.roll(s, SSM_STATE, 1)
        zprev.append(_shift_rows(eg, 1, rows).astype(BF16))
    zp = jnp.concatenate(zprev, axis=1)

    for mblk in range(SSM_BW // SSM_COLS):
        c0, kdim = mblk * SSM_COLS, (mblk + 1) * SSM_COLS
        y = jnp.dot(u[:, :kdim], km_ref[0:kdim, c0:c0 + SSM_COLS], preferred_element_type=F32)
        y = y + jnp.dot(zp, cm_ref[0, :, c0:c0 + SSM_COLS], preferred_element_type=F32)
        o_ref[0, :, c0:c0 + SSM_COLS] = jax.nn.gelu(y).astype(BF16)


def _ssm(u8, bd, em, cm, ar, ai):
    nb, rows, _ = u8.shape
    spec = lambda *dims: pl.BlockSpec((1,) + dims, lambda i: (i,) + (0,) * len(dims))
    return pl.pallas_call(
        _ssm_kernel,
        out_shape=jax.ShapeDtypeStruct((nb, rows, SSM_BW), BF16),
        grid=(nb,),
        in_specs=[spec(rows, SSM_BW), spec(SSM_CHUNK, LANES, LANES), spec(SSM_BW, SSM_SW), spec(SSM_SW, SSM_BW),
                  spec(8, SSM_SW), spec(8, SSM_SW)],
        out_specs=spec(rows, SSM_BW),
        scratch_shapes=[pltpu.VMEM((SSM_BW, SSM_BW), BF16)],
        compiler_params=_params("parallel"),
        name="s5_chunked_ssm",
    )(u8, bd, em, cm, ar, ai)


def _cmul(xr, xi, yr, yi):
    return xr * yr - xi * yi, xr * yi + xi * yr


def _ssm_matrices(lam_re, lam_im, log_dt, b_re, b_im, c_re, c_im, d):
    hi = lax.Precision.HIGHEST
    lr, li = lam_re.astype(F32), lam_im.astype(F32)
    dt = jnp.exp(log_dt.astype(F32))[:, None]
    mag = jnp.exp(lr * dt)
    ar, ai = mag * jnp.cos(li * dt), mag * jnp.sin(li * dt)
    nr, ni = ar - 1.0, ai
    den = lr * lr + li * li
    zr = (nr * lr + ni * li) / den
    zi = (ni * lr - nr * li) / den
    br, bi = b_re.astype(F32), b_im.astype(F32)
    bbr = zr[..., None] * br - zi[..., None] * bi
    bbi = zr[..., None] * bi + zi[..., None] * br
    cr, ci = c_re.astype(F32), c_im.astype(F32)

    pr, pi = [jnp.ones_like(ar)], [jnp.zeros_like(ar)]
    for _ in range(SSM_CHUNK):
        nr_, ni_ = _cmul(pr[-1], pi[-1], ar, ai)
        pr.append(nr_)
        pi.append(ni_)
    prr, pir = jnp.stack(pr[SSM_CHUNK - 1::-1]), jnp.stack(pi[SSM_CHUNK - 1::-1])
    pr, pi = jnp.stack(pr), jnp.stack(pi)

    T, G, H, P = SSM_CHUNK, SSM_N_GROUPS, SSM_GROUP, SSM_STATE
    NB, GB = SSM_N_BUNDLES, SSM_BUNDLE
    eye_g = jnp.eye(GB, dtype=F32)
    car = cr[None] * pr[:, :, None, :] - ci[None] * pi[:, :, None, :]
    cai = cr[None] * pi[:, :, None, :] + ci[None] * pr[:, :, None, :]
    kall = (jnp.einsum('tghp,gpk->tghk', car[:T], bbr, precision=hi)
            - jnp.einsum('tghp,gpk->tghk', cai[:T], bbi, precision=hi))
    kall = kall.at[0].add(d.astype(F32)[:, :, None] * jnp.eye(H, dtype=F32))
    kb = kall.reshape(T, NB, GB, H, H).transpose(1, 0, 2, 4, 3)
    bd = kb[:, :, :, :, None, :] * eye_g[None, None, :, None, :, None]
    bd = bd.reshape(NB, T, GB * H, GB * H)

    er = prr[:, :, :, None] * bbr[None] - pir[:, :, :, None] * bbi[None]
    ei = prr[:, :, :, None] * bbi[None] + pir[:, :, :, None] * bbr[None]
    eg = jnp.concatenate([er.transpose(1, 0, 3, 2), ei.transpose(1, 0, 3, 2)], -1)
    eg = eg.reshape(NB, GB, T, H, 2 * P).transpose(0, 2, 1, 3, 4)
    em = eg[:, :, :, :, None, :] * eye_g[None, None, :, None, :, None]
    em = em.reshape(NB, T * GB * H, GB * 2 * P)

    cg = jnp.concatenate([car[1:].transpose(1, 3, 0, 2), -cai[1:].transpose(1, 3, 0, 2)], 1)
    cg = cg.reshape(NB, GB, 2 * P, T, H)
    cm = cg[:, :, :, :, None, :] * eye_g[None, :, None, None, :, None]
    cm = cm.reshape(NB, GB * 2 * P, T * GB * H)

    sr, si = [pr[T]], [pi[T]]
    for _ in range(SSM_SCAN_STEPS - 1):
        nr_, ni_ = _cmul(sr[-1], si[-1], sr[-1], si[-1])
        sr.append(nr_)
        si.append(ni_)
    sr.append(jnp.zeros_like(ar))
    si.append(jnp.zeros_like(ar))
    sr, si = jnp.stack(sr, 1), jnp.stack(si, 1)
    lay = lambda t: t.reshape(NB, GB, 8, 2 * P).transpose(0, 2, 1, 3).reshape(NB, 8, GB * 2 * P)
    ar_t = lay(jnp.concatenate([sr, sr], -1))
    ai_t = lay(jnp.concatenate([-si, si], -1))
    return bd.astype(BF16), em.astype(BF16), cm.astype(BF16), ar_t, ai_t


def _deepnorm_ln(x, branch, g, b, alpha):
    y = alpha * x + branch
    mu = y.mean(-1, keepdims=True)
    yc = y - mu
    var = jnp.square(yc).mean(-1, keepdims=True)
    return yc * lax.rsqrt(var + LN_EPS) * g + b


OUT_TM = 512


def _outproj_kernel(alpha, ya_ref, yp_ref, ys_ref, glu_ref, w_ref, x_ref, g_ref, b_ref, o_ref):
    ys = jnp.concatenate([ys_ref[b] for b in range(SSM_N_BUNDLES)], axis=1)
    ab = jnp.dot(ys, glu_ref[...].astype(BF16), preferred_element_type=F32)
    y_ssm = (ab[:, :SSM_WIDTH] * jax.nn.sigmoid(ab[:, SSM_WIDTH:])).astype(BF16)
    o_p, o_s = ATTN_WIDTH, ATTN_WIDTH + POOL_WIDTH
    acc = jnp.dot(ya_ref[...], w_ref[0:o_p, :], preferred_element_type=F32)
    acc += jnp.dot(yp_ref[...], w_ref[o_p:o_s, :], preferred_element_type=F32)
    acc += jnp.dot(y_ssm, w_ref[o_s:, :], preferred_element_type=F32)
    o_ref[...] = _deepnorm_ln(x_ref[...], acc, g_ref[...], b_ref[...], alpha)


def _out_proj(ya, yp, ys8, glu_w2d, w_out_bf, layer, x, g, b, alpha):
    m = x.shape[0]
    row = lambda width: pl.BlockSpec((OUT_TM, width), lambda i: (i, 0))
    vec = pl.BlockSpec((1, D_MODEL), lambda i: (0, 0))
    return pl.pallas_call(
        functools.partial(_outproj_kernel, alpha),
        out_shape=jax.ShapeDtypeStruct((m, D_MODEL), F32),
        grid=(m // OUT_TM,),
        in_specs=[
            row(ATTN_WIDTH), row(POOL_WIDTH),
            pl.BlockSpec((SSM_N_BUNDLES, OUT_TM, LANES), lambda i: (0, i, 0)),
            pl.BlockSpec((SSM_WIDTH, 2 * SSM_WIDTH), lambda i: (layer, 0)),
            _resident((D_MODEL, D_MODEL), lambda i: (layer, 0)),
            row(D_MODEL), vec, vec,
        ],
        out_specs=row(D_MODEL),
        compiler_params=_params("parallel"),
        name="out_proj_ln",
    )(ya, yp, ys8, glu_w2d, w_out_bf, x, g, b)


UP_TM = 1024
UP_TN = 512
UP_NJ = -(-D_FF // UP_TN)
FF_PAD = UP_NJ * UP_TN
UP_SHIFT = FF_PAD - D_FF
HALO = 8
CONV_ROWS = 8


def _ffn_up_kernel(x_ref, xp_ref, wv_ref, wg_ref, cv_ref, cg_ref, o_ref, xe_ref):
    @pl.when(pl.program_id(1) == 0)
    def _():
        xe_ref[0:HALO, :] = xp_ref[...].astype(BF16)
        xe_ref[HALO:, :] = x_ref[...].astype(BF16)

    xe = xe_ref[...]
    seq_start = (pl.program_id(0) % (SEQ // UP_TM)) == 0
    rows = lax.broadcasted_iota(jnp.int32, (UP_TM, 1), 0)

    def conv(w_ref, c_ref):
        hx = jnp.dot(xe, w_ref[...], preferred_element_type=F32)
        h0 = hx[HALO:, :]
        h1 = pltpu.roll(hx, 1, 0)[HALO:, :]
        h2 = pltpu.roll(hx, 2, 0)[HALO:, :]
        h1 = jnp.where(seq_start & (rows < 1), 0.0, h1)
        h2 = jnp.where(seq_start & (rows < 2), 0.0, h2)
        return c_ref[3:4, :] + h2 * c_ref[0:1, :] + h1 * c_ref[1:2, :] + h0 * c_ref[2:3, :]

    val = conv(wv_ref, cv_ref)
    gate = conv(wg_ref, cg_ref)
    act = (jax.nn.silu(gate) * val).astype(BF16)
    last = pl.num_programs(1) - 1

    @pl.when(pl.program_id(1) != last)
    def _():
        o_ref[...] = act

    @pl.when(pl.program_id(1) == last)
    def _():
        o_ref[...] = jnp.concatenate([act[:, UP_SHIFT:], jnp.zeros((UP_TM, UP_SHIFT), BF16)], axis=1)


def _ffn_up(x, w_up_bf, layer, conv8):
    m = x.shape[0]
    col = lambda j: jnp.minimum(j * UP_TN, D_FF - UP_TN)
    window = lambda rows, r0, c0: pl.BlockSpec((pl.Element(rows), pl.Element(UP_TN)),
                                               lambda i, j: (r0(i), pl.multiple_of(c0 + col(j), LANES)))
    wrow = lambda i: layer * D_MODEL
    return pl.pallas_call(
        _ffn_up_kernel,
        out_shape=jax.ShapeDtypeStruct((m, FF_PAD), BF16),
        grid=(m // UP_TM, UP_NJ),
        in_specs=[
            pl.BlockSpec((UP_TM, D_MODEL), lambda i, j: (i, 0)),
            pl.BlockSpec((HALO, D_MODEL), lambda i, j: (jnp.maximum(i * (UP_TM // HALO) - 1, 0), 0)),
            window(D_MODEL, wrow, 0), window(D_MODEL, wrow, D_FF),
            window(CONV_ROWS, lambda i: layer * CONV_ROWS, 0), window(CONV_ROWS, lambda i: layer * CONV_ROWS, D_FF),
        ],
        out_specs=pl.BlockSpec((UP_TM, UP_TN), lambda i, j: (i, j)),
        scratch_shapes=[pltpu.VMEM((HALO + UP_TM, D_MODEL), BF16)],
        compiler_params=_params("parallel", "arbitrary"),
        name="ffn_up_conv_gate",
    )(x, x, w_up_bf, w_up_bf, conv8, conv8)


DOWN_TM = 256


def _ffn_down_kernel(alpha, a_ref, w_ref, x_ref, g_ref, b_ref, o_ref):
    f = jnp.dot(a_ref[...], w_ref[...], preferred_element_type=F32)
    o_ref[...] = _deepnorm_ln(x_ref[...], f, g_ref[...], b_ref[...], alpha)


def _ffn_down(act, w_down_bf, layer, x, g, b, alpha):
    m = x.shape[0]
    vec = pl.BlockSpec((1, D_MODEL), lambda i: (0, 0))
    return pl.pallas_call(
        functools.partial(_ffn_down_kernel, alpha),
        out_shape=jax.ShapeDtypeStruct((m, D_MODEL), F32),
        grid=(m // DOWN_TM,),
        in_specs=[
            pl.BlockSpec((DOWN_TM, D_FF), lambda i: (i, 0)),
            _resident((D_FF, D_MODEL), lambda i: (layer, 0)),
            pl.BlockSpec((DOWN_TM, D_MODEL), lambda i: (i, 0)),
            vec, vec,
        ],
        out_specs=pl.BlockSpec((DOWN_TM, D_MODEL), lambda i: (i, 0)),
        compiler_params=_params("parallel"),
        name="ffn_down_ln",
    )(act, w_down_bf, x, g, b)


def kernel(x, w_in, attn_sinks, pool_w, pool_scale, ssm_lam_re, ssm_lam_im, ssm_log_dt, ssm_b_re, ssm_b_im,
           ssm_c_re, ssm_c_im, ssm_d, ssm_glu_w, w_out, ln1_g, ln1_b, ffn_w_up, ffn_conv_w, ffn_conv_b,
           ffn_w_down, ln2_g, ln2_b):
    bsz, s_len, _ = x.shape
    assert s_len == SEQ and x.shape[2] == D_MODEL
    depth = w_in.shape[0]
    alpha = (2 * depth) ** 0.25
    m = bsz * s_len
    nchunks = m // SSM_CHUNK
    cos_t, sa_t, sb_t = _rope_tables()
    xf = x.reshape(m, D_MODEL).astype(F32)

    w_out_bf = _cast_bf16(w_out.astype(F32).reshape(depth * D_MODEL, D_MODEL), 1024)
    w_up_bf = _cast_bf16(ffn_w_up.astype(F32).reshape(depth * D_MODEL, 2 * D_FF), 256)
    w_down_bf = _cast_bf16(ffn_w_down.astype(F32).reshape(depth * D_FF, D_MODEL), D_FF // 8)
    conv8 = jnp.concatenate([ffn_conv_w.astype(F32), ffn_conv_b.astype(F32)[:, None, :],
                             jnp.zeros((depth, CONV_ROWS - 4, 2 * D_FF), F32)], axis=1)
    conv8 = conv8.reshape(depth * CONV_ROWS, 2 * D_FF)
    w_in2d = w_in.astype(F32).reshape(depth * D_MODEL, 2 * HALF_WIDTH)
    glu_w2d = ssm_glu_w.astype(F32).reshape(depth * SSM_WIDTH, 2 * SSM_WIDTH)
    pool_w3d = pool_w.astype(F32).reshape(depth * len(POOL_WINDOWS), POOL_GROUP, POOL_GROUP)
    vec = lambda a: a.astype(F32).reshape(1, -1)

    for l in range(depth):
        q, k2, v2, pu, su = _in_proj(xf, w_in2d, l, cos_t, sa_t, sb_t)
        y_attn = _attention(q, k2, v2, attn_sinks[l].astype(F32))
        y_pool = _pool(pu, pool_w3d, l, vec(pool_scale[l]))

        bd, em, cm, ar_t, ai_t = _ssm_matrices(ssm_lam_re[l], ssm_lam_im[l], ssm_log_dt[l], ssm_b_re[l],
                                               ssm_b_im[l], ssm_c_re[l], ssm_c_im[l], ssm_d[l])
        y8 = _ssm(su.reshape(SSM_N_BUNDLES, nchunks, SSM_BW), bd, em, cm, ar_t, ai_t)
        y8 = y8.reshape(SSM_N_BUNDLES, m, LANES)

        x1 = _out_proj(y_attn, y_pool, y8, glu_w2d, w_out_bf, l, xf,
                       vec(ln1_g[l]), vec(ln1_b[l]), alpha)
        act = _ffn_up(x1, w_up_bf, l, conv8)
        xf = _ffn_down(act, w_down_bf, l, x1, vec(ln2_g[l]), vec(ln2_b[l]), alpha)

    return xf.reshape(bsz, s_len, D_MODEL).astype(x.dtype)
```

```python
import functools

import jax
import numpy as np
import jax.numpy as jnp
from jax import lax
from jax.experimental import pallas as pl
from jax.experimental.pallas import tpu as pltpu

F32 = jnp.float32
BF16 = jnp.bfloat16

D_MODEL = 2048
SEQ = 2048
HEAD_DIM = 64
N_Q_HEADS = 16
N_KV_HEADS = 4
ATTN_WIDTH = N_Q_HEADS * HEAD_DIM
KV_WIDTH = N_KV_HEADS * HEAD_DIM
ATTN_BLOCK = 128
ROPE_THETA = 10000.0
POOL_WINDOWS = (2, 4, 8, 16)
POOL_GROUP = 128
POOL_WIDTH = 512
SSM_WIDTH = 512
SSM_GROUP = 16
SSM_N_GROUPS = 32
SSM_STATE = 64
SSM_CHUNK = 16
LN_EPS = 1e-5
D_FF = 5504

LANES = 128
SSM_BUNDLE = LANES // SSM_GROUP
SSM_N_BUNDLES = SSM_N_GROUPS // SSM_BUNDLE
SSM_BW = SSM_CHUNK * LANES
SSM_SW = SSM_BUNDLE * 2 * SSM_STATE
HALF_WIDTH = ATTN_WIDTH + KV_WIDTH
KV2_WIDTH = 2 * KV_WIDTH
VMEM_LIMIT = 56 * 1024 * 1024


def _params(*sem):
    return pltpu.CompilerParams(dimension_semantics=sem, vmem_limit_bytes=VMEM_LIMIT)


def _resident(block, index_map):
    return pl.BlockSpec(block, index_map, pipeline_mode=pl.Buffered(1))


def _cast_kernel(x_ref, o_ref):
    o_ref[...] = x_ref[...].astype(BF16)


def _cast_bf16(w, block_rows):
    rows, cols = w.shape
    return pl.pallas_call(
        _cast_kernel,
        out_shape=jax.ShapeDtypeStruct((rows, cols), BF16),
        grid=(rows // block_rows,),
        in_specs=[pl.BlockSpec((block_rows, cols), lambda i: (i, 0))],
        out_specs=pl.BlockSpec((block_rows, cols), lambda i: (i, 0)),
        compiler_params=_params("parallel"),
        name="cast_bf16",
    )(w)


IN_TM = 512
CAST_ROWS = 256


def _cast_weight_once(w_ref, wbf_ref):
    @pl.when(pl.program_id(0) == 0)
    def _():
        for r in range(0, w_ref.shape[0], CAST_ROWS):
            wbf_ref[r:r + CAST_ROWS, :] = w_ref[r:r + CAST_ROWS, :].astype(BF16)


def _dup_heads(pair):
    lo = lax.broadcasted_iota(jnp.int32, pair.shape, 1) < HEAD_DIM
    swapped = pltpu.roll(pair, HEAD_DIM, 1)
    return jnp.where(lo, pair, swapped), jnp.where(lo, swapped, pair)


def _inproj_qk_kernel(x_ref, w_ref, cos_ref, sa_ref, sb_ref, q_ref, k_ref, wbf_ref):
    _cast_weight_once(w_ref, wbf_ref)
    acc = jnp.dot(x_ref[...].astype(BF16), wbf_ref[...], preferred_element_type=F32)
    cos, sa, sb = cos_ref[...], sa_ref[...], sb_ref[...]
    nq = ATTN_WIDTH // LANES
    for c in range(HALF_WIDTH // LANES):
        a = acc[:, c * LANES:(c + 1) * LANES]
        r = a * cos + pltpu.roll(a, LANES - 32, 1) * sa + pltpu.roll(a, 32, 1) * sb
        if c < nq:
            q_ref[:, c * LANES:(c + 1) * LANES] = (r * (HEAD_DIM ** -0.5)).astype(BF16)
        else:
            ka, kb = _dup_heads(r)
            g = 2 * (c - nq)
            k_ref[:, g * LANES:(g + 1) * LANES] = ka.astype(BF16)
            k_ref[:, (g + 1) * LANES:(g + 2) * LANES] = kb.astype(BF16)


def _inproj_vps_kernel(x_ref, w_ref, v_ref, p_ref, s_ref, wbf_ref):
    _cast_weight_once(w_ref, wbf_ref)
    acc = jnp.dot(x_ref[...].astype(BF16), wbf_ref[...], preferred_element_type=F32)
    for c in range(KV_WIDTH // LANES):
        va, vb = _dup_heads(acc[:, c * LANES:(c + 1) * LANES])
        v_ref[:, 2 * c * LANES:(2 * c + 1) * LANES] = va.astype(BF16)
        v_ref[:, (2 * c + 1) * LANES:(2 * c + 2) * LANES] = vb.astype(BF16)
    p_ref[...] = acc[:, KV_WIDTH:KV_WIDTH + POOL_WIDTH].astype(BF16)
    for b in range(SSM_N_BUNDLES):
        c0 = KV_WIDTH + POOL_WIDTH + b * LANES
        s_ref[b] = acc[:, c0:c0 + LANES].astype(BF16)


def _in_proj(x, w_in2d, layer, cos, sa, sb):
    m = x.shape[0]
    nseq = SEQ // IN_TM
    x_spec = pl.BlockSpec((IN_TM, D_MODEL), lambda i: (i, 0))
    tab = pl.BlockSpec((IN_TM, LANES), lambda i: (i % nseq, 0))
    row = lambda width: pl.BlockSpec((IN_TM, width), lambda i: (i, 0))
    wscratch = [pltpu.VMEM((D_MODEL, HALF_WIDTH), BF16)]
    q, k2 = pl.pallas_call(
        _inproj_qk_kernel,
        out_shape=(jax.ShapeDtypeStruct((m, ATTN_WIDTH), BF16), jax.ShapeDtypeStruct((m, KV2_WIDTH), BF16)),
        grid=(m // IN_TM,),
        in_specs=[x_spec, _resident((D_MODEL, HALF_WIDTH), lambda i: (layer, 0)), tab, tab, tab],
        out_specs=(row(ATTN_WIDTH), row(KV2_WIDTH)),
        scratch_shapes=wscratch,
        compiler_params=_params("arbitrary"),
        name="in_proj_qk",
    )(x, w_in2d, cos, sa, sb)
    v2, pu, su = pl.pallas_call(
        _inproj_vps_kernel,
        out_shape=(jax.ShapeDtypeStruct((m, KV2_WIDTH), BF16), jax.ShapeDtypeStruct((m, POOL_WIDTH), BF16),
                   jax.ShapeDtypeStruct((SSM_N_BUNDLES, m, LANES), BF16)),
        grid=(m // IN_TM,),
        in_specs=[x_spec, _resident((D_MODEL, HALF_WIDTH), lambda i: (layer, 1))],
        out_specs=(row(KV2_WIDTH), row(POOL_WIDTH),
                   pl.BlockSpec((SSM_N_BUNDLES, IN_TM, LANES), lambda i: (0, i, 0))),
        scratch_shapes=wscratch,
        compiler_params=_params("arbitrary"),
        name="in_proj_vps",
    )(x, w_in2d)
    return q, k2, v2, pu, su


def _rope_tables():
    half = HEAD_DIM // 2
    inv = ROPE_THETA ** (-jnp.arange(half, dtype=F32) / half)
    ang = jnp.arange(SEQ, dtype=F32)[:, None] * inv[None, :]
    cos, sin = jnp.cos(ang), jnp.sin(ang)
    zero = jnp.zeros_like(sin)
    reps = LANES // HEAD_DIM
    cos_t = jnp.tile(jnp.concatenate([cos, cos], -1), (1, reps))
    sa_t = jnp.tile(jnp.concatenate([-sin, zero], -1), (1, reps))
    sb_t = jnp.tile(jnp.concatenate([zero, sin], -1), (1, reps))
    return cos_t, sa_t, sb_t


ATTN_TQ = 512
ATTN_QB = ATTN_TQ // ATTN_BLOCK
KEYS = 2 * ATTN_BLOCK


def _attn_kernel(sink_ref, q_ref, kc_ref, vc_ref, kp_ref, vp_ref, o_ref, kbuf, vbuf):
    kbuf[0:ATTN_BLOCK, :] = kp_ref[...]
    kbuf[ATTN_BLOCK:, :] = kc_ref[...]
    vbuf[0:ATTN_BLOCK, :] = vp_ref[...]
    vbuf[ATTN_BLOCK:, :] = vc_ref[...]
    seq_start = pl.program_id(1) == 0

    row = lax.broadcasted_iota(jnp.int32, (ATTN_BLOCK, KEYS), 0)
    col = lax.broadcasted_iota(jnp.int32, (ATTN_BLOCK, KEYS), 1)
    dist = row + ATTN_BLOCK - col
    band = (dist >= 0) & (dist < ATTN_BLOCK)
    band_first = band & (jnp.logical_not(seq_start) | (col >= ATTN_BLOCK))
    lo_kv = lax.broadcasted_iota(jnp.int32, (KEYS, LANES), 1) < HEAD_DIM
    lo_out = lax.broadcasted_iota(jnp.int32, (ATTN_BLOCK, LANES), 1) < HEAD_DIM
    zero_kv = jnp.zeros((KEYS, LANES), BF16)

    def split_heads(x):
        return jnp.concatenate([jnp.where(lo_kv, x, zero_kv), jnp.where(lo_kv, zero_kv, x)], axis=0)

    for qb in range(ATTN_QB):
        valid = band_first if qb == 0 else band
        r0 = qb * ATTN_BLOCK
        for g in range(N_KV_HEADS):
            k2 = split_heads(kbuf[r0:r0 + KEYS, g * LANES:(g + 1) * LANES])
            v2 = split_heads(vbuf[r0:r0 + KEYS, g * LANES:(g + 1) * LANES])
            c0 = 2 * g * LANES
            qq = jnp.concatenate([q_ref[r0:r0 + ATTN_BLOCK, c0:c0 + LANES],
                                  q_ref[r0:r0 + ATTN_BLOCK, c0 + LANES:c0 + 2 * LANES]], axis=0)
            s4 = lax.dot_general(qq, k2, (((1,), (1,)), ((), ())), preferred_element_type=F32)
            p_rows, dens = [], []
            for pr in range(2):
                ps = []
                for hh in range(2):
                    sink = sink_ref[4 * g + 2 * pr + hh]
                    s = jnp.where(valid, s4[pr * ATTN_BLOCK:(pr + 1) * ATTN_BLOCK, hh * KEYS:(hh + 1) * KEYS], -1e30)
                    mx = jnp.maximum(s.max(-1, keepdims=True), sink)
                    p = jnp.exp(s - mx)
                    dens.append(p.sum(-1, keepdims=True) + jnp.exp(sink - mx))
                    ps.append(p.astype(BF16))
                p_rows.append(jnp.concatenate(ps, axis=1))
            o4 = jnp.dot(jnp.concatenate(p_rows, axis=0), v2, preferred_element_type=F32)
            for pr in range(2):
                o = o4[pr * ATTN_BLOCK:(pr + 1) * ATTN_BLOCK] / jnp.where(lo_out, dens[2 * pr], dens[2 * pr + 1])
                o_ref[r0:r0 + ATTN_BLOCK, c0 + pr * LANES:c0 + (pr + 1) * LANES] = o.astype(BF16)


def _attention(q, k2, v2, sinks):
    m = q.shape[0]
    nq = SEQ // ATTN_TQ
    cur = lambda b, i: (b * nq + i, 0)
    prev = lambda b, i: (jnp.maximum((b * nq + i) * ATTN_QB - 1, 0), 0)
    return pl.pallas_call(
        _attn_kernel,
        out_shape=jax.ShapeDtypeStruct((m, ATTN_WIDTH), BF16),
        grid=(m // SEQ, nq),
        in_specs=[
            pl.BlockSpec(memory_space=pltpu.SMEM),
            pl.BlockSpec((ATTN_TQ, ATTN_WIDTH), cur),
            pl.BlockSpec((ATTN_TQ, KV2_WIDTH), cur),
            pl.BlockSpec((ATTN_TQ, KV2_WIDTH), cur),
            pl.BlockSpec((ATTN_BLOCK, KV2_WIDTH), prev),
            pl.BlockSpec((ATTN_BLOCK, KV2_WIDTH), prev),
        ],
        out_specs=pl.BlockSpec((ATTN_TQ, ATTN_WIDTH), cur),
        scratch_shapes=[pltpu.VMEM((ATTN_TQ + ATTN_BLOCK, KV2_WIDTH), BF16),
                        pltpu.VMEM((ATTN_TQ + ATTN_BLOCK, KV2_WIDTH), BF16)],
        compiler_params=_params("parallel", "arbitrary"),
        name="swa_attention",
    )(sinks, q, k2, v2, k2, v2)


def _shift_rows(x, d, rows):
    return jnp.where(rows >= d, pltpu.roll(x, d, 0), 0.0)


def _pool_kernel(u_ref, w_ref, scale_ref, o_ref):
    gi = pl.program_id(1)
    u = u_ref[...].astype(F32)
    rows = lax.broadcasted_iota(jnp.int32, u.shape, 0)
    t1 = lax.broadcasted_iota(jnp.int32, (SEQ, 1), 0).astype(F32) + 1.0
    for idx, w in enumerate(POOL_WINDOWS):
        @pl.when(gi == idx)
        def _(w=w):
            s, d = u, 1
            while d < w:
                s = s + _shift_rows(s, d, rows)
                d *= 2
            mean = s / jnp.minimum(t1, float(w))
            y = jnp.dot((mean - u).astype(BF16), w_ref[0].astype(BF16), preferred_element_type=F32)
            o_ref[...] = (y * scale_ref[...]).astype(BF16)


def _pool(pu, pool_w3d, layer, pool_scale_l):
    m = pu.shape[0]
    return pl.pallas_call(
        _pool_kernel,
        out_shape=jax.ShapeDtypeStruct((m, POOL_WIDTH), BF16),
        grid=(m // SEQ, len(POOL_WINDOWS)),
        in_specs=[
            pl.BlockSpec((SEQ, POOL_GROUP), lambda b, g: (b, g)),
            pl.BlockSpec((1, POOL_GROUP, POOL_GROUP), lambda b, g: (layer * len(POOL_WINDOWS) + g, 0, 0)),
            pl.BlockSpec((1, POOL_GROUP), lambda b, g: (0, g)),
        ],
        out_specs=pl.BlockSpec((SEQ, POOL_GROUP), lambda b, g: (b, g)),
        compiler_params=_params("parallel", "arbitrary"),
        name="multiscale_pool",
    )(pu, pool_w3d, pool_scale_l)


SSM_SCAN_STEPS = 7
SSM_COLS = 2 * LANES


def _ssm_kernel(u_ref, a_ref, eg_ref, cg_ref, x_ref, ar_ref, ai_ref, o_ref, km_ref, em_ref, cm_ref):
    T, GB, H = SSM_CHUNK, SSM_BUNDLE, SSM_GROUP
    sh = H.bit_length() - 1

    @pl.when(pl.program_id(0) == 0)
    def _():
        em_ref[...] = jnp.zeros_like(em_ref)

    row_g = lax.broadcasted_iota(jnp.int32, (LANES, LANES), 0) >> sh
    col_g = lax.broadcasted_iota(jnp.int32, (LANES, LANES), 1) >> sh
    zero_blk = jnp.zeros((LANES, LANES), BF16)
    lag_blk = [jnp.where(row_g == col_g, jnp.concatenate([a_ref[0, t]] * GB, axis=0), zero_blk) for t in range(T)]
    for i in range(T):
        for j in range(min(i | 1, T - 1) + 1):
            km_ref[j * LANES:(j + 1) * LANES, i * LANES:(i + 1) * LANES] = lag_blk[i - j] if j <= i else zero_blk

    for j in range(T):
        for g in range(GB):
            r0 = j * LANES + g * H
            em_ref[r0:r0 + H, g * LANES:(g + 1) * LANES] = eg_ref[0, g, j * H:(j + 1) * H, :]

    lane_g = (lax.broadcasted_iota(jnp.int32, (LANES, SSM_BW), 1) & (LANES - 1)) >> sh
    for g in range(GB):
        spread = jnp.dot(cg_ref[0, g], x_ref[...], preferred_element_type=F32)
        cm_ref[g * LANES:(g + 1) * LANES, :] = jnp.where(lane_g == g, spread, 0.0).astype(BF16)

    u = u_ref[0]
    e = jnp.dot(u, em_ref[...], preferred_element_type=F32)
    nchunk = SEQ // SSM_CHUNK
    rows = lax.broadcasted_iota(jnp.int32, (e.shape[0], LANES), 0) % nchunk
    zprev = []
    for g in range(SSM_BUNDLE):
        eg = e[:, g * LANES:(g + 1) * LANES]
        for k in range(SSM_SCAN_STEPS):
            s = _shift_rows(eg, 1 << k, rows)
            sl = slice(g * LANES, (g + 1) * LANES)
            eg = eg + ar_ref[0, k:k + 1, sl] * s + ai_ref[0, k:k + 1, sl] * pltpu.roll(s, SSM_STATE, 1)
        zprev.append(_shift_rows(eg, 1, rows).astype(BF16))
    zp = jnp.concatenate(zprev, axis=1)

    for mblk in range(SSM_BW // SSM_COLS):
        c0, kdim = mblk * SSM_COLS, (mblk + 1) * SSM_COLS
        y = jnp.dot(u[:, :kdim], km_ref[0:kdim, c0:c0 + SSM_COLS], preferred_element_type=F32)
        y = y + jnp.dot(zp, cm_ref[:, c0:c0 + SSM_COLS], preferred_element_type=F32)
        o_ref[0, :, c0:c0 + SSM_COLS] = jax.nn.gelu(y).astype(BF16)


def _ssm(u8, layer, a_lag, eg, cg, spread, ar, ai):
    nb, rows, _ = u8.shape
    T, GB, H, Q = SSM_CHUNK, SSM_BUNDLE, SSM_GROUP, 2 * SSM_STATE
    par = lambda *dims: pl.BlockSpec((1,) + dims, lambda i: (layer * nb + i,) + (0,) * len(dims))
    act = pl.BlockSpec((1, rows, SSM_BW), lambda i: (i, 0, 0))
    return pl.pallas_call(
        _ssm_kernel,
        out_shape=jax.ShapeDtypeStruct((nb, rows, SSM_BW), BF16),
        grid=(nb,),
        in_specs=[act, par(T, H, LANES), par(GB, T * H, Q), par(GB, Q, T * H),
                  pl.BlockSpec((T * H, SSM_BW), lambda i: (0, 0)), par(8, SSM_SW), par(8, SSM_SW)],
        out_specs=act,
        scratch_shapes=[pltpu.VMEM((SSM_BW, SSM_BW), BF16), pltpu.VMEM((SSM_BW, SSM_SW), BF16),
                        pltpu.VMEM((SSM_SW, SSM_BW), BF16)],
        compiler_params=_params("arbitrary"),
        name="s5_chunked_ssm",
    )(u8, a_lag, eg, cg, spread, ar, ai)


def _cmul(xr, xi, yr, yi):
    return xr * yr - xi * yi, xr * yi + xi * yr


def _ssm_operators(lam_re, lam_im, log_dt, b_re, b_im, c_re, c_im, d):
    hi = lax.Precision.HIGHEST
    lr, li = lam_re.astype(F32), lam_im.astype(F32)
    dt = jnp.exp(log_dt.astype(F32))[..., None]
    mag = jnp.exp(lr * dt)
    ar, ai = mag * jnp.cos(li * dt), mag * jnp.sin(li * dt)
    nr, ni = ar - 1.0, ai
    den = lr * lr + li * li
    zr = (nr * lr + ni * li) / den
    zi = (ni * lr - nr * li) / den
    br, bi = b_re.astype(F32), b_im.astype(F32)
    bbr = zr[..., None] * br - zi[..., None] * bi
    bbi = zr[..., None] * bi + zi[..., None] * br
    cr, ci = c_re.astype(F32), c_im.astype(F32)

    pr, pi = [jnp.ones_like(ar)], [jnp.zeros_like(ar)]
    for _ in range(SSM_CHUNK):
        nr_, ni_ = _cmul(pr[-1], pi[-1], ar, ai)
        pr.append(nr_)
        pi.append(ni_)
    prr, pir = jnp.stack(pr[SSM_CHUNK - 1::-1]), jnp.stack(pi[SSM_CHUNK - 1::-1])
    pr, pi = jnp.stack(pr), jnp.stack(pi)

    T, G, H, P = SSM_CHUNK, SSM_N_GROUPS, SSM_GROUP, SSM_STATE
    NB, GB = SSM_N_BUNDLES, SSM_BUNDLE
    L = lr.shape[0]
    car = cr[None] * pr[:, :, :, None, :] - ci[None] * pi[:, :, :, None, :]
    cai = cr[None] * pi[:, :, :, None, :] + ci[None] * pr[:, :, :, None, :]
    kall = (jnp.einsum('tlghp,lgpk->tlghk', car[:T], bbr, precision=hi)
            - jnp.einsum('tlghp,lgpk->tlghk', cai[:T], bbi, precision=hi))
    kall = kall.at[0].add(d.astype(F32)[..., None] * jnp.eye(H, dtype=F32))
    a_lag = kall.reshape(T, L, NB, GB, H, H).transpose(1, 2, 0, 5, 3, 4).reshape(L * NB, T, H, GB * H)

    er = prr[..., None] * bbr[None] - pir[..., None] * bbi[None]
    ei = prr[..., None] * bbi[None] + pir[..., None] * bbr[None]
    eg = jnp.concatenate([er.transpose(1, 2, 0, 4, 3), ei.transpose(1, 2, 0, 4, 3)], -1)
    eg = eg.reshape(L * NB, GB, T * H, 2 * P)

    cg = jnp.concatenate([car[1:].transpose(1, 2, 4, 0, 3), -cai[1:].transpose(1, 2, 4, 0, 3)], 2)
    cg = cg.reshape(L * NB, GB, 2 * P, T * H)

    sr, si = [pr[T]], [pi[T]]
    for _ in range(SSM_SCAN_STEPS - 1):
        nr_, ni_ = _cmul(sr[-1], si[-1], sr[-1], si[-1])
        sr.append(nr_)
        si.append(ni_)
    sr.append(jnp.zeros_like(ar))
    si.append(jnp.zeros_like(ar))
    sr, si = jnp.stack(sr, 2), jnp.stack(si, 2)
    lay = lambda t: t.reshape(L, NB, GB, 8, 2 * P).transpose(0, 1, 3, 2, 4).reshape(L * NB, 8, GB * 2 * P)
    ar_t = lay(jnp.concatenate([sr, sr], -1))
    ai_t = lay(jnp.concatenate([-si, si], -1))

    spread = np.zeros((T, H, T, GB, H), np.float32)
    for i in range(T):
        for h in range(H):
            spread[i, h, i, :, h] = 1.0
    spread = jnp.asarray(spread.reshape(T * H, T * GB * H), BF16)
    return a_lag.astype(BF16), eg.astype(BF16), cg.astype(BF16), spread, ar_t, ai_t


def _deepnorm_ln(x, branch, g, b, alpha):
    y = alpha * x + branch
    mu = y.mean(-1, keepdims=True)
    yc = y - mu
    var = jnp.square(yc).mean(-1, keepdims=True)
    return yc * lax.rsqrt(var + LN_EPS) * g + b


OUT_TM = 512


def _outproj_kernel(alpha, ya_ref, yp_ref, ys_ref, glu_ref, w_ref, x_ref, g_ref, b_ref, o_ref):
    ys = jnp.concatenate([ys_ref[b] for b in range(SSM_N_BUNDLES)], axis=1)
    ab = jnp.dot(ys, glu_ref[...].astype(BF16), preferred_element_type=F32)
    y_ssm = (ab[:, :SSM_WIDTH] * jax.nn.sigmoid(ab[:, SSM_WIDTH:])).astype(BF16)
    o_p, o_s = ATTN_WIDTH, ATTN_WIDTH + POOL_WIDTH
    acc = jnp.dot(ya_ref[...], w_ref[0:o_p, :], preferred_element_type=F32)
    acc += jnp.dot(yp_ref[...], w_ref[o_p:o_s, :], preferred_element_type=F32)
    acc += jnp.dot(y_ssm, w_ref[o_s:, :], preferred_element_type=F32)
    o_ref[...] = _deepnorm_ln(x_ref[...], acc, g_ref[...], b_ref[...], alpha)


def _out_proj(ya, yp, ys8, glu_w2d, w_out_bf, layer, x, g, b, alpha):
    m = x.shape[0]
    row = lambda width: pl.BlockSpec((OUT_TM, width), lambda i: (i, 0))
    vec = pl.BlockSpec((1, D_MODEL), lambda i: (0, 0))
    return pl.pallas_call(
        functools.partial(_outproj_kernel, alpha),
        out_shape=jax.ShapeDtypeStruct((m, D_MODEL), F32),
        grid=(m // OUT_TM,),
        in_specs=[
            row(ATTN_WIDTH), row(POOL_WIDTH),
            pl.BlockSpec((SSM_N_BUNDLES, OUT_TM, LANES), lambda i: (0, i, 0)),
            pl.BlockSpec((SSM_WIDTH, 2 * SSM_WIDTH), lambda i: (layer, 0)),
            _resident((D_MODEL, D_MODEL), lambda i: (layer, 0)),
            row(D_MODEL), vec, vec,
        ],
        out_specs=row(D_MODEL),
        compiler_params=_params("parallel"),
        name="out_proj_ln",
    )(ya, yp, ys8, glu_w2d, w_out_bf, x, g, b)


UP_TM = 1024
UP_TN = 512
UP_NJ = -(-D_FF // UP_TN)
FF_PAD = UP_NJ * UP_TN
UP_SHIFT = FF_PAD - D_FF
HALO = 8
CONV_ROWS = 8


def _ffn_up_kernel(x_ref, xp_ref, wv_ref, wg_ref, cv_ref, cg_ref, o_ref, xe_ref, hv_ref, hg_ref):
    @pl.when(pl.program_id(1) == 0)
    def _():
        xe_ref[0:HALO, :] = xp_ref[...].astype(BF16)
        xe_ref[HALO:, :] = x_ref[...].astype(BF16)

    xe = xe_ref[...]
    seq_start = (pl.program_id(0) % (SEQ // UP_TM)) == 0
    rows = lax.broadcasted_iota(jnp.int32, (HALO, 1), 0)

    def conv(w_ref, c_ref, h_ref):
        h_ref[...] = jnp.dot(xe, w_ref[...], preferred_element_type=F32)

        def tap(shift):
            h = h_ref[HALO - shift:HALO - shift + UP_TM, :]
            head = jnp.where(seq_start & (rows < shift), 0.0, h[:HALO])
            return jnp.concatenate([head, h[HALO:]], axis=0)

        return c_ref[3:4, :] + tap(2) * c_ref[0:1, :] + tap(1) * c_ref[1:2, :] + h_ref[HALO:, :] * c_ref[2:3, :]

    gate = jax.nn.silu(conv(wg_ref, cg_ref, hg_ref))
    act = (gate * conv(wv_ref, cv_ref, hv_ref)).astype(BF16)
    last = pl.num_programs(1) - 1

    @pl.when(pl.program_id(1) != last)
    def _():
        o_ref[...] = act

    @pl.when(pl.program_id(1) == last)
    def _():
        o_ref[...] = jnp.concatenate([act[:, UP_SHIFT:], jnp.zeros((UP_TM, UP_SHIFT), BF16)], axis=1)


def _ffn_up(x, w_up_bf, layer, conv8):
    m = x.shape[0]
    col = lambda c0, j: pl.multiple_of(c0 + jnp.minimum(j * UP_TN, D_FF - UP_TN), LANES)
    window = lambda rows, r0, c0: pl.BlockSpec((pl.Element(rows), pl.Element(UP_TN)),
                                               lambda i, j: (r0, col(c0, j)))
    return pl.pallas_call(
        _ffn_up_kernel,
        out_shape=jax.ShapeDtypeStruct((m, FF_PAD), BF16),
        grid=(m // UP_TM, UP_NJ),
        in_specs=[
            pl.BlockSpec((UP_TM, D_MODEL), lambda i, j: (i, 0)),
            pl.BlockSpec((HALO, D_MODEL), lambda i, j: (jnp.maximum(i * (UP_TM // HALO) - 1, 0), 0)),
            window(D_MODEL, layer * D_MODEL, 0), window(D_MODEL, layer * D_MODEL, D_FF),
            window(CONV_ROWS, layer * CONV_ROWS, 0), window(CONV_ROWS, layer * CONV_ROWS, D_FF),
        ],
        out_specs=pl.BlockSpec((UP_TM, UP_TN), lambda i, j: (i, j)),
        scratch_shapes=[pltpu.VMEM((HALO + UP_TM, D_MODEL), BF16),
                        pltpu.VMEM((HALO + UP_TM, UP_TN), F32), pltpu.VMEM((HALO + UP_TM, UP_TN), F32)],
        compiler_params=_params("parallel", "arbitrary"),
        name="ffn_up_conv_gate",
    )(x, x, w_up_bf, w_up_bf, conv8, conv8)


DOWN_TM = 256


def _ffn_down_kernel(alpha, a_ref, w_ref, x_ref, g_ref, b_ref, o_ref):
    f = jnp.dot(a_ref[...], w_ref[...], preferred_element_type=F32)
    o_ref[...] = _deepnorm_ln(x_ref[...], f, g_ref[...], b_ref[...], alpha)


def _ffn_down(act, w_down_bf, layer, x, g, b, alpha):
    m = x.shape[0]
    vec = pl.BlockSpec((1, D_MODEL), lambda i: (0, 0))
    return pl.pallas_call(
        functools.partial(_ffn_down_kernel, alpha),
        out_shape=jax.ShapeDtypeStruct((m, D_MODEL), F32),
        grid=(m // DOWN_TM,),
        in_specs=[
            pl.BlockSpec((DOWN_TM, D_FF), lambda i: (i, 0)),
            _resident((D_FF, D_MODEL), lambda i: (layer, 0)),
            pl.BlockSpec((DOWN_TM, D_MODEL), lambda i: (i, 0)),
            vec, vec,
        ],
        out_specs=pl.BlockSpec((DOWN_TM, D_MODEL), lambda i: (i, 0)),
        compiler_params=_params("parallel"),
        name="ffn_down_ln",
    )(act, w_down_bf, x, g, b)


def kernel(x, w_in, attn_sinks, pool_w, pool_scale, ssm_lam_re, ssm_lam_im, ssm_log_dt, ssm_b_re, ssm_b_im,
           ssm_c_re, ssm_c_im, ssm_d, ssm_glu_w, w_out, ln1_g, ln1_b, ffn_w_up, ffn_conv_w, ffn_conv_b,
           ffn_w_down, ln2_g, ln2_b):
    bsz, s_len, _ = x.shape
    assert s_len == SEQ and x.shape[2] == D_MODEL
    depth = w_in.shape[0]
    alpha = (2 * depth) ** 0.25
    m = bsz * s_len
    nchunks = m // SSM_CHUNK
    cos_t, sa_t, sb_t = _rope_tables()
    xf = x.reshape(m, D_MODEL).astype(F32)

    w_out_bf = _cast_bf16(w_out.astype(F32).reshape(depth * D_MODEL, D_MODEL), 1024)
    w_up_bf = _cast_bf16(ffn_w_up.astype(F32).reshape(depth * D_MODEL, 2 * D_FF), 256)
    w_down_bf = _cast_bf16(ffn_w_down.astype(F32).reshape(depth * D_FF, D_MODEL), D_FF // 8)
    conv8 = jnp.concatenate([ffn_conv_w.astype(F32), ffn_conv_b.astype(F32)[:, None, :],
                             jnp.zeros((depth, CONV_ROWS - 4, 2 * D_FF), F32)], axis=1)
    conv8 = conv8.reshape(depth * CONV_ROWS, 2 * D_FF)
    w_in2d = w_in.astype(F32).reshape(depth * D_MODEL, 2 * HALF_WIDTH)
    glu_w2d = ssm_glu_w.astype(F32).reshape(depth * SSM_WIDTH, 2 * SSM_WIDTH)
    pool_w3d = pool_w.astype(F32).reshape(depth * len(POOL_WINDOWS), POOL_GROUP, POOL_GROUP)
    vec = lambda a: a.astype(F32).reshape(1, -1)
    ssm_ops = _ssm_operators(ssm_lam_re, ssm_lam_im, ssm_log_dt, ssm_b_re, ssm_b_im, ssm_c_re, ssm_c_im, ssm_d)

    for l in range(depth):
        q, k2, v2, pu, su = _in_proj(xf, w_in2d, l, cos_t, sa_t, sb_t)
        y_attn = _attention(q, k2, v2, attn_sinks[l].astype(F32))
        y_pool = _pool(pu, pool_w3d, l, vec(pool_scale[l]))

        y8 = _ssm(su.reshape(SSM_N_BUNDLES, nchunks, SSM_BW), l, *ssm_ops)
        y8 = y8.reshape(SSM_N_BUNDLES, m, LANES)

        x1 = _out_proj(y_attn, y_pool, y8, glu_w2d, w_out_bf, l, xf,
                       vec(ln1_g[l]), vec(ln1_b[l]), alpha)
        act = _ffn_up(x1, w_up_bf, l, conv8)
        xf = _ffn_down(act, w_down_bf, l, x1, vec(ln2_g[l]), vec(ln2_b[l]), alpha)

    return xf.reshape(bsz, s_len, D_MODEL).astype(x.dtype)
```

```python
import functools

import jax
import numpy as np
import jax.numpy as jnp
from jax import lax
from jax.experimental import pallas as pl
from jax.experimental.pallas import tpu as pltpu

F32 = jnp.float32
BF16 = jnp.bfloat16

D_MODEL = 2048
SEQ = 2048
HEAD_DIM = 64
N_Q_HEADS = 16
N_KV_HEADS = 4
ATTN_WIDTH = N_Q_HEADS * HEAD_DIM
KV_WIDTH = N_KV_HEADS * HEAD_DIM
ATTN_BLOCK = 128
ROPE_THETA = 10000.0
POOL_WINDOWS = (2, 4, 8, 16)
POOL_GROUP = 128
POOL_WIDTH = 512
SSM_WIDTH = 512
SSM_GROUP = 16
SSM_N_GROUPS = 32
SSM_STATE = 64
SSM_CHUNK = 16
LN_EPS = 1e-5
D_FF = 5504

LANES = 128
SSM_BUNDLE = LANES // SSM_GROUP
SSM_N_BUNDLES = SSM_N_GROUPS // SSM_BUNDLE
SSM_BW = SSM_CHUNK * LANES
SSM_SW = SSM_BUNDLE * 2 * SSM_STATE
HALF_WIDTH = ATTN_WIDTH + KV_WIDTH
KV2_WIDTH = 2 * KV_WIDTH
VMEM_LIMIT = 56 * 1024 * 1024


def _params(*sem):
    return pltpu.CompilerParams(dimension_semantics=sem, vmem_limit_bytes=VMEM_LIMIT)


def _resident(block, index_map):
    return pl.BlockSpec(block, index_map, pipeline_mode=pl.Buffered(1))


def _cast_kernel(x_ref, o_ref):
    o_ref[...] = x_ref[...].astype(BF16)


def _cast_bf16(w, block_rows):
    rows, cols = w.shape
    return pl.pallas_call(
        _cast_kernel,
        out_shape=jax.ShapeDtypeStruct((rows, cols), BF16),
        grid=(rows // block_rows,),
        in_specs=[pl.BlockSpec((block_rows, cols), lambda i: (i, 0))],
        out_specs=pl.BlockSpec((block_rows, cols), lambda i: (i, 0)),
        compiler_params=_params("parallel"),
        name="cast_bf16",
    )(w)


IN_TM = 512
CAST_ROWS = 256


def _cast_weight_once(w_ref, wbf_ref):
    @pl.when(pl.program_id(0) == 0)
    def _():
        for r in range(0, w_ref.shape[0], CAST_ROWS):
            wbf_ref[r:r + CAST_ROWS, :] = w_ref[r:r + CAST_ROWS, :].astype(BF16)


def _dup_heads(pair):
    lo = lax.broadcasted_iota(jnp.int32, pair.shape, 1) < HEAD_DIM
    swapped = pltpu.roll(pair, HEAD_DIM, 1)
    return jnp.where(lo, pair, swapped), jnp.where(lo, swapped, pair)


def _inproj_qk_kernel(x_ref, w_ref, cos_ref, sa_ref, sb_ref, q_ref, k_ref, wbf_ref):
    _cast_weight_once(w_ref, wbf_ref)
    acc = jnp.dot(x_ref[...].astype(BF16), wbf_ref[...], preferred_element_type=F32)
    cos, sa, sb = cos_ref[...], sa_ref[...], sb_ref[...]
    nq = ATTN_WIDTH // LANES
    for c in range(HALF_WIDTH // LANES):
        a = acc[:, c * LANES:(c + 1) * LANES]
        r = a * cos + pltpu.roll(a, LANES - 32, 1) * sa + pltpu.roll(a, 32, 1) * sb
        if c < nq:
            q_ref[:, c * LANES:(c + 1) * LANES] = (r * (HEAD_DIM ** -0.5)).astype(BF16)
        else:
            ka, kb = _dup_heads(r)
            g = 2 * (c - nq)
            k_ref[:, g * LANES:(g + 1) * LANES] = ka.astype(BF16)
            k_ref[:, (g + 1) * LANES:(g + 2) * LANES] = kb.astype(BF16)


def _inproj_vps_kernel(x_ref, w_ref, v_ref, p_ref, s_ref, wbf_ref, tok_ref):
    _cast_weight_once(w_ref, wbf_ref)
    acc = jnp.dot(x_ref[...].astype(BF16), wbf_ref[...], preferred_element_type=F32)
    for c in range(KV_WIDTH // LANES):
        va, vb = _dup_heads(acc[:, c * LANES:(c + 1) * LANES])
        v_ref[:, 2 * c * LANES:(2 * c + 1) * LANES] = va.astype(BF16)
        v_ref[:, (2 * c + 1) * LANES:(2 * c + 2) * LANES] = vb.astype(BF16)
    p_ref[...] = acc[:, KV_WIDTH:KV_WIDTH + POOL_WIDTH].astype(BF16)
    for b in range(SSM_N_BUNDLES):
        c0 = KV_WIDTH + POOL_WIDTH + b * LANES
        tok_ref[b] = acc[:, c0:c0 + LANES]
        for j in range(SSM_CHUNK):
            rows = tok_ref[b, pl.ds(j, IN_TM // SSM_CHUNK, stride=SSM_CHUNK), :]
            s_ref[b, :, j * LANES:(j + 1) * LANES] = rows.astype(BF16)


def _in_proj(x, w_in2d, layer, cos, sa, sb):
    m = x.shape[0]
    nseq = SEQ // IN_TM
    x_spec = pl.BlockSpec((IN_TM, D_MODEL), lambda i: (i, 0))
    tab = pl.BlockSpec((IN_TM, LANES), lambda i: (i % nseq, 0))
    row = lambda width: pl.BlockSpec((IN_TM, width), lambda i: (i, 0))
    wscratch = [pltpu.VMEM((D_MODEL, HALF_WIDTH), BF16)]
    q, k2 = pl.pallas_call(
        _inproj_qk_kernel,
        out_shape=(jax.ShapeDtypeStruct((m, ATTN_WIDTH), BF16), jax.ShapeDtypeStruct((m, KV2_WIDTH), BF16)),
        grid=(m // IN_TM,),
        in_specs=[x_spec, _resident((D_MODEL, HALF_WIDTH), lambda i: (layer, 0)), tab, tab, tab],
        out_specs=(row(ATTN_WIDTH), row(KV2_WIDTH)),
        scratch_shapes=wscratch,
        compiler_params=_params("arbitrary"),
        name="in_proj_qk",
    )(x, w_in2d, cos, sa, sb)
    v2, pu, su = pl.pallas_call(
        _inproj_vps_kernel,
        out_shape=(jax.ShapeDtypeStruct((m, KV2_WIDTH), BF16), jax.ShapeDtypeStruct((m, POOL_WIDTH), BF16),
                   jax.ShapeDtypeStruct((SSM_N_BUNDLES, m // SSM_CHUNK, SSM_BW), BF16)),
        grid=(m // IN_TM,),
        in_specs=[x_spec, _resident((D_MODEL, HALF_WIDTH), lambda i: (layer, 1))],
        out_specs=(row(KV2_WIDTH), row(POOL_WIDTH),
                   pl.BlockSpec((SSM_N_BUNDLES, IN_TM // SSM_CHUNK, SSM_BW), lambda i: (0, i, 0))),
        scratch_shapes=wscratch + [pltpu.VMEM((SSM_N_BUNDLES, IN_TM, LANES), F32)],
        compiler_params=_params("arbitrary"),
        name="in_proj_vps",
    )(x, w_in2d)
    return q, k2, v2, pu, su


def _rope_tables():
    half = HEAD_DIM // 2
    inv = ROPE_THETA ** (-jnp.arange(half, dtype=F32) / half)
    ang = jnp.arange(SEQ, dtype=F32)[:, None] * inv[None, :]
    cos, sin = jnp.cos(ang), jnp.sin(ang)
    zero = jnp.zeros_like(sin)
    reps = LANES // HEAD_DIM
    cos_t = jnp.tile(jnp.concatenate([cos, cos], -1), (1, reps))
    sa_t = jnp.tile(jnp.concatenate([-sin, zero], -1), (1, reps))
    sb_t = jnp.tile(jnp.concatenate([zero, sin], -1), (1, reps))
    return cos_t, sa_t, sb_t


ATTN_TQ = 512
ATTN_QB = ATTN_TQ // ATTN_BLOCK
KEYS = 2 * ATTN_BLOCK


def _attn_kernel(sink_ref, q_ref, kc_ref, vc_ref, kp_ref, vp_ref, o_ref, kbuf, vbuf):
    kbuf[0:ATTN_BLOCK, :] = kp_ref[...]
    kbuf[ATTN_BLOCK:, :] = kc_ref[...]
    vbuf[0:ATTN_BLOCK, :] = vp_ref[...]
    vbuf[ATTN_BLOCK:, :] = vc_ref[...]
    seq_start = pl.program_id(1) == 0

    row = lax.broadcasted_iota(jnp.int32, (ATTN_BLOCK, KEYS), 0)
    col = lax.broadcasted_iota(jnp.int32, (ATTN_BLOCK, KEYS), 1)
    dist = row + ATTN_BLOCK - col
    band = (dist >= 0) & (dist < ATTN_BLOCK)
    band_first = band & (jnp.logical_not(seq_start) | (col >= ATTN_BLOCK))
    lo_kv = lax.broadcasted_iota(jnp.int32, (KEYS, LANES), 1) < HEAD_DIM
    lo_out = lax.broadcasted_iota(jnp.int32, (ATTN_BLOCK, LANES), 1) < HEAD_DIM
    zero_kv = jnp.zeros((KEYS, LANES), BF16)

    def split_heads(x):
        return jnp.concatenate([jnp.where(lo_kv, x, zero_kv), jnp.where(lo_kv, zero_kv, x)], axis=0)

    for qb in range(ATTN_QB):
        valid = band_first if qb == 0 else band
        r0 = qb * ATTN_BLOCK
        for g in range(N_KV_HEADS):
            k2 = split_heads(kbuf[r0:r0 + KEYS, g * LANES:(g + 1) * LANES])
            v2 = split_heads(vbuf[r0:r0 + KEYS, g * LANES:(g + 1) * LANES])
            c0 = 2 * g * LANES
            qq = jnp.concatenate([q_ref[r0:r0 + ATTN_BLOCK, c0:c0 + LANES],
                                  q_ref[r0:r0 + ATTN_BLOCK, c0 + LANES:c0 + 2 * LANES]], axis=0)
            s4 = lax.dot_general(qq, k2, (((1,), (1,)), ((), ())), preferred_element_type=F32)
            p_rows, dens = [], []
            for pr in range(2):
                ps = []
                for hh in range(2):
                    sink = sink_ref[4 * g + 2 * pr + hh]
                    s = jnp.where(valid, s4[pr * ATTN_BLOCK:(pr + 1) * ATTN_BLOCK, hh * KEYS:(hh + 1) * KEYS], -1e30)
                    mx = jnp.maximum(s.max(-1, keepdims=True), sink)
                    p = jnp.exp(s - mx)
                    dens.append(p.sum(-1, keepdims=True) + jnp.exp(sink - mx))
                    ps.append(p.astype(BF16))
                p_rows.append(jnp.concatenate(ps, axis=1))
            o4 = jnp.dot(jnp.concatenate(p_rows, axis=0), v2, preferred_element_type=F32)
            for pr in range(2):
                o = o4[pr * ATTN_BLOCK:(pr + 1) * ATTN_BLOCK] / jnp.where(lo_out, dens[2 * pr], dens[2 * pr + 1])
                o_ref[r0:r0 + ATTN_BLOCK, c0 + pr * LANES:c0 + (pr + 1) * LANES] = o.astype(BF16)


def _attention(q, k2, v2, sinks):
    m = q.shape[0]
    nq = SEQ // ATTN_TQ
    cur = lambda b, i: (b * nq + i, 0)
    prev = lambda b, i: (jnp.maximum((b * nq + i) * ATTN_QB - 1, 0), 0)
    return pl.pallas_call(
        _attn_kernel,
        out_shape=jax.ShapeDtypeStruct((m, ATTN_WIDTH), BF16),
        grid=(m // SEQ, nq),
        in_specs=[
            pl.BlockSpec(memory_space=pltpu.SMEM),
            pl.BlockSpec((ATTN_TQ, ATTN_WIDTH), cur),
            pl.BlockSpec((ATTN_TQ, KV2_WIDTH), cur),
            pl.BlockSpec((ATTN_TQ, KV2_WIDTH), cur),
            pl.BlockSpec((ATTN_BLOCK, KV2_WIDTH), prev),
            pl.BlockSpec((ATTN_BLOCK, KV2_WIDTH), prev),
        ],
        out_specs=pl.BlockSpec((ATTN_TQ, ATTN_WIDTH), cur),
        scratch_shapes=[pltpu.VMEM((ATTN_TQ + ATTN_BLOCK, KV2_WIDTH), BF16),
                        pltpu.VMEM((ATTN_TQ + ATTN_BLOCK, KV2_WIDTH), BF16)],
        compiler_params=_params("parallel", "arbitrary"),
        name="swa_attention",
    )(sinks, q, k2, v2, k2, v2)


def _shift_rows(x, d, rows):
    return jnp.where(rows >= d, pltpu.roll(x, d, 0), 0.0)


def _pool_kernel(u_ref, w_ref, scale_ref, o_ref):
    gi = pl.program_id(1)
    u = u_ref[...].astype(F32)
    rows = lax.broadcasted_iota(jnp.int32, u.shape, 0)
    t1 = lax.broadcasted_iota(jnp.int32, (SEQ, 1), 0).astype(F32) + 1.0
    for idx, w in enumerate(POOL_WINDOWS):
        @pl.when(gi == idx)
        def _(w=w):
            s, d = u, 1
            while d < w:
                s = s + _shift_rows(s, d, rows)
                d *= 2
            mean = s / jnp.minimum(t1, float(w))
            y = jnp.dot((mean - u).astype(BF16), w_ref[0].astype(BF16), preferred_element_type=F32)
            o_ref[...] = (y * scale_ref[...]).astype(BF16)


def _pool(pu, pool_w3d, layer, pool_scale_l):
    m = pu.shape[0]
    return pl.pallas_call(
        _pool_kernel,
        out_shape=jax.ShapeDtypeStruct((m, POOL_WIDTH), BF16),
        grid=(m // SEQ, len(POOL_WINDOWS)),
        in_specs=[
            pl.BlockSpec((SEQ, POOL_GROUP), lambda b, g: (b, g)),
            pl.BlockSpec((1, POOL_GROUP, POOL_GROUP), lambda b, g: (layer * len(POOL_WINDOWS) + g, 0, 0)),
            pl.BlockSpec((1, POOL_GROUP), lambda b, g: (0, g)),
        ],
        out_specs=pl.BlockSpec((SEQ, POOL_GROUP), lambda b, g: (b, g)),
        compiler_params=_params("parallel", "arbitrary"),
        name="multiscale_pool",
    )(pu, pool_w3d, pool_scale_l)


SSM_SCAN_STEPS = 7
SSM_COLS = 2 * LANES


def _ssm_kernel(u_ref, a_ref, eg_ref, cg_ref, x_ref, ar_ref, ai_ref, o_ref, km_ref, em_ref, cm_ref):
    T, GB, H = SSM_CHUNK, SSM_BUNDLE, SSM_GROUP
    sh = H.bit_length() - 1

    @pl.when(pl.program_id(0) == 0)
    def _():
        em_ref[...] = jnp.zeros_like(em_ref)

    row_g = lax.broadcasted_iota(jnp.int32, (LANES, LANES), 0) >> sh
    col_g = lax.broadcasted_iota(jnp.int32, (LANES, LANES), 1) >> sh
    zero_blk = jnp.zeros((LANES, LANES), BF16)
    lag_blk = [jnp.where(row_g == col_g, jnp.concatenate([a_ref[0, t]] * GB, axis=0), zero_blk) for t in range(T)]
    for i in range(T):
        for j in range(min(i | 1, T - 1) + 1):
            km_ref[j * LANES:(j + 1) * LANES, i * LANES:(i + 1) * LANES] = lag_blk[i - j] if j <= i else zero_blk

    for j in range(T):
        for g in range(GB):
            r0 = j * LANES + g * H
            em_ref[r0:r0 + H, g * LANES:(g + 1) * LANES] = eg_ref[0, g, j * H:(j + 1) * H, :]

    lane_g = (lax.broadcasted_iota(jnp.int32, (LANES, SSM_BW), 1) & (LANES - 1)) >> sh
    for g in range(GB):
        spread = jnp.dot(cg_ref[0, g], x_ref[...], preferred_element_type=F32)
        cm_ref[g * LANES:(g + 1) * LANES, :] = jnp.where(lane_g == g, spread, 0.0).astype(BF16)

    u = u_ref[0]
    e = jnp.dot(u, em_ref[...], preferred_element_type=F32)
    nchunk = SEQ // SSM_CHUNK
    rows = lax.broadcasted_iota(jnp.int32, (e.shape[0], LANES), 0) % nchunk
    zprev = []
    for g in range(SSM_BUNDLE):
        eg = e[:, g * LANES:(g + 1) * LANES]
        for k in range(SSM_SCAN_STEPS):
            s = _shift_rows(eg, 1 << k, rows)
            sl = slice(g * LANES, (g + 1) * LANES)
            eg = eg + ar_ref[0, k:k + 1, sl] * s + ai_ref[0, k:k + 1, sl] * pltpu.roll(s, SSM_STATE, 1)
        zprev.append(_shift_rows(eg, 1, rows).astype(BF16))
    zp = jnp.concatenate(zprev, axis=1)

    for mblk in range(SSM_BW // SSM_COLS):
        c0, kdim = mblk * SSM_COLS, (mblk + 1) * SSM_COLS
        y = jnp.dot(u[:, :kdim], km_ref[0:kdim, c0:c0 + SSM_COLS], preferred_element_type=F32)
        y = y + jnp.dot(zp, cm_ref[:, c0:c0 + SSM_COLS], preferred_element_type=F32)
        o_ref[0, :, c0:c0 + SSM_COLS] = jax.nn.gelu(y).astype(BF16)


def _ssm(u8, layer, a_lag, eg, cg, spread, ar, ai):
    nb, rows, _ = u8.shape
    T, GB, H, Q = SSM_CHUNK, SSM_BUNDLE, SSM_GROUP, 2 * SSM_STATE
    par = lambda *dims: pl.BlockSpec((1,) + dims, lambda i: (layer * nb + i,) + (0,) * len(dims))
    act = pl.BlockSpec((1, rows, SSM_BW), lambda i: (i, 0, 0))
    return pl.pallas_call(
        _ssm_kernel,
        out_shape=jax.ShapeDtypeStruct((nb, rows, SSM_BW), BF16),
        grid=(nb,),
        in_specs=[act, par(T, H, LANES), par(GB, T * H, Q), par(GB, Q, T * H),
                  pl.BlockSpec((T * H, SSM_BW), lambda i: (0, 0)), par(8, SSM_SW), par(8, SSM_SW)],
        out_specs=act,
        scratch_shapes=[pltpu.VMEM((SSM_BW, SSM_BW), BF16), pltpu.VMEM((SSM_BW, SSM_SW), BF16),
                        pltpu.VMEM((SSM_SW, SSM_BW), BF16)],
        compiler_params=_params("arbitrary"),
        name="s5_chunked_ssm",
    )(u8, a_lag, eg, cg, spread, ar, ai)


def _cmul(xr, xi, yr, yi):
    return xr * yr - xi * yi, xr * yi + xi * yr


def _lag_kernel(ca_ref, bb_ref, o_ref):
    for g in range(ca_ref.shape[0]):
        o_ref[g] = lax.dot_general(bb_ref[g], ca_ref[g], (((1,), (1,)), ((), ())),
                                   precision=lax.Precision.HIGHEST, preferred_element_type=F32)


def _lag_kernels(ca, bb):
    n, rows, q = ca.shape
    h = bb.shape[1]
    spec = lambda a, b: pl.BlockSpec((SSM_BUNDLE, a, b), lambda i: (i, 0, 0))
    return pl.pallas_call(
        _lag_kernel,
        out_shape=jax.ShapeDtypeStruct((n, h, rows), F32),
        grid=(n // SSM_BUNDLE,),
        in_specs=[spec(rows, q), spec(h, q)],
        out_specs=spec(h, rows),
        compiler_params=_params("parallel"),
        name="ssm_lag_kernels",
    )(ca, bb)


def _ssm_operators(lam_re, lam_im, log_dt, b_re, b_im, c_re, c_im, d):
    lr, li = lam_re.astype(F32), lam_im.astype(F32)
    dt = jnp.exp(log_dt.astype(F32))[..., None]
    mag = jnp.exp(lr * dt)
    ar, ai = mag * jnp.cos(li * dt), mag * jnp.sin(li * dt)
    nr, ni = ar - 1.0, ai
    den = lr * lr + li * li
    zr = (nr * lr + ni * li) / den
    zi = (ni * lr - nr * li) / den
    br, bi = b_re.astype(F32), b_im.astype(F32)
    bbr = zr[..., None] * br - zi[..., None] * bi
    bbi = zr[..., None] * bi + zi[..., None] * br
    cr, ci = c_re.astype(F32), c_im.astype(F32)

    pr, pi = [jnp.ones_like(ar)], [jnp.zeros_like(ar)]
    for _ in range(SSM_CHUNK):
        nr_, ni_ = _cmul(pr[-1], pi[-1], ar, ai)
        pr.append(nr_)
        pi.append(ni_)
    prr, pir = jnp.stack(pr[SSM_CHUNK - 1::-1]), jnp.stack(pi[SSM_CHUNK - 1::-1])
    pr, pi = jnp.stack(pr), jnp.stack(pi)

    T, G, H, P = SSM_CHUNK, SSM_N_GROUPS, SSM_GROUP, SSM_STATE
    NB, GB = SSM_N_BUNDLES, SSM_BUNDLE
    L = lr.shape[0]
    car = cr[None] * pr[:, :, :, None, :] - ci[None] * pi[:, :, :, None, :]
    cai = cr[None] * pi[:, :, :, None, :] + ci[None] * pr[:, :, :, None, :]
    ca = jnp.concatenate([car[:T], cai[:T]], -1).transpose(1, 2, 0, 3, 4).reshape(L * G, T * H, 2 * P)
    bb = jnp.concatenate([bbr, -bbi], 2).transpose(0, 1, 3, 2).reshape(L * G, H, 2 * P)
    kall = _lag_kernels(ca, bb)
    kall = kall.at[:, :, 0:H].add(d.astype(F32).reshape(L * G, 1, H) * jnp.eye(H, dtype=F32))
    a_lag = kall.reshape(L, NB, GB, H, T, H).transpose(0, 1, 4, 3, 2, 5).reshape(L * NB, T, H, GB * H)

    er = prr[..., None] * bbr[None] - pir[..., None] * bbi[None]
    ei = prr[..., None] * bbi[None] + pir[..., None] * bbr[None]
    eg = jnp.concatenate([er.transpose(1, 2, 0, 4, 3), ei.transpose(1, 2, 0, 4, 3)], -1)
    eg = eg.reshape(L * NB, GB, T * H, 2 * P)

    cg = jnp.concatenate([car[1:].transpose(1, 2, 4, 0, 3), -cai[1:].transpose(1, 2, 4, 0, 3)], 2)
    cg = cg.reshape(L * NB, GB, 2 * P, T * H)

    sr, si = [pr[T]], [pi[T]]
    for _ in range(SSM_SCAN_STEPS - 1):
        nr_, ni_ = _cmul(sr[-1], si[-1], sr[-1], si[-1])
        sr.append(nr_)
        si.append(ni_)
    sr.append(jnp.zeros_like(ar))
    si.append(jnp.zeros_like(ar))
    sr, si = jnp.stack(sr, 2), jnp.stack(si, 2)
    lay = lambda t: t.reshape(L, NB, GB, 8, 2 * P).transpose(0, 1, 3, 2, 4).reshape(L * NB, 8, GB * 2 * P)
    ar_t = lay(jnp.concatenate([sr, sr], -1))
    ai_t = lay(jnp.concatenate([-si, si], -1))

    spread = np.zeros((T, H, T, GB, H), np.float32)
    for i in range(T):
        for h in range(H):
            spread[i, h, i, :, h] = 1.0
    spread = jnp.asarray(spread.reshape(T * H, T * GB * H), BF16)
    return a_lag.astype(BF16), eg.astype(BF16), cg.astype(BF16), spread, ar_t, ai_t


def _deepnorm_ln(x, branch, g, b, alpha):
    y = alpha * x + branch
    mu = y.mean(-1, keepdims=True)
    yc = y - mu
    var = jnp.square(yc).mean(-1, keepdims=True)
    return yc * lax.rsqrt(var + LN_EPS) * g + b


OUT_TM = 512


def _outproj_kernel(alpha, ya_ref, yp_ref, ys_ref, glu_ref, w_ref, x_ref, g_ref, b_ref, o_ref, tok_ref):
    for b in range(SSM_N_BUNDLES):
        for j in range(SSM_CHUNK):
            rows = ys_ref[b, :, j * LANES:(j + 1) * LANES].astype(F32)
            tok_ref[b, pl.ds(j, OUT_TM // SSM_CHUNK, stride=SSM_CHUNK), :] = rows
    ys = jnp.concatenate([tok_ref[b].astype(BF16) for b in range(SSM_N_BUNDLES)], axis=1)
    ab = jnp.dot(ys, glu_ref[...].astype(BF16), preferred_element_type=F32)
    y_ssm = (ab[:, :SSM_WIDTH] * jax.nn.sigmoid(ab[:, SSM_WIDTH:])).astype(BF16)
    o_p, o_s = ATTN_WIDTH, ATTN_WIDTH + POOL_WIDTH
    acc = jnp.dot(ya_ref[...], w_ref[0:o_p, :], preferred_element_type=F32)
    acc += jnp.dot(yp_ref[...], w_ref[o_p:o_s, :], preferred_element_type=F32)
    acc += jnp.dot(y_ssm, w_ref[o_s:, :], preferred_element_type=F32)
    o_ref[...] = _deepnorm_ln(x_ref[...], acc, g_ref[...], b_ref[...], alpha)


def _out_proj(ya, yp, ys8, glu_w2d, w_out_bf, layer, x, g, b, alpha):
    m = x.shape[0]
    row = lambda width: pl.BlockSpec((OUT_TM, width), lambda i: (i, 0))
    vec = pl.BlockSpec((1, D_MODEL), lambda i: (0, 0))
    return pl.pallas_call(
        functools.partial(_outproj_kernel, alpha),
        out_shape=jax.ShapeDtypeStruct((m, D_MODEL), F32),
        grid=(m // OUT_TM,),
        in_specs=[
            row(ATTN_WIDTH), row(POOL_WIDTH),
            pl.BlockSpec((SSM_N_BUNDLES, OUT_TM // SSM_CHUNK, SSM_BW), lambda i: (0, i, 0)),
            pl.BlockSpec((SSM_WIDTH, 2 * SSM_WIDTH), lambda i: (layer, 0)),
            _resident((D_MODEL, D_MODEL), lambda i: (layer, 0)),
            row(D_MODEL), vec, vec,
        ],
        out_specs=row(D_MODEL),
        scratch_shapes=[pltpu.VMEM((SSM_N_BUNDLES, OUT_TM, LANES), F32)],
        compiler_params=_params("parallel"),
        name="out_proj_ln",
    )(ya, yp, ys8, glu_w2d, w_out_bf, x, g, b)


UP_TM = 1024
UP_TN = 512
UP_NJ = -(-D_FF // UP_TN)
FF_PAD = UP_NJ * UP_TN
UP_SHIFT = FF_PAD - D_FF
HALO = 8
CONV_ROWS = 8


def _ffn_up_kernel(x_ref, xp_ref, wv_ref, wg_ref, cv_ref, cg_ref, o_ref, xe_ref, hv_ref, hg_ref):
    @pl.when(pl.program_id(1) == 0)
    def _():
        xe_ref[0:HALO, :] = xp_ref[...].astype(BF16)
        xe_ref[HALO:, :] = x_ref[...].astype(BF16)

    xe = xe_ref[...]
    seq_start = (pl.program_id(0) % (SEQ // UP_TM)) == 0
    rows = lax.broadcasted_iota(jnp.int32, (HALO, 1), 0)

    def conv(w_ref, c_ref, h_ref):
        h_ref[...] = jnp.dot(xe, w_ref[...], preferred_element_type=F32)

        def tap(shift):
            h = h_ref[HALO - shift:HALO - shift + UP_TM, :]
            head = jnp.where(seq_start & (rows < shift), 0.0, h[:HALO])
            return jnp.concatenate([head, h[HALO:]], axis=0)

        return c_ref[3:4, :] + tap(2) * c_ref[0:1, :] + tap(1) * c_ref[1:2, :] + h_ref[HALO:, :] * c_ref[2:3, :]

    gate = jax.nn.silu(conv(wg_ref, cg_ref, hg_ref))
    act = (gate * conv(wv_ref, cv_ref, hv_ref)).astype(BF16)
    last = pl.num_programs(1) - 1

    @pl.when(pl.program_id(1) != last)
    def _():
        o_ref[...] = act

    @pl.when(pl.program_id(1) == last)
    def _():
        o_ref[...] = jnp.concatenate([act[:, UP_SHIFT:], jnp.zeros((UP_TM, UP_SHIFT), BF16)], axis=1)


def _ffn_up(x, w_up_bf, layer, conv8):
    m = x.shape[0]
    col = lambda c0, j: pl.multiple_of(c0 + jnp.minimum(j * UP_TN, D_FF - UP_TN), LANES)
    window = lambda rows, r0, c0: pl.BlockSpec((pl.Element(rows), pl.Element(UP_TN)),
                                               lambda i, j: (r0, col(c0, j)))
    return pl.pallas_call(
        _ffn_up_kernel,
        out_shape=jax.ShapeDtypeStruct((m, FF_PAD), BF16),
        grid=(m // UP_TM, UP_NJ),
        in_specs=[
            pl.BlockSpec((UP_TM, D_MODEL), lambda i, j: (i, 0)),
            pl.BlockSpec((HALO, D_MODEL), lambda i, j: (jnp.maximum(i * (UP_TM // HALO) - 1, 0), 0)),
            window(D_MODEL, layer * D_MODEL, 0), window(D_MODEL, layer * D_MODEL, D_FF),
            window(CONV_ROWS, layer * CONV_ROWS, 0), window(CONV_ROWS, layer * CONV_ROWS, D_FF),
        ],
        out_specs=pl.BlockSpec((UP_TM, UP_TN), lambda i, j: (i, j)),
        scratch_shapes=[pltpu.VMEM((HALO + UP_TM, D_MODEL), BF16),
                        pltpu.VMEM((HALO + UP_TM, UP_TN), F32), pltpu.VMEM((HALO + UP_TM, UP_TN), F32)],
        compiler_params=_params("parallel", "arbitrary"),
        name="ffn_up_conv_gate",
    )(x, x, w_up_bf, w_up_bf, conv8, conv8)


DOWN_TM = 256


def _ffn_down_kernel(alpha, a_ref, w_ref, x_ref, g_ref, b_ref, o_ref):
    f = jnp.dot(a_ref[...], w_ref[...], preferred_element_type=F32)
    o_ref[...] = _deepnorm_ln(x_ref[...], f, g_ref[...], b_ref[...], alpha)


def _ffn_down(act, w_down_bf, layer, x, g, b, alpha):
    m = x.shape[0]
    vec = pl.BlockSpec((1, D_MODEL), lambda i: (0, 0))
    return pl.pallas_call(
        functools.partial(_ffn_down_kernel, alpha),
        out_shape=jax.ShapeDtypeStruct((m, D_MODEL), F32),
        grid=(m // DOWN_TM,),
        in_specs=[
            pl.BlockSpec((DOWN_TM, D_FF), lambda i: (i, 0)),
            _resident((D_FF, D_MODEL), lambda i: (layer, 0)),
            pl.BlockSpec((DOWN_TM, D_MODEL), lambda i: (i, 0)),
            vec, vec,
        ],
        out_specs=pl.BlockSpec((DOWN_TM, D_MODEL), lambda i: (i, 0)),
        compiler_params=_params("parallel"),
        name="ffn_down_ln",
    )(act, w_down_bf, x, g, b)


def kernel(x, w_in, attn_sinks, pool_w, pool_scale, ssm_lam_re, ssm_lam_im, ssm_log_dt, ssm_b_re, ssm_b_im,
           ssm_c_re, ssm_c_im, ssm_d, ssm_glu_w, w_out, ln1_g, ln1_b, ffn_w_up, ffn_conv_w, ffn_conv_b,
           ffn_w_down, ln2_g, ln2_b):
    bsz, s_len, _ = x.shape
    assert s_len == SEQ and x.shape[2] == D_MODEL
    depth = w_in.shape[0]
    alpha = (2 * depth) ** 0.25
    m = bsz * s_len
    cos_t, sa_t, sb_t = _rope_tables()
    xf = x.reshape(m, D_MODEL).astype(F32)

    w_out_bf = _cast_bf16(w_out.astype(F32).reshape(depth * D_MODEL, D_MODEL), 1024)
    w_up_bf = _cast_bf16(ffn_w_up.astype(F32).reshape(depth * D_MODEL, 2 * D_FF), 256)
    w_down_bf = _cast_bf16(ffn_w_down.astype(F32).reshape(depth * D_FF, D_MODEL), D_FF // 8)
    conv8 = jnp.concatenate([ffn_conv_w.astype(F32), ffn_conv_b.astype(F32)[:, None, :],
                             jnp.zeros((depth, CONV_ROWS - 4, 2 * D_FF), F32)], axis=1)
    conv8 = conv8.reshape(depth * CONV_ROWS, 2 * D_FF)
    w_in2d = w_in.astype(F32).reshape(depth * D_MODEL, 2 * HALF_WIDTH)
    glu_w2d = ssm_glu_w.astype(F32).reshape(depth * SSM_WIDTH, 2 * SSM_WIDTH)
    pool_w3d = pool_w.astype(F32).reshape(depth * len(POOL_WINDOWS), POOL_GROUP, POOL_GROUP)
    vec = lambda a: a.astype(F32).reshape(1, -1)
    ssm_ops = _ssm_operators(ssm_lam_re, ssm_lam_im, ssm_log_dt, ssm_b_re, ssm_b_im, ssm_c_re, ssm_c_im, ssm_d)

    for l in range(depth):
        q, k2, v2, pu, su = _in_proj(xf, w_in2d, l, cos_t, sa_t, sb_t)
        y_attn = _attention(q, k2, v2, attn_sinks[l].astype(F32))
        y_pool = _pool(pu, pool_w3d, l, vec(pool_scale[l]))

        y8 = _ssm(su, l, *ssm_ops)

        x1 = _out_proj(y_attn, y_pool, y8, glu_w2d, w_out_bf, l, xf,
                       vec(ln1_g[l]), vec(ln1_b[l]), alpha)
        act = _ffn_up(x1, w_up_bf, l, conv8)
        xf = _ffn_down(act, w_down_bf, l, x1, vec(ln2_g[l]), vec(ln2_b[l]), alpha)

    return xf.reshape(bsz, s_len, D_MODEL).astype(x.dtype)
```

```python
import functools

import jax
import numpy as np
import jax.numpy as jnp
from jax import lax
from jax.experimental import pallas as pl
from jax.experimental.pallas import tpu as pltpu

F32 = jnp.float32
BF16 = jnp.bfloat16

D_MODEL = 2048
SEQ = 2048
HEAD_DIM = 64
N_Q_HEADS = 16
N_KV_HEADS = 4
ATTN_WIDTH = N_Q_HEADS * HEAD_DIM
KV_WIDTH = N_KV_HEADS * HEAD_DIM
ATTN_BLOCK = 128
ROPE_THETA = 10000.0
POOL_WINDOWS = (2, 4, 8, 16)
POOL_GROUP = 128
POOL_WIDTH = 512
SSM_WIDTH = 512
SSM_GROUP = 16
SSM_N_GROUPS = 32
SSM_STATE = 64
SSM_CHUNK = 16
LN_EPS = 1e-5
D_FF = 5504

LANES = 128
SSM_BUNDLE = LANES // SSM_GROUP
SSM_N_BUNDLES = SSM_N_GROUPS // SSM_BUNDLE
SSM_BW = SSM_CHUNK * LANES
SSM_SW = SSM_BUNDLE * 2 * SSM_STATE
HALF_WIDTH = ATTN_WIDTH + KV_WIDTH
KV2_WIDTH = 2 * KV_WIDTH
VMEM_LIMIT = 56 * 1024 * 1024


def _params(*sem):
    return pltpu.CompilerParams(dimension_semantics=sem, vmem_limit_bytes=VMEM_LIMIT)


def _resident(block, index_map):
    return pl.BlockSpec(block, index_map, pipeline_mode=pl.Buffered(1))


def _cast_kernel(x_ref, o_ref):
    o_ref[...] = x_ref[...].astype(BF16)


def _cast_bf16(w, block_rows):
    rows, cols = w.shape
    return pl.pallas_call(
        _cast_kernel,
        out_shape=jax.ShapeDtypeStruct((rows, cols), BF16),
        grid=(rows // block_rows,),
        in_specs=[pl.BlockSpec((block_rows, cols), lambda i: (i, 0))],
        out_specs=pl.BlockSpec((block_rows, cols), lambda i: (i, 0)),
        compiler_params=_params("parallel"),
        name="cast_bf16",
    )(w)


IN_TM = 512
CAST_ROWS = 256


def _cast_weight_once(w_ref, wbf_ref):
    @pl.when(pl.program_id(0) == 0)
    def _():
        for r in range(0, w_ref.shape[0], CAST_ROWS):
            wbf_ref[r:r + CAST_ROWS, :] = w_ref[r:r + CAST_ROWS, :].astype(BF16)


def _dup_heads(pair):
    lo = lax.broadcasted_iota(jnp.int32, pair.shape, 1) < HEAD_DIM
    swapped = pltpu.roll(pair, HEAD_DIM, 1)
    return jnp.where(lo, pair, swapped), jnp.where(lo, swapped, pair)


def _inproj_qk_kernel(x_ref, w_ref, cos_ref, sa_ref, sb_ref, q_ref, k_ref, wbf_ref):
    _cast_weight_once(w_ref, wbf_ref)
    acc = jnp.dot(x_ref[...].astype(BF16), wbf_ref[...], preferred_element_type=F32)
    cos, sa, sb = cos_ref[...], sa_ref[...], sb_ref[...]
    nq = ATTN_WIDTH // LANES
    for c in range(HALF_WIDTH // LANES):
        a = acc[:, c * LANES:(c + 1) * LANES]
        r = a * cos + pltpu.roll(a, LANES - 32, 1) * sa + pltpu.roll(a, 32, 1) * sb
        if c < nq:
            q_ref[:, c * LANES:(c + 1) * LANES] = (r * (HEAD_DIM ** -0.5)).astype(BF16)
        else:
            ka, kb = _dup_heads(r)
            g = 2 * (c - nq)
            k_ref[:, g * LANES:(g + 1) * LANES] = ka.astype(BF16)
            k_ref[:, (g + 1) * LANES:(g + 2) * LANES] = kb.astype(BF16)


def _inproj_vps_kernel(x_ref, w_ref, v_ref, p_ref, s_ref, wbf_ref, tok_ref):
    _cast_weight_once(w_ref, wbf_ref)
    acc = jnp.dot(x_ref[...].astype(BF16), wbf_ref[...], preferred_element_type=F32)
    for c in range(KV_WIDTH // LANES):
        va, vb = _dup_heads(acc[:, c * LANES:(c + 1) * LANES])
        v_ref[:, 2 * c * LANES:(2 * c + 1) * LANES] = va.astype(BF16)
        v_ref[:, (2 * c + 1) * LANES:(2 * c + 2) * LANES] = vb.astype(BF16)
    p_ref[...] = acc[:, KV_WIDTH:KV_WIDTH + POOL_WIDTH].astype(BF16)
    for b in range(SSM_N_BUNDLES):
        c0 = KV_WIDTH + POOL_WIDTH + b * LANES
        tok_ref[b] = acc[:, c0:c0 + LANES]
        for j in range(SSM_CHUNK):
            rows = tok_ref[b, pl.ds(j, IN_TM // SSM_CHUNK, stride=SSM_CHUNK), :]
            s_ref[b, :, j * LANES:(j + 1) * LANES] = rows.astype(BF16)


def _in_proj(x, w_in2d, layer, cos, sa, sb):
    m = x.shape[0]
    nseq = SEQ // IN_TM
    x_spec = pl.BlockSpec((IN_TM, D_MODEL), lambda i: (i, 0))
    tab = pl.BlockSpec((IN_TM, LANES), lambda i: (i % nseq, 0))
    row = lambda width: pl.BlockSpec((IN_TM, width), lambda i: (i, 0))
    wscratch = [pltpu.VMEM((D_MODEL, HALF_WIDTH), BF16)]
    q, k2 = pl.pallas_call(
        _inproj_qk_kernel,
        out_shape=(jax.ShapeDtypeStruct((m, ATTN_WIDTH), BF16), jax.ShapeDtypeStruct((m, KV2_WIDTH), BF16)),
        grid=(m // IN_TM,),
        in_specs=[x_spec, _resident((D_MODEL, HALF_WIDTH), lambda i: (layer, 0)), tab, tab, tab],
        out_specs=(row(ATTN_WIDTH), row(KV2_WIDTH)),
        scratch_shapes=wscratch,
        compiler_params=_params("arbitrary"),
        name="in_proj_qk",
    )(x, w_in2d, cos, sa, sb)
    v2, pu, su = pl.pallas_call(
        _inproj_vps_kernel,
        out_shape=(jax.ShapeDtypeStruct((m, KV2_WIDTH), BF16), jax.ShapeDtypeStruct((m, POOL_WIDTH), BF16),
                   jax.ShapeDtypeStruct((SSM_N_BUNDLES, m // SSM_CHUNK, SSM_BW), BF16)),
        grid=(m // IN_TM,),
        in_specs=[x_spec, _resident((D_MODEL, HALF_WIDTH), lambda i: (layer, 1))],
        out_specs=(row(KV2_WIDTH), row(POOL_WIDTH),
                   pl.BlockSpec((SSM_N_BUNDLES, IN_TM // SSM_CHUNK, SSM_BW), lambda i: (0, i, 0))),
        scratch_shapes=wscratch + [pltpu.VMEM((SSM_N_BUNDLES, IN_TM, LANES), F32)],
        compiler_params=_params("arbitrary"),
        name="in_proj_vps",
    )(x, w_in2d)
    return q, k2, v2, pu, su


def _rope_tables():
    half = HEAD_DIM // 2
    inv = ROPE_THETA ** (-jnp.arange(half, dtype=F32) / half)
    ang = jnp.arange(SEQ, dtype=F32)[:, None] * inv[None, :]
    cos, sin = jnp.cos(ang), jnp.sin(ang)
    zero = jnp.zeros_like(sin)
    reps = LANES // HEAD_DIM
    cos_t = jnp.tile(jnp.concatenate([cos, cos], -1), (1, reps))
    sa_t = jnp.tile(jnp.concatenate([-sin, zero], -1), (1, reps))
    sb_t = jnp.tile(jnp.concatenate([zero, sin], -1), (1, reps))
    return cos_t, sa_t, sb_t


ATTN_TQ = 512
ATTN_QB = ATTN_TQ // ATTN_BLOCK
KEYS = 2 * ATTN_BLOCK


def _attn_kernel(sink_ref, q_ref, kc_ref, vc_ref, kp_ref, vp_ref, o_ref, kbuf, vbuf):
    kbuf[0:ATTN_BLOCK, :] = kp_ref[...]
    kbuf[ATTN_BLOCK:, :] = kc_ref[...]
    vbuf[0:ATTN_BLOCK, :] = vp_ref[...]
    vbuf[ATTN_BLOCK:, :] = vc_ref[...]
    seq_start = pl.program_id(1) == 0

    row = lax.broadcasted_iota(jnp.int32, (ATTN_BLOCK, KEYS), 0)
    col = lax.broadcasted_iota(jnp.int32, (ATTN_BLOCK, KEYS), 1)
    dist = row + ATTN_BLOCK - col
    band = (dist >= 0) & (dist < ATTN_BLOCK)
    band_first = band & (jnp.logical_not(seq_start) | (col >= ATTN_BLOCK))
    lo_kv = lax.broadcasted_iota(jnp.int32, (KEYS, LANES), 1) < HEAD_DIM
    lo_out = lax.broadcasted_iota(jnp.int32, (ATTN_BLOCK, LANES), 1) < HEAD_DIM
    zero_kv = jnp.zeros((KEYS, LANES), BF16)

    def split_heads(x):
        return jnp.concatenate([jnp.where(lo_kv, x, zero_kv), jnp.where(lo_kv, zero_kv, x)], axis=0)

    for qb in range(ATTN_QB):
        valid = band_first if qb == 0 else band
        r0 = qb * ATTN_BLOCK
        for g in range(N_KV_HEADS):
            k2 = split_heads(kbuf[r0:r0 + KEYS, g * LANES:(g + 1) * LANES])
            v2 = split_heads(vbuf[r0:r0 + KEYS, g * LANES:(g + 1) * LANES])
            c0 = 2 * g * LANES
            qq = jnp.concatenate([q_ref[r0:r0 + ATTN_BLOCK, c0:c0 + LANES],
                                  q_ref[r0:r0 + ATTN_BLOCK, c0 + LANES:c0 + 2 * LANES]], axis=0)
            s4 = lax.dot_general(qq, k2, (((1,), (1,)), ((), ())), preferred_element_type=F32)
            p_rows, dens = [], []
            for pr in range(2):
                ps = []
                for hh in range(2):
                    sink = sink_ref[4 * g + 2 * pr + hh]
                    s = jnp.where(valid, s4[pr * ATTN_BLOCK:(pr + 1) * ATTN_BLOCK, hh * KEYS:(hh + 1) * KEYS], -1e30)
                    mx = jnp.maximum(s.max(-1, keepdims=True), sink)
                    p = jnp.exp(s - mx)
                    dens.append(p.sum(-1, keepdims=True) + jnp.exp(sink - mx))
                    ps.append(p.astype(BF16))
                p_rows.append(jnp.concatenate(ps, axis=1))
            o4 = jnp.dot(jnp.concatenate(p_rows, axis=0), v2, preferred_element_type=F32)
            for pr in range(2):
                o = o4[pr * ATTN_BLOCK:(pr + 1) * ATTN_BLOCK] / jnp.where(lo_out, dens[2 * pr], dens[2 * pr + 1])
                o_ref[r0:r0 + ATTN_BLOCK, c0 + pr * LANES:c0 + (pr + 1) * LANES] = o.astype(BF16)


def _attention(q, k2, v2, sinks):
    m = q.shape[0]
    nq = SEQ // ATTN_TQ
    cur = lambda b, i: (b * nq + i, 0)
    prev = lambda b, i: (jnp.maximum((b * nq + i) * ATTN_QB - 1, 0), 0)
    return pl.pallas_call(
        _attn_kernel,
        out_shape=jax.ShapeDtypeStruct((m, ATTN_WIDTH), BF16),
        grid=(m // SEQ, nq),
        in_specs=[
            pl.BlockSpec(memory_space=pltpu.SMEM),
            pl.BlockSpec((ATTN_TQ, ATTN_WIDTH), cur),
            pl.BlockSpec((ATTN_TQ, KV2_WIDTH), cur),
            pl.BlockSpec((ATTN_TQ, KV2_WIDTH), cur),
            pl.BlockSpec((ATTN_BLOCK, KV2_WIDTH), prev),
            pl.BlockSpec((ATTN_BLOCK, KV2_WIDTH), prev),
        ],
        out_specs=pl.BlockSpec((ATTN_TQ, ATTN_WIDTH), cur),
        scratch_shapes=[pltpu.VMEM((ATTN_TQ + ATTN_BLOCK, KV2_WIDTH), BF16),
                        pltpu.VMEM((ATTN_TQ + ATTN_BLOCK, KV2_WIDTH), BF16)],
        compiler_params=_params("parallel", "arbitrary"),
        name="swa_attention",
    )(sinks, q, k2, v2, k2, v2)


def _shift_rows(x, d, rows):
    return jnp.where(rows >= d, pltpu.roll(x, d, 0), 0.0)


def _pool_kernel(u_ref, w_ref, scale_ref, o_ref):
    gi = pl.program_id(1)
    u = u_ref[...].astype(F32)
    rows = lax.broadcasted_iota(jnp.int32, u.shape, 0)
    t1 = lax.broadcasted_iota(jnp.int32, (SEQ, 1), 0).astype(F32) + 1.0
    for idx, w in enumerate(POOL_WINDOWS):
        @pl.when(gi == idx)
        def _(w=w):
            s, d = u, 1
            while d < w:
                s = s + _shift_rows(s, d, rows)
                d *= 2
            mean = s / jnp.minimum(t1, float(w))
            y = jnp.dot((mean - u).astype(BF16), w_ref[0].astype(BF16), preferred_element_type=F32)
            o_ref[...] = (y * scale_ref[...]).astype(BF16)


def _pool(pu, pool_w3d, layer, pool_scale_l):
    m = pu.shape[0]
    return pl.pallas_call(
        _pool_kernel,
        out_shape=jax.ShapeDtypeStruct((m, POOL_WIDTH), BF16),
        grid=(m // SEQ, len(POOL_WINDOWS)),
        in_specs=[
            pl.BlockSpec((SEQ, POOL_GROUP), lambda b, g: (b, g)),
            pl.BlockSpec((1, POOL_GROUP, POOL_GROUP), lambda b, g: (layer * len(POOL_WINDOWS) + g, 0, 0)),
            pl.BlockSpec((1, POOL_GROUP), lambda b, g: (0, g)),
        ],
        out_specs=pl.BlockSpec((SEQ, POOL_GROUP), lambda b, g: (b, g)),
        compiler_params=_params("parallel", "arbitrary"),
        name="multiscale_pool",
    )(pu, pool_w3d, pool_scale_l)


SSM_SCAN_STEPS = 7
SSM_COLS = 2 * LANES


def _ssm_kernel(u_ref, a_ref, eg_ref, cg_ref, x_ref, ar_ref, ai_ref, o_ref, km_ref, em_ref, cm_ref):
    T, GB, H = SSM_CHUNK, SSM_BUNDLE, SSM_GROUP
    sh = H.bit_length() - 1

    @pl.when(pl.program_id(0) == 0)
    def _():
        em_ref[...] = jnp.zeros_like(em_ref)

    row_g = lax.broadcasted_iota(jnp.int32, (LANES, LANES), 0) >> sh
    col_g = lax.broadcasted_iota(jnp.int32, (LANES, LANES), 1) >> sh
    zero_blk = jnp.zeros((LANES, LANES), BF16)
    lag_blk = [jnp.where(row_g == col_g, jnp.concatenate([a_ref[0, t]] * GB, axis=0), zero_blk) for t in range(T)]
    for i in range(T):
        for j in range(min(i | 1, T - 1) + 1):
            km_ref[j * LANES:(j + 1) * LANES, i * LANES:(i + 1) * LANES] = lag_blk[i - j] if j <= i else zero_blk

    for j in range(T):
        for g in range(GB):
            r0 = j * LANES + g * H
            em_ref[r0:r0 + H, g * LANES:(g + 1) * LANES] = eg_ref[0, g, j * H:(j + 1) * H, :]

    lane_g = (lax.broadcasted_iota(jnp.int32, (LANES, SSM_BW), 1) & (LANES - 1)) >> sh
    for g in range(GB):
        spread = jnp.dot(cg_ref[0, g], x_ref[...], preferred_element_type=F32)
        cm_ref[g * LANES:(g + 1) * LANES, :] = jnp.where(lane_g == g, spread, 0.0).astype(BF16)

    u = u_ref[0]
    e = jnp.dot(u, em_ref[...], preferred_element_type=F32)
    nchunk = SEQ // SSM_CHUNK
    rows = lax.broadcasted_iota(jnp.int32, (e.shape[0], LANES), 0) % nchunk
    zprev = []
    for g in range(SSM_BUNDLE):
        eg = e[:, g * LANES:(g + 1) * LANES]
        for k in range(SSM_SCAN_STEPS):
            s = _shift_rows(eg, 1 << k, rows)
            sl = slice(g * LANES, (g + 1) * LANES)
            eg = eg + ar_ref[0, k:k + 1, sl] * s + ai_ref[0, k:k + 1, sl] * pltpu.roll(s, SSM_STATE, 1)
        zprev.append(_shift_rows(eg, 1, rows).astype(BF16))
    zp = jnp.concatenate(zprev, axis=1)

    for mblk in range(SSM_BW // SSM_COLS):
        c0, kdim = mblk * SSM_COLS, (mblk + 1) * SSM_COLS
        y = jnp.dot(u[:, :kdim], km_ref[0:kdim, c0:c0 + SSM_COLS], preferred_element_type=F32)
        y = y + jnp.dot(zp, cm_ref[:, c0:c0 + SSM_COLS], preferred_element_type=F32)
        o_ref[0, :, c0:c0 + SSM_COLS] = jax.nn.gelu(y).astype(BF16)


def _ssm(u8, layer, a_lag, eg, cg, spread, ar, ai):
    nb, rows, _ = u8.shape
    T, GB, H, Q = SSM_CHUNK, SSM_BUNDLE, SSM_GROUP, 2 * SSM_STATE
    par = lambda *dims: pl.BlockSpec((1,) + dims, lambda i: (layer * nb + i,) + (0,) * len(dims))
    act = pl.BlockSpec((1, rows, SSM_BW), lambda i: (i, 0, 0))
    return pl.pallas_call(
        _ssm_kernel,
        out_shape=jax.ShapeDtypeStruct((nb, rows, SSM_BW), BF16),
        grid=(nb,),
        in_specs=[act, par(T, H, LANES), par(GB, T * H, Q), par(GB, Q, T * H),
                  pl.BlockSpec((T * H, SSM_BW), lambda i: (0, 0)), par(8, SSM_SW), par(8, SSM_SW)],
        out_specs=act,
        scratch_shapes=[pltpu.VMEM((SSM_BW, SSM_BW), BF16), pltpu.VMEM((SSM_BW, SSM_SW), BF16),
                        pltpu.VMEM((SSM_SW, SSM_BW), BF16)],
        compiler_params=_params("arbitrary"),
        name="s5_chunked_ssm",
    )(u8, a_lag, eg, cg, spread, ar, ai)


def _cmul(xr, xi, yr, yi):
    return xr * yr - xi * yi, xr * yi + xi * yr


def _lag_kernel(ca_ref, bb_ref, o_ref):
    for g in range(ca_ref.shape[0]):
        o_ref[g] = lax.dot_general(bb_ref[g], ca_ref[g], (((1,), (1,)), ((), ())),
                                   precision=lax.Precision.HIGHEST, preferred_element_type=F32)


def _lag_kernels(ca, bb):
    n, rows, q = ca.shape
    h = bb.shape[1]
    spec = lambda a, b: pl.BlockSpec((SSM_BUNDLE, a, b), lambda i: (i, 0, 0))
    return pl.pallas_call(
        _lag_kernel,
        out_shape=jax.ShapeDtypeStruct((n, h, rows), F32),
        grid=(n // SSM_BUNDLE,),
        in_specs=[spec(rows, q), spec(h, q)],
        out_specs=spec(h, rows),
        compiler_params=_params("parallel"),
        name="ssm_lag_kernels",
    )(ca, bb)


def _ssm_operators(lam_re, lam_im, log_dt, b_re, b_im, c_re, c_im, d):
    lr, li = lam_re.astype(F32), lam_im.astype(F32)
    dt = jnp.exp(log_dt.astype(F32))[..., None]
    mag = jnp.exp(lr * dt)
    ar, ai = mag * jnp.cos(li * dt), mag * jnp.sin(li * dt)
    nr, ni = ar - 1.0, ai
    den = lr * lr + li * li
    zr = (nr * lr + ni * li) / den
    zi = (ni * lr - nr * li) / den
    br, bi = b_re.astype(F32), b_im.astype(F32)
    bbr = zr[..., None] * br - zi[..., None] * bi
    bbi = zr[..., None] * bi + zi[..., None] * br
    cr, ci = c_re.astype(F32), c_im.astype(F32)

    pr, pi = [jnp.ones_like(ar)], [jnp.zeros_like(ar)]
    for _ in range(SSM_CHUNK):
        nr_, ni_ = _cmul(pr[-1], pi[-1], ar, ai)
        pr.append(nr_)
        pi.append(ni_)
    prr, pir = jnp.stack(pr[SSM_CHUNK - 1::-1]), jnp.stack(pi[SSM_CHUNK - 1::-1])
    pr, pi = jnp.stack(pr), jnp.stack(pi)

    T, G, H, P = SSM_CHUNK, SSM_N_GROUPS, SSM_GROUP, SSM_STATE
    NB, GB = SSM_N_BUNDLES, SSM_BUNDLE
    L = lr.shape[0]
    car = cr[None] * pr[:, :, :, None, :] - ci[None] * pi[:, :, :, None, :]
    cai = cr[None] * pi[:, :, :, None, :] + ci[None] * pr[:, :, :, None, :]
    ca = jnp.concatenate([car[:T], cai[:T]], -1).transpose(1, 2, 0, 3, 4).reshape(L * G, T * H, 2 * P)
    bb = jnp.concatenate([bbr, -bbi], 2).transpose(0, 1, 3, 2).reshape(L * G, H, 2 * P)
    kall = _lag_kernels(ca, bb)
    kall = kall.at[:, :, 0:H].add(d.astype(F32).reshape(L * G, 1, H) * jnp.eye(H, dtype=F32))
    a_lag = kall.reshape(L, NB, GB, H, T, H).transpose(0, 1, 4, 3, 2, 5).reshape(L * NB, T, H, GB * H)

    er = prr[..., None] * bbr[None] - pir[..., None] * bbi[None]
    ei = prr[..., None] * bbi[None] + pir[..., None] * bbr[None]
    eg = jnp.concatenate([er.transpose(1, 2, 0, 4, 3), ei.transpose(1, 2, 0, 4, 3)], -1)
    eg = eg.reshape(L * NB, GB, T * H, 2 * P)

    cg = jnp.concatenate([car[1:].transpose(1, 2, 4, 0, 3), -cai[1:].transpose(1, 2, 4, 0, 3)], 2)
    cg = cg.reshape(L * NB, GB, 2 * P, T * H)

    sr, si = [pr[T]], [pi[T]]
    for _ in range(SSM_SCAN_STEPS - 1):
        nr_, ni_ = _cmul(sr[-1], si[-1], sr[-1], si[-1])
        sr.append(nr_)
        si.append(ni_)
    sr.append(jnp.zeros_like(ar))
    si.append(jnp.zeros_like(ar))
    sr, si = jnp.stack(sr, 2), jnp.stack(si, 2)
    lay = lambda t: t.reshape(L, NB, GB, 8, 2 * P).transpose(0, 1, 3, 2, 4).reshape(L * NB, 8, GB * 2 * P)
    ar_t = lay(jnp.concatenate([sr, sr], -1))
    ai_t = lay(jnp.concatenate([-si, si], -1))

    spread = np.zeros((T, H, T, GB, H), np.float32)
    for i in range(T):
        for h in range(H):
            spread[i, h, i, :, h] = 1.0
    spread = jnp.asarray(spread.reshape(T * H, T * GB * H), BF16)
    return a_lag.astype(BF16), eg.astype(BF16), cg.astype(BF16), spread, ar_t, ai_t


def _deepnorm_ln(x, branch, g, b, alpha):
    y = alpha * x + branch
    mu = y.mean(-1, keepdims=True)
    yc = y - mu
    var = jnp.square(yc).mean(-1, keepdims=True)
    return yc * lax.rsqrt(var + LN_EPS) * g + b


OUT_TM = 512


def _outproj_kernel(alpha, ya_ref, yp_ref, ys_ref, glu_ref, w_ref, x_ref, g_ref, b_ref, o_ref, ob_ref, tok_ref):
    for b in range(SSM_N_BUNDLES):
        for j in range(SSM_CHUNK):
            rows = ys_ref[b, :, j * LANES:(j + 1) * LANES].astype(F32)
            tok_ref[b, pl.ds(j, OUT_TM // SSM_CHUNK, stride=SSM_CHUNK), :] = rows
    ys = jnp.concatenate([tok_ref[b].astype(BF16) for b in range(SSM_N_BUNDLES)], axis=1)
    ab = jnp.dot(ys, glu_ref[...].astype(BF16), preferred_element_type=F32)
    y_ssm = (ab[:, :SSM_WIDTH] * jax.nn.sigmoid(ab[:, SSM_WIDTH:])).astype(BF16)
    o_p, o_s = ATTN_WIDTH, ATTN_WIDTH + POOL_WIDTH
    acc = jnp.dot(ya_ref[...], w_ref[0:o_p, :], preferred_element_type=F32)
    acc += jnp.dot(yp_ref[...], w_ref[o_p:o_s, :], preferred_element_type=F32)
    acc += jnp.dot(y_ssm, w_ref[o_s:, :], preferred_element_type=F32)
    x1 = _deepnorm_ln(x_ref[...], acc, g_ref[...], b_ref[...], alpha)
    o_ref[...] = x1
    ob_ref[...] = x1.astype(BF16)


def _out_proj(ya, yp, ys8, glu_w2d, w_out_bf, layer, x, g, b, alpha):
    m = x.shape[0]
    row = lambda width: pl.BlockSpec((OUT_TM, width), lambda i: (i, 0))
    vec = pl.BlockSpec((1, D_MODEL), lambda i: (0, 0))
    return pl.pallas_call(
        functools.partial(_outproj_kernel, alpha),
        out_shape=(jax.ShapeDtypeStruct((m, D_MODEL), F32), jax.ShapeDtypeStruct((m, D_MODEL), BF16)),
        grid=(m // OUT_TM,),
        in_specs=[
            row(ATTN_WIDTH), row(POOL_WIDTH),
            pl.BlockSpec((SSM_N_BUNDLES, OUT_TM // SSM_CHUNK, SSM_BW), lambda i: (0, i, 0)),
            pl.BlockSpec((SSM_WIDTH, 2 * SSM_WIDTH), lambda i: (layer, 0)),
            _resident((D_MODEL, D_MODEL), lambda i: (layer, 0)),
            row(D_MODEL), vec, vec,
        ],
        out_specs=(row(D_MODEL), row(D_MODEL)),
        scratch_shapes=[pltpu.VMEM((SSM_N_BUNDLES, OUT_TM, LANES), F32)],
        compiler_params=_params("parallel"),
        name="out_proj_ln",
    )(ya, yp, ys8, glu_w2d, w_out_bf, x, g, b)


UP_TM = 1024
UP_TN = 512
UP_NJ = -(-D_FF // UP_TN)
FF_PAD = UP_NJ * UP_TN
UP_SHIFT = FF_PAD - D_FF
HALO = 8
CONV_ROWS = 8


def _ffn_up_kernel(x_ref, wv_ref, wg_ref, cv_ref, cg_ref, o_ref, wvb_ref, wgb_ref, hv_ref, hg_ref):
    j, i = pl.program_id(0), pl.program_id(1)

    @pl.when(i == 0)
    def _():
        for r in range(0, D_MODEL, CAST_ROWS):
            wvb_ref[r:r + CAST_ROWS, :] = wv_ref[r:r + CAST_ROWS, :].astype(BF16)
            wgb_ref[r:r + CAST_ROWS, :] = wg_ref[r:r + CAST_ROWS, :].astype(BF16)
        hv_ref[0:HALO, :] = jnp.zeros((HALO, UP_TN), F32)
        hg_ref[0:HALO, :] = jnp.zeros((HALO, UP_TN), F32)

    @pl.when(i != 0)
    def _():
        hv_ref[0:HALO, :] = hv_ref[UP_TM:, :]
        hg_ref[0:HALO, :] = hg_ref[UP_TM:, :]

    x = x_ref[...]
    seq_start = (i % (SEQ // UP_TM)) == 0
    rows = lax.broadcasted_iota(jnp.int32, (HALO, 1), 0)

    def conv(wb_ref, c_ref, h_ref):
        h_ref[HALO:, :] = jnp.dot(x, wb_ref[...], preferred_element_type=F32)

        def tap(shift):
            h = h_ref[HALO - shift:HALO - shift + UP_TM, :]
            head = jnp.where(seq_start & (rows < shift), 0.0, h[:HALO])
            return jnp.concatenate([head, h[HALO:]], axis=0)

        return c_ref[3:4, :] + tap(2) * c_ref[0:1, :] + tap(1) * c_ref[1:2, :] + h_ref[HALO:, :] * c_ref[2:3, :]

    gate = jax.nn.silu(conv(wgb_ref, cg_ref, hg_ref))
    act = (gate * conv(wvb_ref, cv_ref, hv_ref)).astype(BF16)
    last = pl.num_programs(0) - 1

    @pl.when(j != last)
    def _():
        o_ref[...] = act

    @pl.when(j == last)
    def _():
        o_ref[...] = jnp.concatenate([act[:, UP_SHIFT:], jnp.zeros((UP_TM, UP_SHIFT), BF16)], axis=1)


def _ffn_up(x_bf, w_up2d, layer, conv8):
    m = x_bf.shape[0]
    col = lambda c0, j: pl.multiple_of(c0 + jnp.minimum(j * UP_TN, D_FF - UP_TN), LANES)
    window = lambda rows, r0, c0: pl.BlockSpec((pl.Element(rows), pl.Element(UP_TN)),
                                               lambda j, i: (r0, col(c0, j)))
    return pl.pallas_call(
        _ffn_up_kernel,
        out_shape=jax.ShapeDtypeStruct((m, FF_PAD), BF16),
        grid=(UP_NJ, m // UP_TM),
        in_specs=[
            pl.BlockSpec((UP_TM, D_MODEL), lambda j, i: (i, 0)),
            window(D_MODEL, layer * D_MODEL, 0), window(D_MODEL, layer * D_MODEL, D_FF),
            window(CONV_ROWS, layer * CONV_ROWS, 0), window(CONV_ROWS, layer * CONV_ROWS, D_FF),
        ],
        out_specs=pl.BlockSpec((UP_TM, UP_TN), lambda j, i: (i, j)),
        scratch_shapes=[pltpu.VMEM((D_MODEL, UP_TN), BF16), pltpu.VMEM((D_MODEL, UP_TN), BF16),
                        pltpu.VMEM((HALO + UP_TM, UP_TN), F32), pltpu.VMEM((HALO + UP_TM, UP_TN), F32)],
        compiler_params=_params("arbitrary", "arbitrary"),
        name="ffn_up_conv_gate",
    )(x_bf, w_up2d, w_up2d, conv8, conv8)


DOWN_TM = 256


def _ffn_down_kernel(alpha, a_ref, w_ref, x_ref, g_ref, b_ref, o_ref):
    f = jnp.dot(a_ref[...], w_ref[...], preferred_element_type=F32)
    o_ref[...] = _deepnorm_ln(x_ref[...], f, g_ref[...], b_ref[...], alpha)


def _ffn_down(act, w_down_bf, layer, x, g, b, alpha):
    m = x.shape[0]
    vec = pl.BlockSpec((1, D_MODEL), lambda i: (0, 0))
    return pl.pallas_call(
        functools.partial(_ffn_down_kernel, alpha),
        out_shape=jax.ShapeDtypeStruct((m, D_MODEL), F32),
        grid=(m // DOWN_TM,),
        in_specs=[
            pl.BlockSpec((DOWN_TM, D_FF), lambda i: (i, 0)),
            _resident((D_FF, D_MODEL), lambda i: (layer, 0)),
            pl.BlockSpec((DOWN_TM, D_MODEL), lambda i: (i, 0)),
            vec, vec,
        ],
        out_specs=pl.BlockSpec((DOWN_TM, D_MODEL), lambda i: (i, 0)),
        compiler_params=_params("parallel"),
        name="ffn_down_ln",
    )(act, w_down_bf, x, g, b)


def kernel(x, w_in, attn_sinks, pool_w, pool_scale, ssm_lam_re, ssm_lam_im, ssm_log_dt, ssm_b_re, ssm_b_im,
           ssm_c_re, ssm_c_im, ssm_d, ssm_glu_w, w_out, ln1_g, ln1_b, ffn_w_up, ffn_conv_w, ffn_conv_b,
           ffn_w_down, ln2_g, ln2_b):
    bsz, s_len, _ = x.shape
    assert s_len == SEQ and x.shape[2] == D_MODEL
    depth = w_in.shape[0]
    alpha = (2 * depth) ** 0.25
    m = bsz * s_len
    cos_t, sa_t, sb_t = _rope_tables()
    xf = x.reshape(m, D_MODEL).astype(F32)

    w_out_bf = _cast_bf16(w_out.astype(F32).reshape(depth * D_MODEL, D_MODEL), 1024)
    w_up2d = ffn_w_up.astype(F32).reshape(depth * D_MODEL, 2 * D_FF)
    w_down_bf = _cast_bf16(ffn_w_down.astype(F32).reshape(depth * D_FF, D_MODEL), D_FF // 8)
    conv8 = jnp.concatenate([ffn_conv_w.astype(F32), ffn_conv_b.astype(F32)[:, None, :],
                             jnp.zeros((depth, CONV_ROWS - 4, 2 * D_FF), F32)], axis=1)
    conv8 = conv8.reshape(depth * CONV_ROWS, 2 * D_FF)
    w_in2d = w_in.astype(F32).reshape(depth * D_MODEL, 2 * HALF_WIDTH)
    glu_w2d = ssm_glu_w.astype(F32).reshape(depth * SSM_WIDTH, 2 * SSM_WIDTH)
    pool_w3d = pool_w.astype(F32).reshape(depth * len(POOL_WINDOWS), POOL_GROUP, POOL_GROUP)
    vec = lambda a: a.astype(F32).reshape(1, -1)
    ssm_ops = _ssm_operators(ssm_lam_re, ssm_lam_im, ssm_log_dt, ssm_b_re, ssm_b_im, ssm_c_re, ssm_c_im, ssm_d)

    for l in range(depth):
        q, k2, v2, pu, su = _in_proj(xf, w_in2d, l, cos_t, sa_t, sb_t)
        y_attn = _attention(q, k2, v2, attn_sinks[l].astype(F32))
        y_pool = _pool(pu, pool_w3d, l, vec(pool_scale[l]))

        y8 = _ssm(su, l, *ssm_ops)

        x1, x1_bf = _out_proj(y_attn, y_pool, y8, glu_w2d, w_out_bf, l, xf,
                              vec(ln1_g[l]), vec(ln1_b[l]), alpha)
        act = _ffn_up(x1_bf, w_up2d, l, conv8)
        xf = _ffn_down(act, w_down_bf, l, x1, vec(ln2_g[l]), vec(ln2_b[l]), alpha)

    return xf.reshape(bsz, s_len, D_MODEL).astype(x.dtype)
```

```python
import functools

import jax
import numpy as np
import jax.numpy as jnp
from jax import lax
from jax.experimental import pallas as pl
from jax.experimental.pallas import tpu as pltpu

F32 = jnp.float32
BF16 = jnp.bfloat16

D_MODEL = 2048
SEQ = 2048
HEAD_DIM = 64
N_Q_HEADS = 16
N_KV_HEADS = 4
ATTN_WIDTH = N_Q_HEADS * HEAD_DIM
KV_WIDTH = N_KV_HEADS * HEAD_DIM
ATTN_BLOCK = 128
ROPE_THETA = 10000.0
POOL_WINDOWS = (2, 4, 8, 16)
POOL_GROUP = 128
POOL_WIDTH = 512
SSM_WIDTH = 512
SSM_GROUP = 16
SSM_N_GROUPS = 32
SSM_STATE = 64
SSM_CHUNK = 16
LN_EPS = 1e-5
D_FF = 5504

LANES = 128
SSM_BUNDLE = LANES // SSM_GROUP
SSM_N_BUNDLES = SSM_N_GROUPS // SSM_BUNDLE
SSM_BW = SSM_CHUNK * LANES
SSM_SW = SSM_BUNDLE * 2 * SSM_STATE
HALF_WIDTH = ATTN_WIDTH + KV_WIDTH
KV2_WIDTH = 2 * KV_WIDTH
VMEM_LIMIT = 56 * 1024 * 1024


def _params(*sem):
    return pltpu.CompilerParams(dimension_semantics=sem, vmem_limit_bytes=VMEM_LIMIT)


def _resident(block, index_map):
    return pl.BlockSpec(block, index_map, pipeline_mode=pl.Buffered(1))


IN_TM = 512
CAST_ROWS = 256


def _cast_weight_once(w_ref, wbf_ref):
    @pl.when(pl.program_id(0) == 0)
    def _():
        for r in range(0, w_ref.shape[0], CAST_ROWS):
            wbf_ref[r:r + CAST_ROWS, :] = w_ref[r:r + CAST_ROWS, :].astype(BF16)


def _dup_heads(pair):
    lo = lax.broadcasted_iota(jnp.int32, pair.shape, 1) < HEAD_DIM
    swapped = pltpu.roll(pair, HEAD_DIM, 1)
    return jnp.where(lo, pair, swapped), jnp.where(lo, swapped, pair)


def _inproj_kernel(x_ref, wa_ref, wb_ref, cos_ref, sa_ref, sb_ref, wo_ref, q_ref, k_ref, v_ref, p_ref, s_ref,
                   wobf_ref, wabf_ref, wbbf_ref, tok_ref):
    _cast_weight_once(wa_ref, wabf_ref)
    _cast_weight_once(wb_ref, wbbf_ref)
    wobf_ref[...] = wo_ref[...].astype(BF16)
    xb = x_ref[...].astype(BF16)
    acc = jnp.dot(xb, wabf_ref[...], preferred_element_type=F32)
    cos, sa, sb = cos_ref[...], sa_ref[...], sb_ref[...]
    nq = ATTN_WIDTH // LANES
    for c in range(HALF_WIDTH // LANES):
        a = acc[:, c * LANES:(c + 1) * LANES]
        r = a * cos + pltpu.roll(a, LANES - 32, 1) * sa + pltpu.roll(a, 32, 1) * sb
        if c < nq:
            q_ref[:, c * LANES:(c + 1) * LANES] = (r * (HEAD_DIM ** -0.5)).astype(BF16)
        else:
            ka, kb = _dup_heads(r)
            g = 2 * (c - nq)
            k_ref[:, g * LANES:(g + 1) * LANES] = ka.astype(BF16)
            k_ref[:, (g + 1) * LANES:(g + 2) * LANES] = kb.astype(BF16)

    acc = jnp.dot(xb, wbbf_ref[...], preferred_element_type=F32)
    for c in range(KV_WIDTH // LANES):
        va, vb = _dup_heads(acc[:, c * LANES:(c + 1) * LANES])
        v_ref[:, 2 * c * LANES:(2 * c + 1) * LANES] = va.astype(BF16)
        v_ref[:, (2 * c + 1) * LANES:(2 * c + 2) * LANES] = vb.astype(BF16)
    p_ref[...] = acc[:, KV_WIDTH:KV_WIDTH + POOL_WIDTH].astype(BF16)
    for b in range(SSM_N_BUNDLES):
        c0 = KV_WIDTH + POOL_WIDTH + b * LANES
        tok_ref[b] = acc[:, c0:c0 + LANES]
        for j in range(SSM_CHUNK):
            rows = tok_ref[b, pl.ds(j, IN_TM // SSM_CHUNK, stride=SSM_CHUNK), :]
            s_ref[b, :, j * LANES:(j + 1) * LANES] = rows.astype(BF16)


def _in_proj(x, w_in2d, w_out2d, layer, cos, sa, sb):
    m = x.shape[0]
    nseq = SEQ // IN_TM
    x_spec = pl.BlockSpec((IN_TM, D_MODEL), lambda i: (i, 0))
    tab = pl.BlockSpec((IN_TM, LANES), lambda i: (i % nseq, 0))
    row = lambda width: pl.BlockSpec((IN_TM, width), lambda i: (i, 0))
    steps = m // IN_TM
    wo_rows = D_MODEL // steps
    return pl.pallas_call(
        _inproj_kernel,
        out_shape=(jax.ShapeDtypeStruct((m, ATTN_WIDTH), BF16), jax.ShapeDtypeStruct((m, KV2_WIDTH), BF16),
                   jax.ShapeDtypeStruct((m, KV2_WIDTH), BF16), jax.ShapeDtypeStruct((m, POOL_WIDTH), BF16),
                   jax.ShapeDtypeStruct((SSM_N_BUNDLES, m // SSM_CHUNK, SSM_BW), BF16),
                   jax.ShapeDtypeStruct((D_MODEL, D_MODEL), BF16)),
        grid=(steps,),
        in_specs=[x_spec, _resident((D_MODEL, HALF_WIDTH), lambda i: (layer, 0)),
                  _resident((D_MODEL, HALF_WIDTH), lambda i: (layer, 1)), tab, tab, tab,
                  pl.BlockSpec((wo_rows, D_MODEL), lambda i: (layer * steps + i, 0))],
        out_specs=(row(ATTN_WIDTH), row(KV2_WIDTH), row(KV2_WIDTH), row(POOL_WIDTH),
                   pl.BlockSpec((SSM_N_BUNDLES, IN_TM // SSM_CHUNK, SSM_BW), lambda i: (0, i, 0)),
                   pl.BlockSpec((wo_rows, D_MODEL), lambda i: (i, 0))),
        scratch_shapes=[pltpu.VMEM((D_MODEL, HALF_WIDTH), BF16), pltpu.VMEM((D_MODEL, HALF_WIDTH), BF16),
                        pltpu.VMEM((SSM_N_BUNDLES, IN_TM, LANES), F32)],
        compiler_params=_params("arbitrary"),
        name="in_proj",
    )(x, w_in2d, w_in2d, cos, sa, sb, w_out2d)


def _rope_tables():
    half = HEAD_DIM // 2
    inv = ROPE_THETA ** (-jnp.arange(half, dtype=F32) / half)
    ang = jnp.arange(SEQ, dtype=F32)[:, None] * inv[None, :]
    cos, sin = jnp.cos(ang), jnp.sin(ang)
    zero = jnp.zeros_like(sin)
    reps = LANES // HEAD_DIM
    cos_t = jnp.tile(jnp.concatenate([cos, cos], -1), (1, reps))
    sa_t = jnp.tile(jnp.concatenate([-sin, zero], -1), (1, reps))
    sb_t = jnp.tile(jnp.concatenate([zero, sin], -1), (1, reps))
    return cos_t, sa_t, sb_t


ATTN_TQ = 512
ATTN_QB = ATTN_TQ // ATTN_BLOCK
KEYS = 2 * ATTN_BLOCK


def _attn_kernel(sink_ref, q_ref, kc_ref, vc_ref, kp_ref, vp_ref, o_ref, kbuf, vbuf):
    kbuf[0:ATTN_BLOCK, :] = kp_ref[...]
    kbuf[ATTN_BLOCK:, :] = kc_ref[...]
    vbuf[0:ATTN_BLOCK, :] = vp_ref[...]
    vbuf[ATTN_BLOCK:, :] = vc_ref[...]
    seq_start = pl.program_id(1) == 0

    row = lax.broadcasted_iota(jnp.int32, (ATTN_BLOCK, KEYS), 0)
    col = lax.broadcasted_iota(jnp.int32, (ATTN_BLOCK, KEYS), 1)
    dist = row + ATTN_BLOCK - col
    band = (dist >= 0) & (dist < ATTN_BLOCK)
    band_first = band & (jnp.logical_not(seq_start) | (col >= ATTN_BLOCK))
    lo_kv = lax.broadcasted_iota(jnp.int32, (KEYS, LANES), 1) < HEAD_DIM
    lo_out = lax.broadcasted_iota(jnp.int32, (ATTN_BLOCK, LANES), 1) < HEAD_DIM
    zero_kv = jnp.zeros((KEYS, LANES), BF16)

    def split_heads(x):
        return jnp.concatenate([jnp.where(lo_kv, x, zero_kv), jnp.where(lo_kv, zero_kv, x)], axis=0)

    for qb in range(ATTN_QB):
        valid = band_first if qb == 0 else band
        r0 = qb * ATTN_BLOCK
        for g in range(N_KV_HEADS):
            k2 = split_heads(kbuf[r0:r0 + KEYS, g * LANES:(g + 1) * LANES])
            v2 = split_heads(vbuf[r0:r0 + KEYS, g * LANES:(g + 1) * LANES])
            c0 = 2 * g * LANES
            qq = jnp.concatenate([q_ref[r0:r0 + ATTN_BLOCK, c0:c0 + LANES],
                                  q_ref[r0:r0 + ATTN_BLOCK, c0 + LANES:c0 + 2 * LANES]], axis=0)
            s4 = lax.dot_general(qq, k2, (((1,), (1,)), ((), ())), preferred_element_type=F32)
            p_rows, dens = [], []
            for pr in range(2):
                ps = []
                for hh in range(2):
                    sink = sink_ref[4 * g + 2 * pr + hh]
                    s = jnp.where(valid, s4[pr * ATTN_BLOCK:(pr + 1) * ATTN_BLOCK, hh * KEYS:(hh + 1) * KEYS], -1e30)
                    mx = jnp.maximum(s.max(-1, keepdims=True), sink)
                    p = jnp.exp(s - mx)
                    dens.append(p.sum(-1, keepdims=True) + jnp.exp(sink - mx))
                    ps.append(p.astype(BF16))
                p_rows.append(jnp.concatenate(ps, axis=1))
            o4 = jnp.dot(jnp.concatenate(p_rows, axis=0), v2, preferred_element_type=F32)
            for pr in range(2):
                o = o4[pr * ATTN_BLOCK:(pr + 1) * ATTN_BLOCK] / jnp.where(lo_out, dens[2 * pr], dens[2 * pr + 1])
                o_ref[r0:r0 + ATTN_BLOCK, c0 + pr * LANES:c0 + (pr + 1) * LANES] = o.astype(BF16)


def _attention(q, k2, v2, sinks):
    m = q.shape[0]
    nq = SEQ // ATTN_TQ
    cur = lambda b, i: (b * nq + i, 0)
    prev = lambda b, i: (jnp.maximum((b * nq + i) * ATTN_QB - 1, 0), 0)
    return pl.pallas_call(
        _attn_kernel,
        out_shape=jax.ShapeDtypeStruct((m, ATTN_WIDTH), BF16),
        grid=(m // SEQ, nq),
        in_specs=[
            pl.BlockSpec(memory_space=pltpu.SMEM),
            pl.BlockSpec((ATTN_TQ, ATTN_WIDTH), cur),
            pl.BlockSpec((ATTN_TQ, KV2_WIDTH), cur),
            pl.BlockSpec((ATTN_TQ, KV2_WIDTH), cur),
            pl.BlockSpec((ATTN_BLOCK, KV2_WIDTH), prev),
            pl.BlockSpec((ATTN_BLOCK, KV2_WIDTH), prev),
        ],
        out_specs=pl.BlockSpec((ATTN_TQ, ATTN_WIDTH), cur),
        scratch_shapes=[pltpu.VMEM((ATTN_TQ + ATTN_BLOCK, KV2_WIDTH), BF16),
                        pltpu.VMEM((ATTN_TQ + ATTN_BLOCK, KV2_WIDTH), BF16)],
        compiler_params=_params("parallel", "arbitrary"),
        name="swa_attention",
    )(sinks, q, k2, v2, k2, v2)


def _shift_rows(x, d, rows):
    return jnp.where(rows >= d, pltpu.roll(x, d, 0), 0.0)


def _pool_kernel(u_ref, w_ref, scale_ref, o_ref):
    gi = pl.program_id(1)
    u = u_ref[...].astype(F32)
    rows = lax.broadcasted_iota(jnp.int32, u.shape, 0)
    t1 = lax.broadcasted_iota(jnp.int32, (SEQ, 1), 0).astype(F32) + 1.0
    for idx, w in enumerate(POOL_WINDOWS):
        @pl.when(gi == idx)
        def _(w=w):
            s, d = u, 1
            while d < w:
                s = s + _shift_rows(s, d, rows)
                d *= 2
            mean = s / jnp.minimum(t1, float(w))
            y = jnp.dot((mean - u).astype(BF16), w_ref[0].astype(BF16), preferred_element_type=F32)
            o_ref[...] = (y * scale_ref[...]).astype(BF16)


def _pool(pu, pool_w3d, layer, pool_scale_l):
    m = pu.shape[0]
    return pl.pallas_call(
        _pool_kernel,
        out_shape=jax.ShapeDtypeStruct((m, POOL_WIDTH), BF16),
        grid=(m // SEQ, len(POOL_WINDOWS)),
        in_specs=[
            pl.BlockSpec((SEQ, POOL_GROUP), lambda b, g: (b, g)),
            pl.BlockSpec((1, POOL_GROUP, POOL_GROUP), lambda b, g: (layer * len(POOL_WINDOWS) + g, 0, 0)),
            pl.BlockSpec((1, POOL_GROUP), lambda b, g: (0, g)),
        ],
        out_specs=pl.BlockSpec((SEQ, POOL_GROUP), lambda b, g: (b, g)),
        compiler_params=_params("parallel", "arbitrary"),
        name="multiscale_pool",
    )(pu, pool_w3d, pool_scale_l)


SSM_SCAN_STEPS = 7
SSM_COLS = 2 * LANES


def _ssm_kernel(u_ref, a_ref, eg_ref, cg_ref, x_ref, ar_ref, ai_ref, o_ref, km_ref, em_ref, cm_ref):
    T, GB, H = SSM_CHUNK, SSM_BUNDLE, SSM_GROUP
    sh = H.bit_length() - 1

    @pl.when(pl.program_id(0) == 0)
    def _():
        em_ref[...] = jnp.zeros_like(em_ref)

    row_g = lax.broadcasted_iota(jnp.int32, (LANES, LANES), 0) >> sh
    col_g = lax.broadcasted_iota(jnp.int32, (LANES, LANES), 1) >> sh
    zero_blk = jnp.zeros((LANES, LANES), BF16)
    lag_blk = [jnp.where(row_g == col_g, jnp.concatenate([a_ref[0, t]] * GB, axis=0), zero_blk) for t in range(T)]
    for i in range(T):
        for j in range(min(i | 1, T - 1) + 1):
            km_ref[j * LANES:(j + 1) * LANES, i * LANES:(i + 1) * LANES] = lag_blk[i - j] if j <= i else zero_blk

    for j in range(T):
        for g in range(GB):
            r0 = j * LANES + g * H
            em_ref[r0:r0 + H, g * LANES:(g + 1) * LANES] = eg_ref[0, g, j * H:(j + 1) * H, :]

    lane_g = (lax.broadcasted_iota(jnp.int32, (LANES, SSM_BW), 1) & (LANES - 1)) >> sh
    for g in range(GB):
        spread = jnp.dot(cg_ref[0, g], x_ref[...], preferred_element_type=F32)
        cm_ref[g * LANES:(g + 1) * LANES, :] = jnp.where(lane_g == g, spread, 0.0).astype(BF16)

    u = u_ref[0]
    e = jnp.dot(u, em_ref[...], preferred_element_type=F32)
    nchunk = SEQ // SSM_CHUNK
    rows = lax.broadcasted_iota(jnp.int32, (e.shape[0], LANES), 0) % nchunk
    zprev = []
    for g in range(SSM_BUNDLE):
        eg = e[:, g * LANES:(g + 1) * LANES]
        for k in range(SSM_SCAN_STEPS):
            s = _shift_rows(eg, 1 << k, rows)
            sl = slice(g * LANES, (g + 1) * LANES)
            eg = eg + ar_ref[0, k:k + 1, sl] * s + ai_ref[0, k:k + 1, sl] * pltpu.roll(s, SSM_STATE, 1)
        zprev.append(_shift_rows(eg, 1, rows).astype(BF16))
    zp = jnp.concatenate(zprev, axis=1)

    for mblk in range(SSM_BW // SSM_COLS):
        c0, kdim = mblk * SSM_COLS, (mblk + 1) * SSM_COLS
        y = jnp.dot(u[:, :kdim], km_ref[0:kdim, c0:c0 + SSM_COLS], preferred_element_type=F32)
        y = y + jnp.dot(zp, cm_ref[:, c0:c0 + SSM_COLS], preferred_element_type=F32)
        o_ref[0, :, c0:c0 + SSM_COLS] = jax.nn.gelu(y).astype(BF16)


def _ssm(u8, layer, a_lag, eg, cg, spread, ar, ai):
    nb, rows, _ = u8.shape
    T, GB, H, Q = SSM_CHUNK, SSM_BUNDLE, SSM_GROUP, 2 * SSM_STATE
    par = lambda *dims: pl.BlockSpec((1,) + dims, lambda i: (layer * nb + i,) + (0,) * len(dims))
    act = pl.BlockSpec((1, rows, SSM_BW), lambda i: (i, 0, 0))
    return pl.pallas_call(
        _ssm_kernel,
        out_shape=jax.ShapeDtypeStruct((nb, rows, SSM_BW), BF16),
        grid=(nb,),
        in_specs=[act, par(T, H, LANES), par(GB, T * H, Q), par(GB, Q, T * H),
                  pl.BlockSpec((T * H, SSM_BW), lambda i: (0, 0)), par(8, SSM_SW), par(8, SSM_SW)],
        out_specs=act,
        scratch_shapes=[pltpu.VMEM((SSM_BW, SSM_BW), BF16), pltpu.VMEM((SSM_BW, SSM_SW), BF16),
                        pltpu.VMEM((SSM_SW, SSM_BW), BF16)],
        compiler_params=_params("arbitrary"),
        name="s5_chunked_ssm",
    )(u8, a_lag, eg, cg, spread, ar, ai)


def _cmul(xr, xi, yr, yi):
    return xr * yr - xi * yi, xr * yi + xi * yr


def _lag_kernel(ca_ref, bb_ref, o_ref):
    for g in range(ca_ref.shape[0]):
        o_ref[g] = lax.dot_general(bb_ref[g], ca_ref[g], (((1,), (1,)), ((), ())),
                                   precision=lax.Precision.HIGHEST, preferred_element_type=F32)


def _lag_kernels(ca, bb):
    n, rows, q = ca.shape
    h = bb.shape[1]
    spec = lambda a, b: pl.BlockSpec((SSM_BUNDLE, a, b), lambda i: (i, 0, 0))
    return pl.pallas_call(
        _lag_kernel,
        out_shape=jax.ShapeDtypeStruct((n, h, rows), F32),
        grid=(n // SSM_BUNDLE,),
        in_specs=[spec(rows, q), spec(h, q)],
        out_specs=spec(h, rows),
        compiler_params=_params("parallel"),
        name="ssm_lag_kernels",
    )(ca, bb)


def _ssm_operators(lam_re, lam_im, log_dt, b_re, b_im, c_re, c_im, d):
    lr, li = lam_re.astype(F32), lam_im.astype(F32)
    dt = jnp.exp(log_dt.astype(F32))[..., None]
    mag = jnp.exp(lr * dt)
    ar, ai = mag * jnp.cos(li * dt), mag * jnp.sin(li * dt)
    nr, ni = ar - 1.0, ai
    den = lr * lr + li * li
    zr = (nr * lr + ni * li) / den
    zi = (ni * lr - nr * li) / den
    br, bi = b_re.astype(F32), b_im.astype(F32)
    bbr = zr[..., None] * br - zi[..., None] * bi
    bbi = zr[..., None] * bi + zi[..., None] * br
    cr, ci = c_re.astype(F32), c_im.astype(F32)

    pr, pi = [jnp.ones_like(ar)], [jnp.zeros_like(ar)]
    for _ in range(SSM_CHUNK):
        nr_, ni_ = _cmul(pr[-1], pi[-1], ar, ai)
        pr.append(nr_)
        pi.append(ni_)
    prr, pir = jnp.stack(pr[SSM_CHUNK - 1::-1]), jnp.stack(pi[SSM_CHUNK - 1::-1])
    pr, pi = jnp.stack(pr), jnp.stack(pi)

    T, G, H, P = SSM_CHUNK, SSM_N_GROUPS, SSM_GROUP, SSM_STATE
    NB, GB = SSM_N_BUNDLES, SSM_BUNDLE
    L = lr.shape[0]
    car = cr[None] * pr[:, :, :, None, :] - ci[None] * pi[:, :, :, None, :]
    cai = cr[None] * pi[:, :, :, None, :] + ci[None] * pr[:, :, :, None, :]
    ca = jnp.concatenate([car[:T], cai[:T]], -1).transpose(1, 2, 0, 3, 4).reshape(L * G, T * H, 2 * P)
    bb = jnp.concatenate([bbr, -bbi], 2).transpose(0, 1, 3, 2).reshape(L * G, H, 2 * P)
    kall = _lag_kernels(ca, bb)
    kall = kall.at[:, :, 0:H].add(d.astype(F32).reshape(L * G, 1, H) * jnp.eye(H, dtype=F32))
    a_lag = kall.reshape(L, NB, GB, H, T, H).transpose(0, 1, 4, 3, 2, 5).reshape(L * NB, T, H, GB * H)

    er = prr[..., None] * bbr[None] - pir[..., None] * bbi[None]
    ei = prr[..., None] * bbi[None] + pir[..., None] * bbr[None]
    eg = jnp.concatenate([er.transpose(1, 2, 0, 4, 3), ei.transpose(1, 2, 0, 4, 3)], -1)
    eg = eg.reshape(L * NB, GB, T * H, 2 * P)

    cg = jnp.concatenate([car[1:].transpose(1, 2, 4, 0, 3), -cai[1:].transpose(1, 2, 4, 0, 3)], 2)
    cg = cg.reshape(L * NB, GB, 2 * P, T * H)

    sr, si = [pr[T]], [pi[T]]
    for _ in range(SSM_SCAN_STEPS - 1):
        nr_, ni_ = _cmul(sr[-1], si[-1], sr[-1], si[-1])
        sr.append(nr_)
        si.append(ni_)
    sr.append(jnp.zeros_like(ar))
    si.append(jnp.zeros_like(ar))
    sr, si = jnp.stack(sr, 2), jnp.stack(si, 2)
    lay = lambda t: t.reshape(L, NB, GB, 8, 2 * P).transpose(0, 1, 3, 2, 4).reshape(L * NB, 8, GB * 2 * P)
    ar_t = lay(jnp.concatenate([sr, sr], -1))
    ai_t = lay(jnp.concatenate([-si, si], -1))

    spread = np.zeros((T, H, T, GB, H), np.float32)
    for i in range(T):
        for h in range(H):
            spread[i, h, i, :, h] = 1.0
    spread = jnp.asarray(spread.reshape(T * H, T * GB * H), BF16)
    return a_lag.astype(BF16), eg.astype(BF16), cg.astype(BF16), spread, ar_t, ai_t


def _deepnorm_ln(x, branch, g, b, alpha):
    y = alpha * x + branch
    mu = y.mean(-1, keepdims=True)
    yc = y - mu
    var = jnp.square(yc).mean(-1, keepdims=True)
    return yc * lax.rsqrt(var + LN_EPS) * g + b


OUT_TM = 512


def _outproj_kernel(alpha, ya_ref, yp_ref, ys_ref, glu_ref, w_ref, x_ref, g_ref, b_ref, o_ref, ob_ref, tok_ref):
    for b in range(SSM_N_BUNDLES):
        for j in range(SSM_CHUNK):
            rows = ys_ref[b, :, j * LANES:(j + 1) * LANES].astype(F32)
            tok_ref[b, pl.ds(j, OUT_TM // SSM_CHUNK, stride=SSM_CHUNK), :] = rows
    ys = jnp.concatenate([tok_ref[b].astype(BF16) for b in range(SSM_N_BUNDLES)], axis=1)
    ab = jnp.dot(ys, glu_ref[...].astype(BF16), preferred_element_type=F32)
    y_ssm = (ab[:, :SSM_WIDTH] * jax.nn.sigmoid(ab[:, SSM_WIDTH:])).astype(BF16)
    o_p, o_s = ATTN_WIDTH, ATTN_WIDTH + POOL_WIDTH
    acc = jnp.dot(ya_ref[...], w_ref[0:o_p, :], preferred_element_type=F32)
    acc += jnp.dot(yp_ref[...], w_ref[o_p:o_s, :], preferred_element_type=F32)
    acc += jnp.dot(y_ssm, w_ref[o_s:, :], preferred_element_type=F32)
    x1 = _deepnorm_ln(x_ref[...], acc, g_ref[...], b_ref[...], alpha)
    o_ref[...] = x1
    ob_ref[...] = x1.astype(BF16)


def _out_proj(ya, yp, ys8, glu_w2d, layer, w_out_bf, x, g, b, alpha):
    m = x.shape[0]
    row = lambda width: pl.BlockSpec((OUT_TM, width), lambda i: (i, 0))
    vec = pl.BlockSpec((1, D_MODEL), lambda i: (0, 0))
    return pl.pallas_call(
        functools.partial(_outproj_kernel, alpha),
        out_shape=(jax.ShapeDtypeStruct((m, D_MODEL), F32), jax.ShapeDtypeStruct((m, D_MODEL), BF16)),
        grid=(m // OUT_TM,),
        in_specs=[
            row(ATTN_WIDTH), row(POOL_WIDTH),
            pl.BlockSpec((SSM_N_BUNDLES, OUT_TM // SSM_CHUNK, SSM_BW), lambda i: (0, i, 0)),
            pl.BlockSpec((SSM_WIDTH, 2 * SSM_WIDTH), lambda i: (layer, 0)),
            _resident((D_MODEL, D_MODEL), lambda i: (0, 0)),
            row(D_MODEL), vec, vec,
        ],
        out_specs=(row(D_MODEL), row(D_MODEL)),
        scratch_shapes=[pltpu.VMEM((SSM_N_BUNDLES, OUT_TM, LANES), F32)],
        compiler_params=_params("parallel"),
        name="out_proj_ln",
    )(ya, yp, ys8, glu_w2d, w_out_bf, x, g, b)


UP_TM = 1024
UP_TN = 512
UP_NJ = -(-D_FF // UP_TN)
FF_PAD = UP_NJ * UP_TN
UP_SHIFT = FF_PAD - D_FF
HALO = 8
CONV_ROWS = 8
DOWN_BLOCK_ROWS = 128
DOWN_BLOCKS = D_FF // DOWN_BLOCK_ROWS


def _ffn_up_kernel(x_ref, wv_ref, wg_ref, cv_ref, cg_ref, wd_ref, o_ref, wdb_ref, wvb_ref, wgb_ref, hv_ref, hg_ref):
    j, i = pl.program_id(0), pl.program_id(1)

    @pl.when(j * pl.num_programs(1) + i < DOWN_BLOCKS)
    def _():
        wdb_ref[...] = wd_ref[...].astype(BF16)

    @pl.when(i == 0)
    def _():
        for r in range(0, D_MODEL, CAST_ROWS):
            wvb_ref[r:r + CAST_ROWS, :] = wv_ref[r:r + CAST_ROWS, :].astype(BF16)
            wgb_ref[r:r + CAST_ROWS, :] = wg_ref[r:r + CAST_ROWS, :].astype(BF16)
        hv_ref[0:HALO, :] = jnp.zeros((HALO, UP_TN), F32)
        hg_ref[0:HALO, :] = jnp.zeros((HALO, UP_TN), F32)

    @pl.when(i != 0)
    def _():
        hv_ref[0:HALO, :] = hv_ref[UP_TM:, :]
        hg_ref[0:HALO, :] = hg_ref[UP_TM:, :]

    x = x_ref[...]
    seq_start = (i % (SEQ // UP_TM)) == 0
    rows = lax.broadcasted_iota(jnp.int32, (HALO, 1), 0)

    def conv(wb_ref, c_ref, h_ref):
        h_ref[HALO:, :] = jnp.dot(x, wb_ref[...], preferred_element_type=F32)

        def tap(shift):
            h = h_ref[HALO - shift:HALO - shift + UP_TM, :]
            head = jnp.where(seq_start & (rows < shift), 0.0, h[:HALO])
            return jnp.concatenate([head, h[HALO:]], axis=0)

        return c_ref[3:4, :] + tap(2) * c_ref[0:1, :] + tap(1) * c_ref[1:2, :] + h_ref[HALO:, :] * c_ref[2:3, :]

    gate = jax.nn.silu(conv(wgb_ref, cg_ref, hg_ref))
    act = (gate * conv(wvb_ref, cv_ref, hv_ref)).astype(BF16)
    last = pl.num_programs(0) - 1

    @pl.when(j != last)
    def _():
        o_ref[...] = act

    @pl.when(j == last)
    def _():
        o_ref[...] = jnp.concatenate([act[:, UP_SHIFT:], jnp.zeros((UP_TM, UP_SHIFT), BF16)], axis=1)


def _ffn_up(x_bf, w_up2d, layer, conv8, w_down4d):
    m = x_bf.shape[0]
    ni = m // UP_TM
    assert UP_NJ * ni >= DOWN_BLOCKS
    col = lambda c0, j: pl.multiple_of(c0 + jnp.minimum(j * UP_TN, D_FF - UP_TN), LANES)
    window = lambda rows, r0, c0: pl.BlockSpec((pl.Element(rows), pl.Element(UP_TN)),
                                               lambda j, i: (r0, col(c0, j)))
    dblk = lambda j, i: jnp.minimum(j * ni + i, DOWN_BLOCKS - 1)
    return pl.pallas_call(
        _ffn_up_kernel,
        out_shape=(jax.ShapeDtypeStruct((m, FF_PAD), BF16),
                   jax.ShapeDtypeStruct((DOWN_BLOCKS, DOWN_BLOCK_ROWS, D_MODEL), BF16)),
        grid=(UP_NJ, ni),
        in_specs=[
            pl.BlockSpec((UP_TM, D_MODEL), lambda j, i: (i, 0)),
            window(D_MODEL, layer * D_MODEL, 0), window(D_MODEL, layer * D_MODEL, D_FF),
            window(CONV_ROWS, layer * CONV_ROWS, 0), window(CONV_ROWS, layer * CONV_ROWS, D_FF),
            pl.BlockSpec((None, None, DOWN_BLOCK_ROWS, D_MODEL), lambda j, i: (layer, dblk(j, i), 0, 0)),
        ],
        out_specs=(pl.BlockSpec((UP_TM, UP_TN), lambda j, i: (i, j)),
                   pl.BlockSpec((None, DOWN_BLOCK_ROWS, D_MODEL), lambda j, i: (dblk(j, i), 0, 0))),
        scratch_shapes=[pltpu.VMEM((D_MODEL, UP_TN), BF16), pltpu.VMEM((D_MODEL, UP_TN), BF16),
                        pltpu.VMEM((HALO + UP_TM, UP_TN), F32), pltpu.VMEM((HALO + UP_TM, UP_TN), F32)],
        compiler_params=_params("arbitrary", "arbitrary"),
        name="ffn_up_conv_gate",
    )(x_bf, w_up2d, w_up2d, conv8, conv8, w_down4d)


DOWN_TM = 256


def _ffn_down_kernel(alpha, a_ref, w_ref, x_ref, g_ref, b_ref, o_ref):
    f = jnp.dot(a_ref[...], w_ref[...], preferred_element_type=F32)
    o_ref[...] = _deepnorm_ln(x_ref[...], f, g_ref[...], b_ref[...], alpha)


def _ffn_down(act, w_down_bf, x, g, b, alpha):
    m = x.shape[0]
    vec = pl.BlockSpec((1, D_MODEL), lambda i: (0, 0))
    return pl.pallas_call(
        functools.partial(_ffn_down_kernel, alpha),
        out_shape=jax.ShapeDtypeStruct((m, D_MODEL), F32),
        grid=(m // DOWN_TM,),
        in_specs=[
            pl.BlockSpec((DOWN_TM, D_FF), lambda i: (i, 0)),
            _resident((D_FF, D_MODEL), lambda i: (0, 0)),
            pl.BlockSpec((DOWN_TM, D_MODEL), lambda i: (i, 0)),
            vec, vec,
        ],
        out_specs=pl.BlockSpec((DOWN_TM, D_MODEL), lambda i: (i, 0)),
        compiler_params=_params("parallel"),
        name="ffn_down_ln",
    )(act, w_down_bf, x, g, b)


def kernel(x, w_in, attn_sinks, pool_w, pool_scale, ssm_lam_re, ssm_lam_im, ssm_log_dt, ssm_b_re, ssm_b_im,
           ssm_c_re, ssm_c_im, ssm_d, ssm_glu_w, w_out, ln1_g, ln1_b, ffn_w_up, ffn_conv_w, ffn_conv_b,
           ffn_w_down, ln2_g, ln2_b):
    bsz, s_len, _ = x.shape
    assert s_len == SEQ and x.shape[2] == D_MODEL
    depth = w_in.shape[0]
    alpha = (2 * depth) ** 0.25
    m = bsz * s_len
    cos_t, sa_t, sb_t = _rope_tables()
    xf = x.reshape(m, D_MODEL).astype(F32)

    w_out2d = w_out.astype(F32).reshape(depth * D_MODEL, D_MODEL)
    w_up2d = ffn_w_up.astype(F32).reshape(depth * D_MODEL, 2 * D_FF)
    w_down4d = ffn_w_down.astype(F32).reshape(depth, DOWN_BLOCKS, DOWN_BLOCK_ROWS, D_MODEL)
    conv8 = jnp.concatenate([ffn_conv_w.astype(F32), ffn_conv_b.astype(F32)[:, None, :],
                             jnp.zeros((depth, CONV_ROWS - 4, 2 * D_FF), F32)], axis=1)
    conv8 = conv8.reshape(depth * CONV_ROWS, 2 * D_FF)
    w_in2d = w_in.astype(F32).reshape(depth * D_MODEL, 2 * HALF_WIDTH)
    glu_w2d = ssm_glu_w.astype(F32).reshape(depth * SSM_WIDTH, 2 * SSM_WIDTH)
    pool_w3d = pool_w.astype(F32).reshape(depth * len(POOL_WINDOWS), POOL_GROUP, POOL_GROUP)
    vec = lambda a: a.astype(F32).reshape(1, -1)
    ssm_ops = _ssm_operators(ssm_lam_re, ssm_lam_im, ssm_log_dt, ssm_b_re, ssm_b_im, ssm_c_re, ssm_c_im, ssm_d)

    for l in range(depth):
        q, k2, v2, pu, su, w_out_bf = _in_proj(xf, w_in2d, w_out2d, l, cos_t, sa_t, sb_t)
        y_attn = _attention(q, k2, v2, attn_sinks[l].astype(F32))
        y_pool = _pool(pu, pool_w3d, l, vec(pool_scale[l]))

        y8 = _ssm(su, l, *ssm_ops)

        x1, x1_bf = _out_proj(y_attn, y_pool, y8, glu_w2d, l, w_out_bf, xf,
                              vec(ln1_g[l]), vec(ln1_b[l]), alpha)
        act, w_down_bf = _ffn_up(x1_bf, w_up2d, l, conv8, w_down4d)
        xf = _ffn_down(act, w_down_bf.reshape(D_FF, D_MODEL), x1, vec(ln2_g[l]), vec(ln2_b[l]), alpha)

    return xf.reshape(bsz, s_len, D_MODEL).astype(x.dtype)
```

```python
import functools

import jax
import numpy as np
import jax.numpy as jnp
from jax import lax
from jax.experimental import pallas as pl
from jax.experimental.pallas import tpu as pltpu

F32 = jnp.float32
BF16 = jnp.bfloat16

D_MODEL = 2048
SEQ = 2048
HEAD_DIM = 64
N_Q_HEADS = 16
N_KV_HEADS = 4
ATTN_WIDTH = N_Q_HEADS * HEAD_DIM
KV_WIDTH = N_KV_HEADS * HEAD_DIM
ATTN_BLOCK = 128
ROPE_THETA = 10000.0
POOL_WINDOWS = (2, 4, 8, 16)
POOL_GROUP = 128
POOL_WIDTH = 512
SSM_WIDTH = 512
SSM_GROUP = 16
SSM_N_GROUPS = 32
SSM_STATE = 64
SSM_CHUNK = 16
LN_EPS = 1e-5
D_FF = 5504

LANES = 128
SSM_BUNDLE = LANES // SSM_GROUP
SSM_N_BUNDLES = SSM_N_GROUPS // SSM_BUNDLE
SSM_BW = SSM_CHUNK * LANES
SSM_SW = SSM_BUNDLE * 2 * SSM_STATE
HALF_WIDTH = ATTN_WIDTH + KV_WIDTH
KV2_WIDTH = 2 * KV_WIDTH
VMEM_LIMIT = 56 * 1024 * 1024


def _params(*sem):
    return pltpu.CompilerParams(dimension_semantics=sem, vmem_limit_bytes=VMEM_LIMIT)


def _resident(block, index_map):
    return pl.BlockSpec(block, index_map, pipeline_mode=pl.Buffered(1))


IN_TM = 512
CAST_ROWS = 256


def _cast_weight_once(w_ref, wbf_ref):
    @pl.when(pl.program_id(0) == 0)
    def _():
        for r in range(0, w_ref.shape[0], CAST_ROWS):
            wbf_ref[r:r + CAST_ROWS, :] = w_ref[r:r + CAST_ROWS, :].astype(BF16)


def _dup_heads(pair):
    lo = lax.broadcasted_iota(jnp.int32, pair.shape, 1) < HEAD_DIM
    swapped = pltpu.roll(pair, HEAD_DIM, 1)
    return jnp.where(lo, pair, swapped), jnp.where(lo, swapped, pair)


def _inproj_kernel(x_ref, wa_ref, wb_ref, cos_ref, sa_ref, sb_ref, wo_ref, q_ref, k_ref, v_ref, p_ref, s_ref,
                   wobf_ref, wabf_ref, wbbf_ref, tok_ref):
    _cast_weight_once(wa_ref, wabf_ref)
    _cast_weight_once(wb_ref, wbbf_ref)
    wobf_ref[...] = wo_ref[...].astype(BF16)
    xb = x_ref[...].astype(BF16)
    acc = jnp.dot(xb, wabf_ref[...], preferred_element_type=F32)
    cos, sa, sb = cos_ref[...], sa_ref[...], sb_ref[...]
    nq = ATTN_WIDTH // LANES
    for c in range(HALF_WIDTH // LANES):
        a = acc[:, c * LANES:(c + 1) * LANES]
        r = a * cos + pltpu.roll(a, LANES - 32, 1) * sa + pltpu.roll(a, 32, 1) * sb
        if c < nq:
            q_ref[:, c * LANES:(c + 1) * LANES] = (r * (HEAD_DIM ** -0.5)).astype(BF16)
        else:
            ka, kb = _dup_heads(r)
            g = 2 * (c - nq)
            k_ref[:, g * LANES:(g + 1) * LANES] = ka.astype(BF16)
            k_ref[:, (g + 1) * LANES:(g + 2) * LANES] = kb.astype(BF16)

    acc = jnp.dot(xb, wbbf_ref[...], preferred_element_type=F32)
    for c in range(KV_WIDTH // LANES):
        va, vb = _dup_heads(acc[:, c * LANES:(c + 1) * LANES])
        v_ref[:, 2 * c * LANES:(2 * c + 1) * LANES] = va.astype(BF16)
        v_ref[:, (2 * c + 1) * LANES:(2 * c + 2) * LANES] = vb.astype(BF16)
    p_ref[...] = acc[:, KV_WIDTH:KV_WIDTH + POOL_WIDTH].astype(BF16)
    for b in range(SSM_N_BUNDLES):
        c0 = KV_WIDTH + POOL_WIDTH + b * LANES
        tok_ref[b] = acc[:, c0:c0 + LANES]
        for j in range(SSM_CHUNK):
            rows = tok_ref[b, pl.ds(j, IN_TM // SSM_CHUNK, stride=SSM_CHUNK), :]
            s_ref[b, :, j * LANES:(j + 1) * LANES] = rows.astype(BF16)


def _in_proj(x, w_in2d, w_out2d, layer, cos, sa, sb):
    m = x.shape[0]
    nseq = SEQ // IN_TM
    x_spec = pl.BlockSpec((IN_TM, D_MODEL), lambda i: (i, 0))
    tab = pl.BlockSpec((IN_TM, LANES), lambda i: (i % nseq, 0))
    row = lambda width: pl.BlockSpec((IN_TM, width), lambda i: (i, 0))
    steps = m // IN_TM
    wo_rows = D_MODEL // steps
    return pl.pallas_call(
        _inproj_kernel,
        out_shape=(jax.ShapeDtypeStruct((m, ATTN_WIDTH), BF16), jax.ShapeDtypeStruct((m, KV2_WIDTH), BF16),
                   jax.ShapeDtypeStruct((m, KV2_WIDTH), BF16), jax.ShapeDtypeStruct((m, POOL_WIDTH), BF16),
                   jax.ShapeDtypeStruct((SSM_N_BUNDLES, m // SSM_CHUNK, SSM_BW), BF16),
                   jax.ShapeDtypeStruct((D_MODEL, D_MODEL), BF16)),
        grid=(steps,),
        in_specs=[x_spec, _resident((D_MODEL, HALF_WIDTH), lambda i: (layer, 0)),
                  _resident((D_MODEL, HALF_WIDTH), lambda i: (layer, 1)), tab, tab, tab,
                  pl.BlockSpec((wo_rows, D_MODEL), lambda i: (layer * steps + i, 0))],
        out_specs=(row(ATTN_WIDTH), row(KV2_WIDTH), row(KV2_WIDTH), row(POOL_WIDTH),
                   pl.BlockSpec((SSM_N_BUNDLES, IN_TM // SSM_CHUNK, SSM_BW), lambda i: (0, i, 0)),
                   pl.BlockSpec((wo_rows, D_MODEL), lambda i: (i, 0))),
        scratch_shapes=[pltpu.VMEM((D_MODEL, HALF_WIDTH), BF16), pltpu.VMEM((D_MODEL, HALF_WIDTH), BF16),
                        pltpu.VMEM((SSM_N_BUNDLES, IN_TM, LANES), F32)],
        compiler_params=_params("arbitrary"),
        name="in_proj",
    )(x, w_in2d, w_in2d, cos, sa, sb, w_out2d)


def _rope_tables():
    half = HEAD_DIM // 2
    inv = ROPE_THETA ** (-jnp.arange(half, dtype=F32) / half)
    ang = jnp.arange(SEQ, dtype=F32)[:, None] * inv[None, :]
    cos, sin = jnp.cos(ang), jnp.sin(ang)
    zero = jnp.zeros_like(sin)
    reps = LANES // HEAD_DIM
    cos_t = jnp.tile(jnp.concatenate([cos, cos], -1), (1, reps))
    sa_t = jnp.tile(jnp.concatenate([-sin, zero], -1), (1, reps))
    sb_t = jnp.tile(jnp.concatenate([zero, sin], -1), (1, reps))
    return cos_t, sa_t, sb_t


ATTN_TQ = 512
ATTN_QB = ATTN_TQ // ATTN_BLOCK
KEYS = 2 * ATTN_BLOCK


def _attn_kernel(sink_ref, q_ref, kc_ref, vc_ref, kp_ref, vp_ref, o_ref, kbuf, vbuf):
    kbuf[0:ATTN_BLOCK, :] = kp_ref[...]
    kbuf[ATTN_BLOCK:, :] = kc_ref[...]
    vbuf[0:ATTN_BLOCK, :] = vp_ref[...]
    vbuf[ATTN_BLOCK:, :] = vc_ref[...]
    seq_start = pl.program_id(1) == 0

    row = lax.broadcasted_iota(jnp.int32, (ATTN_BLOCK, KEYS), 0)
    col = lax.broadcasted_iota(jnp.int32, (ATTN_BLOCK, KEYS), 1)
    dist = row + ATTN_BLOCK - col
    band = (dist >= 0) & (dist < ATTN_BLOCK)
    band_first = band & (jnp.logical_not(seq_start) | (col >= ATTN_BLOCK))
    lo_kv = lax.broadcasted_iota(jnp.int32, (KEYS, LANES), 1) < HEAD_DIM
    lo_out = lax.broadcasted_iota(jnp.int32, (ATTN_BLOCK, LANES), 1) < HEAD_DIM
    zero_kv = jnp.zeros((KEYS, LANES), BF16)

    def split_heads(x):
        return jnp.concatenate([jnp.where(lo_kv, x, zero_kv), jnp.where(lo_kv, zero_kv, x)], axis=0)

    for qb in range(ATTN_QB):
        valid = band_first if qb == 0 else band
        r0 = qb * ATTN_BLOCK
        for g in range(N_KV_HEADS):
            k2 = split_heads(kbuf[r0:r0 + KEYS, g * LANES:(g + 1) * LANES])
            v2 = split_heads(vbuf[r0:r0 + KEYS, g * LANES:(g + 1) * LANES])
            c0 = 2 * g * LANES
            qq = jnp.concatenate([q_ref[r0:r0 + ATTN_BLOCK, c0:c0 + LANES],
                                  q_ref[r0:r0 + ATTN_BLOCK, c0 + LANES:c0 + 2 * LANES]], axis=0)
            s4 = lax.dot_general(qq, k2, (((1,), (1,)), ((), ())), preferred_element_type=F32)
            p_rows, dens = [], []
            for pr in range(2):
                ps = []
                for hh in range(2):
                    sink = sink_ref[4 * g + 2 * pr + hh]
                    s = jnp.where(valid, s4[pr * ATTN_BLOCK:(pr + 1) * ATTN_BLOCK, hh * KEYS:(hh + 1) * KEYS], -1e30)
                    mx = jnp.maximum(s.max(-1, keepdims=True), sink)
                    p = jnp.exp(s - mx)
                    dens.append(p.sum(-1, keepdims=True) + jnp.exp(sink - mx))
                    ps.append(p.astype(BF16))
                p_rows.append(jnp.concatenate(ps, axis=1))
            o4 = jnp.dot(jnp.concatenate(p_rows, axis=0), v2, preferred_element_type=F32)
            for pr in range(2):
                o = o4[pr * ATTN_BLOCK:(pr + 1) * ATTN_BLOCK] / jnp.where(lo_out, dens[2 * pr], dens[2 * pr + 1])
                o_ref[r0:r0 + ATTN_BLOCK, c0 + pr * LANES:c0 + (pr + 1) * LANES] = o.astype(BF16)


def _attention(q, k2, v2, sinks):
    m = q.shape[0]
    nq = SEQ // ATTN_TQ
    cur = lambda b, i: (b * nq + i, 0)
    prev = lambda b, i: (jnp.maximum((b * nq + i) * ATTN_QB - 1, 0), 0)
    return pl.pallas_call(
        _attn_kernel,
        out_shape=jax.ShapeDtypeStruct((m, ATTN_WIDTH), BF16),
        grid=(m // SEQ, nq),
        in_specs=[
            pl.BlockSpec(memory_space=pltpu.SMEM),
            pl.BlockSpec((ATTN_TQ, ATTN_WIDTH), cur),
            pl.BlockSpec((ATTN_TQ, KV2_WIDTH), cur),
            pl.BlockSpec((ATTN_TQ, KV2_WIDTH), cur),
            pl.BlockSpec((ATTN_BLOCK, KV2_WIDTH), prev),
            pl.BlockSpec((ATTN_BLOCK, KV2_WIDTH), prev),
        ],
        out_specs=pl.BlockSpec((ATTN_TQ, ATTN_WIDTH), cur),
        scratch_shapes=[pltpu.VMEM((ATTN_TQ + ATTN_BLOCK, KV2_WIDTH), BF16),
                        pltpu.VMEM((ATTN_TQ + ATTN_BLOCK, KV2_WIDTH), BF16)],
        compiler_params=_params("parallel", "arbitrary"),
        name="swa_attention",
    )(sinks, q, k2, v2, k2, v2)


def _shift_rows(x, d, rows):
    return jnp.where(rows >= d, pltpu.roll(x, d, 0), 0.0)


def _pool_kernel(u_ref, w_ref, scale_ref, o_ref):
    gi = pl.program_id(1)
    u = u_ref[...].astype(F32)
    rows = lax.broadcasted_iota(jnp.int32, u.shape, 0)
    t1 = lax.broadcasted_iota(jnp.int32, (SEQ, 1), 0).astype(F32) + 1.0
    for idx, w in enumerate(POOL_WINDOWS):
        @pl.when(gi == idx)
        def _(w=w):
            s, d = u, 1
            while d < w:
                s = s + _shift_rows(s, d, rows)
                d *= 2
            mean = s / jnp.minimum(t1, float(w))
            y = jnp.dot((mean - u).astype(BF16), w_ref[0].astype(BF16), preferred_element_type=F32)
            o_ref[...] = (y * scale_ref[...]).astype(BF16)


def _pool(pu, pool_w3d, layer, pool_scale_l):
    m = pu.shape[0]
    return pl.pallas_call(
        _pool_kernel,
        out_shape=jax.ShapeDtypeStruct((m, POOL_WIDTH), BF16),
        grid=(m // SEQ, len(POOL_WINDOWS)),
        in_specs=[
            pl.BlockSpec((SEQ, POOL_GROUP), lambda b, g: (b, g)),
            pl.BlockSpec((1, POOL_GROUP, POOL_GROUP), lambda b, g: (layer * len(POOL_WINDOWS) + g, 0, 0)),
            pl.BlockSpec((1, POOL_GROUP), lambda b, g: (0, g)),
        ],
        out_specs=pl.BlockSpec((SEQ, POOL_GROUP), lambda b, g: (b, g)),
        compiler_params=_params("parallel", "arbitrary"),
        name="multiscale_pool",
    )(pu, pool_w3d, pool_scale_l)


SSM_SCAN_STEPS = 7
SSM_COLS = 2 * LANES


def _ssm_kernel(u_ref, a_ref, eg_ref, cg_ref, x_ref, ar_ref, ai_ref, o_ref, km_ref, em_ref, cm_ref):
    T, GB, H = SSM_CHUNK, SSM_BUNDLE, SSM_GROUP
    sh = H.bit_length() - 1

    @pl.when(pl.program_id(0) == 0)
    def _():
        em_ref[...] = jnp.zeros_like(em_ref)

    row_g = lax.broadcasted_iota(jnp.int32, (LANES, LANES), 0) >> sh
    col_g = lax.broadcasted_iota(jnp.int32, (LANES, LANES), 1) >> sh
    zero_blk = jnp.zeros((LANES, LANES), BF16)
    lag_blk = [jnp.where(row_g == col_g, jnp.concatenate([a_ref[0, t]] * GB, axis=0), zero_blk) for t in range(T)]
    for i in range(T):
        for j in range(min(i | 1, T - 1) + 1):
            km_ref[j * LANES:(j + 1) * LANES, i * LANES:(i + 1) * LANES] = lag_blk[i - j] if j <= i else zero_blk

    for j in range(T):
        for g in range(GB):
            r0 = j * LANES + g * H
            em_ref[r0:r0 + H, g * LANES:(g + 1) * LANES] = eg_ref[0, g, j * H:(j + 1) * H, :]

    lane_g = (lax.broadcasted_iota(jnp.int32, (LANES, SSM_BW), 1) & (LANES - 1)) >> sh
    for g in range(GB):
        spread = jnp.dot(cg_ref[0, g], x_ref[...], preferred_element_type=F32)
        cm_ref[g * LANES:(g + 1) * LANES, :] = jnp.where(lane_g == g, spread, 0.0).astype(BF16)

    u = u_ref[0]
    e = jnp.dot(u, em_ref[...], preferred_element_type=F32)
    nchunk = SEQ // SSM_CHUNK
    rows = lax.broadcasted_iota(jnp.int32, (e.shape[0], LANES), 0) % nchunk
    zprev = []
    for g in range(SSM_BUNDLE):
        eg = e[:, g * LANES:(g + 1) * LANES]
        for k in range(SSM_SCAN_STEPS):
            s = _shift_rows(eg, 1 << k, rows)
            sl = slice(g * LANES, (g + 1) * LANES)
            eg = eg + ar_ref[0, k:k + 1, sl] * s + ai_ref[0, k:k + 1, sl] * pltpu.roll(s, SSM_STATE, 1)
        zprev.append(_shift_rows(eg, 1, rows).astype(BF16))
    zp = jnp.concatenate(zprev, axis=1)

    for mblk in range(SSM_BW // SSM_COLS):
        c0, kdim = mblk * SSM_COLS, (mblk + 1) * SSM_COLS
        y = jnp.dot(u[:, :kdim], km_ref[0:kdim, c0:c0 + SSM_COLS], preferred_element_type=F32)
        y = y + jnp.dot(zp, cm_ref[:, c0:c0 + SSM_COLS], preferred_element_type=F32)
        o_ref[0, :, c0:c0 + SSM_COLS] = jax.nn.gelu(y).astype(BF16)


def _ssm(u8, layer, a_lag, eg, cg, spread, ar, ai):
    nb, rows, _ = u8.shape
    T, GB, H, Q = SSM_CHUNK, SSM_BUNDLE, SSM_GROUP, 2 * SSM_STATE
    par = lambda *dims: pl.BlockSpec((1,) + dims, lambda i: (layer * nb + i,) + (0,) * len(dims))
    act = pl.BlockSpec((1, rows, SSM_BW), lambda i: (i, 0, 0))
    return pl.pallas_call(
        _ssm_kernel,
        out_shape=jax.ShapeDtypeStruct((nb, rows, SSM_BW), BF16),
        grid=(nb,),
        in_specs=[act, par(T, H, LANES), par(GB, T * H, Q), par(GB, Q, T * H),
                  pl.BlockSpec((T * H, SSM_BW), lambda i: (0, 0)), par(8, SSM_SW), par(8, SSM_SW)],
        out_specs=act,
        scratch_shapes=[pltpu.VMEM((SSM_BW, SSM_BW), BF16), pltpu.VMEM((SSM_BW, SSM_SW), BF16),
                        pltpu.VMEM((SSM_SW, SSM_BW), BF16)],
        compiler_params=_params("arbitrary"),
        name="s5_chunked_ssm",
    )(u8, a_lag, eg, cg, spread, ar, ai)


def _cmul(xr, xi, yr, yi):
    return xr * yr - xi * yi, xr * yi + xi * yr


def _lag_kernel(ca_ref, bb_ref, o_ref):
    for g in range(ca_ref.shape[0]):
        o_ref[g] = lax.dot_general(bb_ref[g], ca_ref[g], (((1,), (1,)), ((), ())),
                                   precision=lax.Precision.HIGHEST, preferred_element_type=F32)


def _lag_kernels(ca, bb):
    n, rows, q = ca.shape
    h = bb.shape[1]
    spec = lambda a, b: pl.BlockSpec((SSM_BUNDLE, a, b), lambda i: (i, 0, 0))
    return pl.pallas_call(
        _lag_kernel,
        out_shape=jax.ShapeDtypeStruct((n, h, rows), F32),
        grid=(n // SSM_BUNDLE,),
        in_specs=[spec(rows, q), spec(h, q)],
        out_specs=spec(h, rows),
        compiler_params=_params("parallel"),
        name="ssm_lag_kernels",
    )(ca, bb)


def _ssm_operators(lam_re, lam_im, log_dt, b_re, b_im, c_re, c_im, d):
    lr, li = lam_re.astype(F32), lam_im.astype(F32)
    dt = jnp.exp(log_dt.astype(F32))[..., None]
    mag = jnp.exp(lr * dt)
    ar, ai = mag * jnp.cos(li * dt), mag * jnp.sin(li * dt)
    nr, ni = ar - 1.0, ai
    den = lr * lr + li * li
    zr = (nr * lr + ni * li) / den
    zi = (ni * lr - nr * li) / den
    br, bi = b_re.astype(F32), b_im.astype(F32)
    bbr = zr[..., None] * br - zi[..., None] * bi
    bbi = zr[..., None] * bi + zi[..., None] * br
    cr, ci = c_re.astype(F32), c_im.astype(F32)

    pr, pi = [jnp.ones_like(ar)], [jnp.zeros_like(ar)]
    for _ in range(SSM_CHUNK):
        nr_, ni_ = _cmul(pr[-1], pi[-1], ar, ai)
        pr.append(nr_)
        pi.append(ni_)
    prr, pir = jnp.stack(pr[SSM_CHUNK - 1::-1]), jnp.stack(pi[SSM_CHUNK - 1::-1])
    pr, pi = jnp.stack(pr), jnp.stack(pi)

    T, G, H, P = SSM_CHUNK, SSM_N_GROUPS, SSM_GROUP, SSM_STATE
    NB, GB = SSM_N_BUNDLES, SSM_BUNDLE
    L = lr.shape[0]
    car = cr[None] * pr[:, :, :, None, :] - ci[None] * pi[:, :, :, None, :]
    cai = cr[None] * pi[:, :, :, None, :] + ci[None] * pr[:, :, :, None, :]
    ca = jnp.concatenate([car[:T], cai[:T]], -1).transpose(1, 2, 0, 3, 4).reshape(L * G, T * H, 2 * P)
    bb = jnp.concatenate([bbr, -bbi], 2).transpose(0, 1, 3, 2).reshape(L * G, H, 2 * P)
    kall = _lag_kernels(ca, bb)
    kall = kall.at[:, :, 0:H].add(d.astype(F32).reshape(L * G, 1, H) * jnp.eye(H, dtype=F32))
    a_lag = kall.reshape(L, NB, GB, H, T, H).transpose(0, 1, 4, 3, 2, 5).reshape(L * NB, T, H, GB * H)

    er = prr[..., None] * bbr[None] - pir[..., None] * bbi[None]
    ei = prr[..., None] * bbi[None] + pir[..., None] * bbr[None]
    eg = jnp.concatenate([er.transpose(1, 2, 0, 4, 3), ei.transpose(1, 2, 0, 4, 3)], -1)
    eg = eg.reshape(L * NB, GB, T * H, 2 * P)

    cg = jnp.concatenate([car[1:].transpose(1, 2, 4, 0, 3), -cai[1:].transpose(1, 2, 4, 0, 3)], 2)
    cg = cg.reshape(L * NB, GB, 2 * P, T * H)

    sr, si = [pr[T]], [pi[T]]
    for _ in range(SSM_SCAN_STEPS - 1):
        nr_, ni_ = _cmul(sr[-1], si[-1], sr[-1], si[-1])
        sr.append(nr_)
        si.append(ni_)
    sr.append(jnp.zeros_like(ar))
    si.append(jnp.zeros_like(ar))
    sr, si = jnp.stack(sr, 2), jnp.stack(si, 2)
    lay = lambda t: t.reshape(L, NB, GB, 8, 2 * P).transpose(0, 1, 3, 2, 4).reshape(L * NB, 8, GB * 2 * P)
    ar_t = lay(jnp.concatenate([sr, sr], -1))
    ai_t = lay(jnp.concatenate([-si, si], -1))

    spread = np.zeros((T, H, T, GB, H), np.float32)
    for i in range(T):
        for h in range(H):
            spread[i, h, i, :, h] = 1.0
    spread = jnp.asarray(spread.reshape(T * H, T * GB * H), BF16)
    return a_lag.astype(BF16), eg.astype(BF16), cg.astype(BF16), spread, ar_t, ai_t


def _deepnorm_ln(x, branch, g, b, alpha):
    y = alpha * x + branch
    mu = y.mean(-1, keepdims=True)
    yc = y - mu
    var = jnp.square(yc).mean(-1, keepdims=True)
    return yc * lax.rsqrt(var + LN_EPS) * g + b


OUT_TM = 512
OUT_SUB = 256


def _outproj_kernel(alpha, ya_ref, yp_ref, ys_ref, glu_ref, w_ref, x_ref, g_ref, b_ref, o_ref, ob_ref, tok_ref):
    for b in range(SSM_N_BUNDLES):
        for j in range(SSM_CHUNK):
            rows = ys_ref[b, :, j * LANES:(j + 1) * LANES].astype(F32)
            tok_ref[b, pl.ds(j, OUT_TM // SSM_CHUNK, stride=SSM_CHUNK), :] = rows
    ys = jnp.concatenate([tok_ref[b].astype(BF16) for b in range(SSM_N_BUNDLES)], axis=1)
    ab = jnp.dot(ys, glu_ref[...].astype(BF16), preferred_element_type=F32)
    y_ssm = (ab[:, :SSM_WIDTH] * jax.nn.sigmoid(ab[:, SSM_WIDTH:])).astype(BF16)
    o_p, o_s = ATTN_WIDTH, ATTN_WIDTH + POOL_WIDTH
    for r in range(0, OUT_TM, OUT_SUB):
        rs = slice(r, r + OUT_SUB)
        acc = jnp.dot(ya_ref[rs, :], w_ref[0:o_p, :], preferred_element_type=F32)
        acc += jnp.dot(yp_ref[rs, :], w_ref[o_p:o_s, :], preferred_element_type=F32)
        acc += jnp.dot(y_ssm[rs, :], w_ref[o_s:, :], preferred_element_type=F32)
        x1 = _deepnorm_ln(x_ref[rs, :], acc, g_ref[...], b_ref[...], alpha)
        o_ref[rs, :] = x1
        ob_ref[rs, :] = x1.astype(BF16)


def _out_proj(ya, yp, ys8, glu_w2d, layer, w_out_bf, x, g, b, alpha):
    m = x.shape[0]
    row = lambda width: pl.BlockSpec((OUT_TM, width), lambda i: (i, 0))
    vec = pl.BlockSpec((1, D_MODEL), lambda i: (0, 0))
    return pl.pallas_call(
        functools.partial(_outproj_kernel, alpha),
        out_shape=(jax.ShapeDtypeStruct((m, D_MODEL), F32), jax.ShapeDtypeStruct((m, D_MODEL), BF16)),
        grid=(m // OUT_TM,),
        in_specs=[
            row(ATTN_WIDTH), row(POOL_WIDTH),
            pl.BlockSpec((SSM_N_BUNDLES, OUT_TM // SSM_CHUNK, SSM_BW), lambda i: (0, i, 0)),
            pl.BlockSpec((SSM_WIDTH, 2 * SSM_WIDTH), lambda i: (layer, 0)),
            _resident((D_MODEL, D_MODEL), lambda i: (0, 0)),
            row(D_MODEL), vec, vec,
        ],
        out_specs=(row(D_MODEL), row(D_MODEL)),
        scratch_shapes=[pltpu.VMEM((SSM_N_BUNDLES, OUT_TM, LANES), F32)],
        compiler_params=_params("parallel"),
        name="out_proj_ln",
    )(ya, yp, ys8, glu_w2d, w_out_bf, x, g, b)


UP_TM = 1024
UP_TN = 512
UP_NJ = -(-D_FF // UP_TN)
FF_PAD = UP_NJ * UP_TN
UP_SHIFT = FF_PAD - D_FF
HALO = 8
CONV_ROWS = 8
DOWN_BLOCK_ROWS = 128
DOWN_BLOCKS = D_FF // DOWN_BLOCK_ROWS


def _ffn_up_kernel(x_ref, wv_ref, wg_ref, cv_ref, cg_ref, wd_ref, o_ref, wdb_ref, wvb_ref, wgb_ref, hv_ref, hg_ref):
    j, i = pl.program_id(0), pl.program_id(1)

    @pl.when(j * pl.num_programs(1) + i < DOWN_BLOCKS)
    def _():
        wdb_ref[...] = wd_ref[...].astype(BF16)

    @pl.when(i == 0)
    def _():
        for r in range(0, D_MODEL, CAST_ROWS):
            wvb_ref[r:r + CAST_ROWS, :] = wv_ref[r:r + CAST_ROWS, :].astype(BF16)
            wgb_ref[r:r + CAST_ROWS, :] = wg_ref[r:r + CAST_ROWS, :].astype(BF16)
        hv_ref[0:HALO, :] = jnp.zeros((HALO, UP_TN), F32)
        hg_ref[0:HALO, :] = jnp.zeros((HALO, UP_TN), F32)

    @pl.when(i != 0)
    def _():
        hv_ref[0:HALO, :] = hv_ref[UP_TM:, :]
        hg_ref[0:HALO, :] = hg_ref[UP_TM:, :]

    x = x_ref[...]
    seq_start = (i % (SEQ // UP_TM)) == 0
    rows = lax.broadcasted_iota(jnp.int32, (HALO, 1), 0)

    def conv(wb_ref, c_ref, h_ref):
        h_ref[HALO:, :] = jnp.dot(x, wb_ref[...], preferred_element_type=F32)

        def tap(shift):
            h = h_ref[HALO - shift:HALO - shift + UP_TM, :]
            head = jnp.where(seq_start & (rows < shift), 0.0, h[:HALO])
            return jnp.concatenate([head, h[HALO:]], axis=0)

        return c_ref[3:4, :] + tap(2) * c_ref[0:1, :] + tap(1) * c_ref[1:2, :] + h_ref[HALO:, :] * c_ref[2:3, :]

    gate = jax.nn.silu(conv(wgb_ref, cg_ref, hg_ref))
    act = (gate * conv(wvb_ref, cv_ref, hv_ref)).astype(BF16)
    last = pl.num_programs(0) - 1

    @pl.when(j != last)
    def _():
        o_ref[...] = act

    @pl.when(j == last)
    def _():
        o_ref[...] = jnp.concatenate([act[:, UP_SHIFT:], jnp.zeros((UP_TM, UP_SHIFT), BF16)], axis=1)


def _ffn_up(x_bf, w_up2d, layer, conv8, w_down4d):
    m = x_bf.shape[0]
    ni = m // UP_TM
    assert UP_NJ * ni >= DOWN_BLOCKS
    col = lambda c0, j: pl.multiple_of(c0 + jnp.minimum(j * UP_TN, D_FF - UP_TN), LANES)
    window = lambda rows, r0, c0: pl.BlockSpec((pl.Element(rows), pl.Element(UP_TN)),
                                               lambda j, i: (r0, col(c0, j)))
    dblk = lambda j, i: jnp.minimum(j * ni + i, DOWN_BLOCKS - 1)
    return pl.pallas_call(
        _ffn_up_kernel,
        out_shape=(jax.ShapeDtypeStruct((m, FF_PAD), BF16),
                   jax.ShapeDtypeStruct((DOWN_BLOCKS, DOWN_BLOCK_ROWS, D_MODEL), BF16)),
        grid=(UP_NJ, ni),
        in_specs=[
            pl.BlockSpec((UP_TM, D_MODEL), lambda j, i: (i, 0)),
            window(D_MODEL, layer * D_MODEL, 0), window(D_MODEL, layer * D_MODEL, D_FF),
            window(CONV_ROWS, layer * CONV_ROWS, 0), window(CONV_ROWS, layer * CONV_ROWS, D_FF),
            pl.BlockSpec((None, None, DOWN_BLOCK_ROWS, D_MODEL), lambda j, i: (layer, dblk(j, i), 0, 0)),
        ],
        out_specs=(pl.BlockSpec((UP_TM, UP_TN), lambda j, i: (i, j)),
                   pl.BlockSpec((None, DOWN_BLOCK_ROWS, D_MODEL), lambda j, i: (dblk(j, i), 0, 0))),
        scratch_shapes=[pltpu.VMEM((D_MODEL, UP_TN), BF16), pltpu.VMEM((D_MODEL, UP_TN), BF16),
                        pltpu.VMEM((HALO + UP_TM, UP_TN), F32), pltpu.VMEM((HALO + UP_TM, UP_TN), F32)],
        compiler_params=_params("arbitrary", "arbitrary"),
        name="ffn_up_conv_gate",
    )(x_bf, w_up2d, w_up2d, conv8, conv8, w_down4d)


DOWN_TM = 512
DOWN_SUB = 256


def _ffn_down_kernel(alpha, a_ref, w_ref, x_ref, g_ref, b_ref, o_ref):
    for r in range(0, DOWN_TM, DOWN_SUB):
        f = jnp.dot(a_ref[r:r + DOWN_SUB, :], w_ref[...], preferred_element_type=F32)
        o_ref[r:r + DOWN_SUB, :] = _deepnorm_ln(x_ref[r:r + DOWN_SUB, :], f, g_ref[...], b_ref[...], alpha)


def _ffn_down(act, w_down_bf, x, g, b, alpha):
    m = x.shape[0]
    vec = pl.BlockSpec((1, D_MODEL), lambda i: (0, 0))
    return pl.pallas_call(
        functools.partial(_ffn_down_kernel, alpha),
        out_shape=jax.ShapeDtypeStruct((m, D_MODEL), F32),
        grid=(m // DOWN_TM,),
        in_specs=[
            pl.BlockSpec((DOWN_TM, D_FF), lambda i: (i, 0)),
            _resident((D_FF, D_MODEL), lambda i: (0, 0)),
            pl.BlockSpec((DOWN_TM, D_MODEL), lambda i: (i, 0)),
            vec, vec,
        ],
        out_specs=pl.BlockSpec((DOWN_TM, D_MODEL), lambda i: (i, 0)),
        compiler_params=_params("parallel"),
        name="ffn_down_ln",
    )(act, w_down_bf, x, g, b)


def kernel(x, w_in, attn_sinks, pool_w, pool_scale, ssm_lam_re, ssm_lam_im, ssm_log_dt, ssm_b_re, ssm_b_im,
           ssm_c_re, ssm_c_im, ssm_d, ssm_glu_w, w_out, ln1_g, ln1_b, ffn_w_up, ffn_conv_w, ffn_conv_b,
           ffn_w_down, ln2_g, ln2_b):
    bsz, s_len, _ = x.shape
    assert s_len == SEQ and x.shape[2] == D_MODEL
    depth = w_in.shape[0]
    alpha = (2 * depth) ** 0.25
    m = bsz * s_len
    cos_t, sa_t, sb_t = _rope_tables()
    xf = x.reshape(m, D_MODEL).astype(F32)

    w_out2d = w_out.astype(F32).reshape(depth * D_MODEL, D_MODEL)
    w_up2d = ffn_w_up.astype(F32).reshape(depth * D_MODEL, 2 * D_FF)
    w_down4d = ffn_w_down.astype(F32).reshape(depth, DOWN_BLOCKS, DOWN_BLOCK_ROWS, D_MODEL)
    conv8 = jnp.concatenate([ffn_conv_w.astype(F32), ffn_conv_b.astype(F32)[:, None, :],
                             jnp.zeros((depth, CONV_ROWS - 4, 2 * D_FF), F32)], axis=1)
    conv8 = conv8.reshape(depth * CONV_ROWS, 2 * D_FF)
    w_in2d = w_in.astype(F32).reshape(depth * D_MODEL, 2 * HALF_WIDTH)
    glu_w2d = ssm_glu_w.astype(F32).reshape(depth * SSM_WIDTH, 2 * SSM_WIDTH)
    pool_w3d = pool_w.astype(F32).reshape(depth * len(POOL_WINDOWS), POOL_GROUP, POOL_GROUP)
    vec = lambda a: a.astype(F32).reshape(1, -1)
    ssm_ops = _ssm_operators(ssm_lam_re, ssm_lam_im, ssm_log_dt, ssm_b_re, ssm_b_im, ssm_c_re, ssm_c_im, ssm_d)

    for l in range(depth):
        q, k2, v2, pu, su, w_out_bf = _in_proj(xf, w_in2d, w_out2d, l, cos_t, sa_t, sb_t)
        y_attn = _attention(q, k2, v2, attn_sinks[l].astype(F32))
        y_pool = _pool(pu, pool_w3d, l, vec(pool_scale[l]))

        y8 = _ssm(su, l, *ssm_ops)

        x1, x1_bf = _out_proj(y_attn, y_pool, y8, glu_w2d, l, w_out_bf, xf,
                              vec(ln1_g[l]), vec(ln1_b[l]), alpha)
        act, w_down_bf = _ffn_up(x1_bf, w_up2d, l, conv8, w_down4d)
        xf = _ffn_down(act, w_down_bf.reshape(D_FF, D_MODEL), x1, vec(ln2_g[l]), vec(ln2_b[l]), alpha)

    return xf.reshape(bsz, s_len, D_MODEL).astype(x.dtype)
```

```python
import functools

import jax
import numpy as np
import jax.numpy as jnp
from jax import lax
from jax.experimental import pallas as pl
from jax.experimental.pallas import tpu as pltpu

F32 = jnp.float32
BF16 = jnp.bfloat16

D_MODEL = 2048
SEQ = 2048
HEAD_DIM = 64
N_Q_HEADS = 16
N_KV_HEADS = 4
ATTN_WIDTH = N_Q_HEADS * HEAD_DIM
KV_WIDTH = N_KV_HEADS * HEAD_DIM
ATTN_BLOCK = 128
ROPE_THETA = 10000.0
POOL_WINDOWS = (2, 4, 8, 16)
POOL_GROUP = 128
POOL_WIDTH = 512
SSM_WIDTH = 512
SSM_GROUP = 16
SSM_N_GROUPS = 32
SSM_STATE = 64
SSM_CHUNK = 16
LN_EPS = 1e-5
D_FF = 5504

LANES = 128
SSM_BUNDLE = LANES // SSM_GROUP
SSM_N_BUNDLES = SSM_N_GROUPS // SSM_BUNDLE
SSM_BW = SSM_CHUNK * LANES
SSM_SW = SSM_BUNDLE * 2 * SSM_STATE
HALF_WIDTH = ATTN_WIDTH + KV_WIDTH
KV2_WIDTH = 2 * KV_WIDTH
VMEM_LIMIT = 56 * 1024 * 1024


def _params(*sem):
    return pltpu.CompilerParams(dimension_semantics=sem, vmem_limit_bytes=VMEM_LIMIT)


def _resident(block, index_map):
    return pl.BlockSpec(block, index_map, pipeline_mode=pl.Buffered(1))


IN_TM = 512
CAST_ROWS = 256


def _cast_weight_once(w_ref, wbf_ref):
    @pl.when(pl.program_id(0) == 0)
    def _():
        for r in range(0, w_ref.shape[0], CAST_ROWS):
            wbf_ref[r:r + CAST_ROWS, :] = w_ref[r:r + CAST_ROWS, :].astype(BF16)


def _dup_heads(pair):
    lo = lax.broadcasted_iota(jnp.int32, pair.shape, 1) < HEAD_DIM
    swapped = pltpu.roll(pair, HEAD_DIM, 1)
    return jnp.where(lo, pair, swapped), jnp.where(lo, swapped, pair)


def _inproj_kernel(x_ref, wa_ref, wb_ref, cos_ref, sa_ref, sb_ref, wo_ref, q_ref, k_ref, v_ref, p_ref, s_ref,
                   wobf_ref, wabf_ref, wbbf_ref, tok_ref):
    _cast_weight_once(wa_ref, wabf_ref)
    _cast_weight_once(wb_ref, wbbf_ref)
    wobf_ref[...] = wo_ref[...].astype(BF16)
    xb = x_ref[...].astype(BF16)
    acc = jnp.dot(xb, wabf_ref[...], preferred_element_type=F32)
    cos, sa, sb = cos_ref[...], sa_ref[...], sb_ref[...]
    nq = ATTN_WIDTH // LANES
    for c in range(HALF_WIDTH // LANES):
        a = acc[:, c * LANES:(c + 1) * LANES]
        r = a * cos + pltpu.roll(a, LANES - 32, 1) * sa + pltpu.roll(a, 32, 1) * sb
        if c < nq:
            q_ref[:, c * LANES:(c + 1) * LANES] = (r * (HEAD_DIM ** -0.5)).astype(BF16)
        else:
            ka, kb = _dup_heads(r)
            g = 2 * (c - nq)
            k_ref[:, g * LANES:(g + 1) * LANES] = ka.astype(BF16)
            k_ref[:, (g + 1) * LANES:(g + 2) * LANES] = kb.astype(BF16)

    acc = jnp.dot(xb, wbbf_ref[...], preferred_element_type=F32)
    for c in range(KV_WIDTH // LANES):
        va, vb = _dup_heads(acc[:, c * LANES:(c + 1) * LANES])
        v_ref[:, 2 * c * LANES:(2 * c + 1) * LANES] = va.astype(BF16)
        v_ref[:, (2 * c + 1) * LANES:(2 * c + 2) * LANES] = vb.astype(BF16)
    p_ref[...] = acc[:, KV_WIDTH:KV_WIDTH + POOL_WIDTH].astype(BF16)
    for b in range(SSM_N_BUNDLES):
        c0 = KV_WIDTH + POOL_WIDTH + b * LANES
        tok_ref[b] = acc[:, c0:c0 + LANES]
        for j in range(SSM_CHUNK):
            rows = tok_ref[b, pl.ds(j, IN_TM // SSM_CHUNK, stride=SSM_CHUNK), :]
            s_ref[b, :, j * LANES:(j + 1) * LANES] = rows.astype(BF16)


def _in_proj(x, w_in2d, w_out2d, layer, cos, sa, sb):
    m = x.shape[0]
    nseq = SEQ // IN_TM
    x_spec = pl.BlockSpec((IN_TM, D_MODEL), lambda i: (i, 0))
    tab = pl.BlockSpec((IN_TM, LANES), lambda i: (i % nseq, 0))
    row = lambda width: pl.BlockSpec((IN_TM, width), lambda i: (i, 0))
    steps = m // IN_TM
    wo_rows = D_MODEL // steps
    return pl.pallas_call(
        _inproj_kernel,
        out_shape=(jax.ShapeDtypeStruct((m, ATTN_WIDTH), BF16), jax.ShapeDtypeStruct((m, KV2_WIDTH), BF16),
                   jax.ShapeDtypeStruct((m, KV2_WIDTH), BF16), jax.ShapeDtypeStruct((m, POOL_WIDTH), BF16),
                   jax.ShapeDtypeStruct((SSM_N_BUNDLES, m // SSM_CHUNK, SSM_BW), BF16),
                   jax.ShapeDtypeStruct((D_MODEL, D_MODEL), BF16)),
        grid=(steps,),
        in_specs=[x_spec, _resident((D_MODEL, HALF_WIDTH), lambda i: (layer, 0)),
                  _resident((D_MODEL, HALF_WIDTH), lambda i: (layer, 1)), tab, tab, tab,
                  pl.BlockSpec((wo_rows, D_MODEL), lambda i: (layer * steps + i, 0))],
        out_specs=(row(ATTN_WIDTH), row(KV2_WIDTH), row(KV2_WIDTH), row(POOL_WIDTH),
                   pl.BlockSpec((SSM_N_BUNDLES, IN_TM // SSM_CHUNK, SSM_BW), lambda i: (0, i, 0)),
                   pl.BlockSpec((wo_rows, D_MODEL), lambda i: (i, 0))),
        scratch_shapes=[pltpu.VMEM((D_MODEL, HALF_WIDTH), BF16), pltpu.VMEM((D_MODEL, HALF_WIDTH), BF16),
                        pltpu.VMEM((SSM_N_BUNDLES, IN_TM, LANES), F32)],
        compiler_params=_params("arbitrary"),
        name="in_proj",
    )(x, w_in2d, w_in2d, cos, sa, sb, w_out2d)


def _rope_tables():
    half = HEAD_DIM // 2
    inv = ROPE_THETA ** (-jnp.arange(half, dtype=F32) / half)
    ang = jnp.arange(SEQ, dtype=F32)[:, None] * inv[None, :]
    cos, sin = jnp.cos(ang), jnp.sin(ang)
    zero = jnp.zeros_like(sin)
    reps = LANES // HEAD_DIM
    cos_t = jnp.tile(jnp.concatenate([cos, cos], -1), (1, reps))
    sa_t = jnp.tile(jnp.concatenate([-sin, zero], -1), (1, reps))
    sb_t = jnp.tile(jnp.concatenate([zero, sin], -1), (1, reps))
    return cos_t, sa_t, sb_t


ATTN_TQ = 512
ATTN_QB = ATTN_TQ // ATTN_BLOCK
KEYS = 2 * ATTN_BLOCK


def _attn_kernel(sink_ref, q_ref, kc_ref, vc_ref, kp_ref, vp_ref, o_ref, kbuf, vbuf):
    kbuf[0:ATTN_BLOCK, :] = kp_ref[...]
    kbuf[ATTN_BLOCK:, :] = kc_ref[...]
    vbuf[0:ATTN_BLOCK, :] = vp_ref[...]
    vbuf[ATTN_BLOCK:, :] = vc_ref[...]
    seq_start = pl.program_id(1) == 0

    row = lax.broadcasted_iota(jnp.int32, (ATTN_BLOCK, KEYS), 0)
    col = lax.broadcasted_iota(jnp.int32, (ATTN_BLOCK, KEYS), 1)
    dist = row + ATTN_BLOCK - col
    band = (dist >= 0) & (dist < ATTN_BLOCK)
    band_first = band & (jnp.logical_not(seq_start) | (col >= ATTN_BLOCK))
    lo_kv = lax.broadcasted_iota(jnp.int32, (KEYS, LANES), 1) < HEAD_DIM
    lo_out = lax.broadcasted_iota(jnp.int32, (ATTN_BLOCK, LANES), 1) < HEAD_DIM
    zero_kv = jnp.zeros((KEYS, LANES), BF16)

    def split_heads(x):
        return jnp.concatenate([jnp.where(lo_kv, x, zero_kv), jnp.where(lo_kv, zero_kv, x)], axis=0)

    for qb in range(ATTN_QB):
        valid = band_first if qb == 0 else band
        r0 = qb * ATTN_BLOCK
        for g in range(N_KV_HEADS):
            k2 = split_heads(kbuf[r0:r0 + KEYS, g * LANES:(g + 1) * LANES])
            v2 = split_heads(vbuf[r0:r0 + KEYS, g * LANES:(g + 1) * LANES])
            c0 = 2 * g * LANES
            qq = jnp.concatenate([q_ref[r0:r0 + ATTN_BLOCK, c0:c0 + LANES],
                                  q_ref[r0:r0 + ATTN_BLOCK, c0 + LANES:c0 + 2 * LANES]], axis=0)
            s4 = lax.dot_general(qq, k2, (((1,), (1,)), ((), ())), preferred_element_type=F32)
            p_rows, dens = [], []
            for pr in range(2):
                ps = []
                for hh in range(2):
                    sink = sink_ref[4 * g + 2 * pr + hh]
                    s = jnp.where(valid, s4[pr * ATTN_BLOCK:(pr + 1) * ATTN_BLOCK, hh * KEYS:(hh + 1) * KEYS], -1e30)
                    mx = jnp.maximum(s.max(-1, keepdims=True), sink)
                    p = jnp.exp(s - mx)
                    dens.append(p.sum(-1, keepdims=True) + jnp.exp(sink - mx))
                    ps.append(p.astype(BF16))
                p_rows.append(jnp.concatenate(ps, axis=1))
            o4 = jnp.dot(jnp.concatenate(p_rows, axis=0), v2, preferred_element_type=F32)
            for pr in range(2):
                o = o4[pr * ATTN_BLOCK:(pr + 1) * ATTN_BLOCK] / jnp.where(lo_out, dens[2 * pr], dens[2 * pr + 1])
                o_ref[r0:r0 + ATTN_BLOCK, c0 + pr * LANES:c0 + (pr + 1) * LANES] = o.astype(BF16)


def _attention(q, k2, v2, sinks):
    m = q.shape[0]
    nq = SEQ // ATTN_TQ
    cur = lambda b, i: (b * nq + i, 0)
    prev = lambda b, i: (jnp.maximum((b * nq + i) * ATTN_QB - 1, 0), 0)
    return pl.pallas_call(
        _attn_kernel,
        out_shape=jax.ShapeDtypeStruct((m, ATTN_WIDTH), BF16),
        grid=(m // SEQ, nq),
        in_specs=[
            pl.BlockSpec(memory_space=pltpu.SMEM),
            pl.BlockSpec((ATTN_TQ, ATTN_WIDTH), cur),
            pl.BlockSpec((ATTN_TQ, KV2_WIDTH), cur),
            pl.BlockSpec((ATTN_TQ, KV2_WIDTH), cur),
            pl.BlockSpec((ATTN_BLOCK, KV2_WIDTH), prev),
            pl.BlockSpec((ATTN_BLOCK, KV2_WIDTH), prev),
        ],
        out_specs=pl.BlockSpec((ATTN_TQ, ATTN_WIDTH), cur),
        scratch_shapes=[pltpu.VMEM((ATTN_TQ + ATTN_BLOCK, KV2_WIDTH), BF16),
                        pltpu.VMEM((ATTN_TQ + ATTN_BLOCK, KV2_WIDTH), BF16)],
        compiler_params=_params("parallel", "arbitrary"),
        name="swa_attention",
    )(sinks, q, k2, v2, k2, v2)


def _shift_rows(x, d, rows):
    return jnp.where(rows >= d, pltpu.roll(x, d, 0), 0.0)


def _pool_kernel(u_ref, w_ref, scale_ref, o_ref):
    gi = pl.program_id(1)
    u = u_ref[...].astype(F32)
    rows = lax.broadcasted_iota(jnp.int32, u.shape, 0)
    t1 = lax.broadcasted_iota(jnp.int32, (SEQ, 1), 0).astype(F32) + 1.0
    for idx, w in enumerate(POOL_WINDOWS):
        @pl.when(gi == idx)
        def _(w=w):
            s, d = u, 1
            while d < w:
                s = s + _shift_rows(s, d, rows)
                d *= 2
            mean = s / jnp.minimum(t1, float(w))
            y = jnp.dot((mean - u).astype(BF16), w_ref[0].astype(BF16), preferred_element_type=F32)
            o_ref[...] = (y * scale_ref[...]).astype(BF16)


def _pool(pu, pool_w3d, layer, pool_scale_l):
    m = pu.shape[0]
    return pl.pallas_call(
        _pool_kernel,
        out_shape=jax.ShapeDtypeStruct((m, POOL_WIDTH), BF16),
        grid=(m // SEQ, len(POOL_WINDOWS)),
        in_specs=[
            pl.BlockSpec((SEQ, POOL_GROUP), lambda b, g: (b, g)),
            pl.BlockSpec((1, POOL_GROUP, POOL_GROUP), lambda b, g: (layer * len(POOL_WINDOWS) + g, 0, 0)),
            pl.BlockSpec((1, POOL_GROUP), lambda b, g: (0, g)),
        ],
        out_specs=pl.BlockSpec((SEQ, POOL_GROUP), lambda b, g: (b, g)),
        compiler_params=_params("parallel", "arbitrary"),
        name="multiscale_pool",
    )(pu, pool_w3d, pool_scale_l)


SSM_SCAN_STEPS = 7
SSM_COLS = 2 * LANES


def _ssm_kernel(u_ref, a_ref, eg_ref, cg_ref, x_ref, ar_ref, ai_ref, o_ref, km_ref, em_ref, cm_ref):
    T, GB, H = SSM_CHUNK, SSM_BUNDLE, SSM_GROUP
    sh = H.bit_length() - 1

    @pl.when(pl.program_id(0) == 0)
    def _():
        em_ref[...] = jnp.zeros_like(em_ref)

    row_g = lax.broadcasted_iota(jnp.int32, (LANES, LANES), 0) >> sh
    col_g = lax.broadcasted_iota(jnp.int32, (LANES, LANES), 1) >> sh
    zero_blk = jnp.zeros((LANES, LANES), BF16)
    lag_blk = [jnp.where(row_g == col_g, jnp.concatenate([a_ref[0, t]] * GB, axis=0), zero_blk) for t in range(T)]
    for i in range(T):
        for j in range(min(i | 1, T - 1) + 1):
            km_ref[j * LANES:(j + 1) * LANES, i * LANES:(i + 1) * LANES] = lag_blk[i - j] if j <= i else zero_blk

    for j in range(T):
        for g in range(GB):
            r0 = j * LANES + g * H
            em_ref[r0:r0 + H, g * LANES:(g + 1) * LANES] = eg_ref[0, g, j * H:(j + 1) * H, :]

    lane_g = (lax.broadcasted_iota(jnp.int32, (LANES, SSM_BW), 1) & (LANES - 1)) >> sh
    for g in range(GB):
        spread = jnp.dot(cg_ref[0, g], x_ref[...], preferred_element_type=F32)
        cm_ref[g * LANES:(g + 1) * LANES, :] = jnp.where(lane_g == g, spread, 0.0).astype(BF16)

    u = u_ref[0]
    e = jnp.dot(u, em_ref[...], preferred_element_type=F32)
    nchunk = SEQ // SSM_CHUNK
    rows = lax.broadcasted_iota(jnp.int32, (e.shape[0], LANES), 0) % nchunk
    zprev = []
    for g in range(SSM_BUNDLE):
        eg = e[:, g * LANES:(g + 1) * LANES]
        for k in range(SSM_SCAN_STEPS):
            s = _shift_rows(eg, 1 << k, rows)
            sl = slice(g * LANES, (g + 1) * LANES)
            eg = eg + ar_ref[0, k:k + 1, sl] * s + ai_ref[0, k:k + 1, sl] * pltpu.roll(s, SSM_STATE, 1)
        zprev.append(_shift_rows(eg, 1, rows).astype(BF16))
    zp = jnp.concatenate(zprev, axis=1)

    for mblk in range(SSM_BW // SSM_COLS):
        c0, kdim = mblk * SSM_COLS, (mblk + 1) * SSM_COLS
        y = jnp.dot(u[:, :kdim], km_ref[0:kdim, c0:c0 + SSM_COLS], preferred_element_type=F32)
        y = y + jnp.dot(zp, cm_ref[:, c0:c0 + SSM_COLS], preferred_element_type=F32)
        o_ref[0, :, c0:c0 + SSM_COLS] = jax.nn.gelu(y).astype(BF16)


def _ssm(u8, layer, a_lag, eg, cg, spread, ar, ai):
    nb, rows, _ = u8.shape
    T, GB, H, Q = SSM_CHUNK, SSM_BUNDLE, SSM_GROUP, 2 * SSM_STATE
    par = lambda *dims: pl.BlockSpec((1,) + dims, lambda i: (layer * nb + i,) + (0,) * len(dims))
    act = pl.BlockSpec((1, rows, SSM_BW), lambda i: (i, 0, 0))
    return pl.pallas_call(
        _ssm_kernel,
        out_shape=jax.ShapeDtypeStruct((nb, rows, SSM_BW), BF16),
        grid=(nb,),
        in_specs=[act, par(T, H, LANES), par(GB, T * H, Q), par(GB, Q, T * H),
                  pl.BlockSpec((T * H, SSM_BW), lambda i: (0, 0)), par(8, SSM_SW), par(8, SSM_SW)],
        out_specs=act,
        scratch_shapes=[pltpu.VMEM((SSM_BW, SSM_BW), BF16), pltpu.VMEM((SSM_BW, SSM_SW), BF16),
                        pltpu.VMEM((SSM_SW, SSM_BW), BF16)],
        compiler_params=_params("arbitrary"),
        name="s5_chunked_ssm",
    )(u8, a_lag, eg, cg, spread, ar, ai)


def _cmul(xr, xi, yr, yi):
    return xr * yr - xi * yi, xr * yi + xi * yr


def _ssm_group_kernel(cr_ref, ci_ref, br_ref, bi_ref, pr_ref, pi_ref, qr_ref, qi_ref, k_ref, e_ref, c_ref):
    T, H, P = SSM_CHUNK, SSM_GROUP, SSM_STATE
    for g in range(cr_ref.shape[0]):
        cr, ci, br, bi = cr_ref[g], ci_ref[g], br_ref[g], bi_ref[g]
        pr, pi = pr_ref[g][:, None, :], pi_ref[g][:, None, :]
        car = cr[None] * pr - ci[None] * pi
        cai = cr[None] * pi + ci[None] * pr
        ca = jnp.concatenate([car[:T].reshape(T * H, P), cai[:T].reshape(T * H, P)], axis=1)
        bb = jnp.concatenate([br, -bi], axis=1)
        k_ref[g] = lax.dot_general(bb, ca, (((1,), (1,)), ((), ())),
                                   precision=lax.Precision.HIGHEST, preferred_element_type=F32)
        qr, qi = qr_ref[g][:, None, :], qi_ref[g][:, None, :]
        er = (qr * br[None] - qi * bi[None]).reshape(T * H, P)
        ei = (qr * bi[None] + qi * br[None]).reshape(T * H, P)
        e_ref[g] = jnp.concatenate([er, ei], axis=1).astype(BF16)
        c_ref[g] = jnp.concatenate([car[1:].reshape(T * H, P).T, -cai[1:].reshape(T * H, P).T], axis=0).astype(BF16)


def _ssm_group_operators(cr, ci, brt, bit, pr, pi, qr, qi):
    n = cr.shape[0]
    T, H, P = SSM_CHUNK, SSM_GROUP, SSM_STATE
    spec = lambda a, b: pl.BlockSpec((SSM_BUNDLE, a, b), lambda i: (i, 0, 0))
    return pl.pallas_call(
        _ssm_group_kernel,
        out_shape=(jax.ShapeDtypeStruct((n, H, T * H), F32), jax.ShapeDtypeStruct((n, T * H, 2 * P), BF16),
                   jax.ShapeDtypeStruct((n, 2 * P, T * H), BF16)),
        grid=(n // SSM_BUNDLE,),
        in_specs=[spec(H, P)] * 4 + [spec(T + 1, P)] * 2 + [spec(T, P)] * 2,
        out_specs=(spec(H, T * H), spec(T * H, 2 * P), spec(2 * P, T * H)),
        compiler_params=_params("parallel"),
        name="ssm_group_operators",
    )(cr, ci, brt, bit, pr, pi, qr, qi)


def _ssm_operators(lam_re, lam_im, log_dt, b_re, b_im, c_re, c_im, d):
    lr, li = lam_re.astype(F32), lam_im.astype(F32)
    dt = jnp.exp(log_dt.astype(F32))[..., None]
    mag = jnp.exp(lr * dt)
    ar, ai = mag * jnp.cos(li * dt), mag * jnp.sin(li * dt)
    nr, ni = ar - 1.0, ai
    den = lr * lr + li * li
    zr = (nr * lr + ni * li) / den
    zi = (ni * lr - nr * li) / den
    br, bi = b_re.astype(F32), b_im.astype(F32)
    bbr = zr[..., None] * br - zi[..., None] * bi
    bbi = zr[..., None] * bi + zi[..., None] * br
    cr, ci = c_re.astype(F32), c_im.astype(F32)

    pr, pi = [jnp.ones_like(ar)], [jnp.zeros_like(ar)]
    for _ in range(SSM_CHUNK):
        nr_, ni_ = _cmul(pr[-1], pi[-1], ar, ai)
        pr.append(nr_)
        pi.append(ni_)
    prr, pir = jnp.stack(pr[SSM_CHUNK - 1::-1]), jnp.stack(pi[SSM_CHUNK - 1::-1])
    pr, pi = jnp.stack(pr), jnp.stack(pi)

    T, G, H, P = SSM_CHUNK, SSM_N_GROUPS, SSM_GROUP, SSM_STATE
    NB, GB = SSM_N_BUNDLES, SSM_BUNDLE
    L = lr.shape[0]
    per_group = lambda t: jnp.moveaxis(t, 0, 2).reshape(L * G, t.shape[0], P)
    kall, eg, cg = _ssm_group_operators(
        cr.reshape(L * G, H, P), ci.reshape(L * G, H, P),
        bbr.transpose(0, 1, 3, 2).reshape(L * G, H, P), bbi.transpose(0, 1, 3, 2).reshape(L * G, H, P),
        per_group(pr), per_group(pi), per_group(prr), per_group(pir))
    kall = kall.at[:, :, 0:H].add(d.astype(F32).reshape(L * G, 1, H) * jnp.eye(H, dtype=F32))
    a_lag = kall.reshape(L, NB, GB, H, T, H).transpose(0, 1, 4, 3, 2, 5).reshape(L * NB, T, H, GB * H)
    eg = eg.reshape(L * NB, GB, T * H, 2 * P)
    cg = cg.reshape(L * NB, GB, 2 * P, T * H)

    sr, si = [pr[T]], [pi[T]]
    for _ in range(SSM_SCAN_STEPS - 1):
        nr_, ni_ = _cmul(sr[-1], si[-1], sr[-1], si[-1])
        sr.append(nr_)
        si.append(ni_)
    sr.append(jnp.zeros_like(ar))
    si.append(jnp.zeros_like(ar))
    sr, si = jnp.stack(sr, 2), jnp.stack(si, 2)
    lay = lambda t: t.reshape(L, NB, GB, 8, 2 * P).transpose(0, 1, 3, 2, 4).reshape(L * NB, 8, GB * 2 * P)
    ar_t = lay(jnp.concatenate([sr, sr], -1))
    ai_t = lay(jnp.concatenate([-si, si], -1))

    spread = np.zeros((T, H, T, GB, H), np.float32)
    for i in range(T):
        for h in range(H):
            spread[i, h, i, :, h] = 1.0
    spread = jnp.asarray(spread.reshape(T * H, T * GB * H), BF16)
    return a_lag.astype(BF16), eg, cg, spread, ar_t, ai_t


def _deepnorm_ln(x, branch, g, b, alpha):
    y = alpha * x + branch
    mu = y.mean(-1, keepdims=True)
    yc = y - mu
    var = jnp.square(yc).mean(-1, keepdims=True)
    return yc * lax.rsqrt(var + LN_EPS) * g + b


OUT_TM = 512
OUT_SUB = 256


def _outproj_kernel(alpha, ya_ref, yp_ref, ys_ref, glu_ref, w_ref, x_ref, g_ref, b_ref, o_ref, ob_ref, tok_ref):
    for b in range(SSM_N_BUNDLES):
        for j in range(SSM_CHUNK):
            rows = ys_ref[b, :, j * LANES:(j + 1) * LANES].astype(F32)
            tok_ref[b, pl.ds(j, OUT_TM // SSM_CHUNK, stride=SSM_CHUNK), :] = rows
    ys = jnp.concatenate([tok_ref[b].astype(BF16) for b in range(SSM_N_BUNDLES)], axis=1)
    ab = jnp.dot(ys, glu_ref[...].astype(BF16), preferred_element_type=F32)
    y_ssm = (ab[:, :SSM_WIDTH] * jax.nn.sigmoid(ab[:, SSM_WIDTH:])).astype(BF16)
    o_p, o_s = ATTN_WIDTH, ATTN_WIDTH + POOL_WIDTH
    for r in range(0, OUT_TM, OUT_SUB):
        rs = slice(r, r + OUT_SUB)
        acc = jnp.dot(ya_ref[rs, :], w_ref[0:o_p, :], preferred_element_type=F32)
        acc += jnp.dot(yp_ref[rs, :], w_ref[o_p:o_s, :], preferred_element_type=F32)
        acc += jnp.dot(y_ssm[rs, :], w_ref[o_s:, :], preferred_element_type=F32)
        x1 = _deepnorm_ln(x_ref[rs, :], acc, g_ref[...], b_ref[...], alpha)
        o_ref[rs, :] = x1
        ob_ref[rs, :] = x1.astype(BF16)


def _out_proj(ya, yp, ys8, glu_w2d, layer, w_out_bf, x, g, b, alpha):
    m = x.shape[0]
    row = lambda width: pl.BlockSpec((OUT_TM, width), lambda i: (i, 0))
    vec = pl.BlockSpec((1, D_MODEL), lambda i: (0, 0))
    return pl.pallas_call(
        functools.partial(_outproj_kernel, alpha),
        out_shape=(jax.ShapeDtypeStruct((m, D_MODEL), F32), jax.ShapeDtypeStruct((m, D_MODEL), BF16)),
        grid=(m // OUT_TM,),
        in_specs=[
            row(ATTN_WIDTH), row(POOL_WIDTH),
            pl.BlockSpec((SSM_N_BUNDLES, OUT_TM // SSM_CHUNK, SSM_BW), lambda i: (0, i, 0)),
            pl.BlockSpec((SSM_WIDTH, 2 * SSM_WIDTH), lambda i: (layer, 0)),
            _resident((D_MODEL, D_MODEL), lambda i: (0, 0)),
            row(D_MODEL), vec, vec,
        ],
        out_specs=(row(D_MODEL), row(D_MODEL)),
        scratch_shapes=[pltpu.VMEM((SSM_N_BUNDLES, OUT_TM, LANES), F32)],
        compiler_params=_params("parallel"),
        name="out_proj_ln",
    )(ya, yp, ys8, glu_w2d, w_out_bf, x, g, b)


UP_TM = 1024
UP_TN = 512
UP_NJ = -(-D_FF // UP_TN)
FF_PAD = UP_NJ * UP_TN
UP_SHIFT = FF_PAD - D_FF
HALO = 8
CONV_ROWS = 8
DOWN_BLOCK_ROWS = 128
DOWN_BLOCKS = D_FF // DOWN_BLOCK_ROWS


def _ffn_up_kernel(x_ref, wv_ref, wg_ref, cv_ref, cg_ref, wd_ref, o_ref, wdb_ref, wvb_ref, wgb_ref, hv_ref, hg_ref):
    j, i = pl.program_id(0), pl.program_id(1)

    @pl.when(j * pl.num_programs(1) + i < DOWN_BLOCKS)
    def _():
        wdb_ref[...] = wd_ref[...].astype(BF16)

    @pl.when(i == 0)
    def _():
        for r in range(0, D_MODEL, CAST_ROWS):
            wvb_ref[r:r + CAST_ROWS, :] = wv_ref[r:r + CAST_ROWS, :].astype(BF16)
            wgb_ref[r:r + CAST_ROWS, :] = wg_ref[r:r + CAST_ROWS, :].astype(BF16)
        hv_ref[0:HALO, :] = jnp.zeros((HALO, UP_TN), F32)
        hg_ref[0:HALO, :] = jnp.zeros((HALO, UP_TN), F32)

    @pl.when(i != 0)
    def _():
        hv_ref[0:HALO, :] = hv_ref[UP_TM:, :]
        hg_ref[0:HALO, :] = hg_ref[UP_TM:, :]

    x = x_ref[...]
    seq_start = (i % (SEQ // UP_TM)) == 0
    rows = lax.broadcasted_iota(jnp.int32, (HALO, 1), 0)

    def conv(wb_ref, c_ref, h_ref):
        h_ref[HALO:, :] = jnp.dot(x, wb_ref[...], preferred_element_type=F32)

        def tap(shift):
            h = h_ref[HALO - shift:HALO - shift + UP_TM, :]
            head = jnp.where(seq_start & (rows < shift), 0.0, h[:HALO])
            return jnp.concatenate([head, h[HALO:]], axis=0)

        return c_ref[3:4, :] + tap(2) * c_ref[0:1, :] + tap(1) * c_ref[1:2, :] + h_ref[HALO:, :] * c_ref[2:3, :]

    gate = jax.nn.silu(conv(wgb_ref, cg_ref, hg_ref))
    act = (gate * conv(wvb_ref, cv_ref, hv_ref)).astype(BF16)
    last = pl.num_programs(0) - 1

    @pl.when(j != last)
    def _():
        o_ref[...] = act

    @pl.when(j == last)
    def _():
        o_ref[...] = jnp.concatenate([act[:, UP_SHIFT:], jnp.zeros((UP_TM, UP_SHIFT), BF16)], axis=1)


def _ffn_up(x_bf, w_up2d, layer, conv8, w_down4d):
    m = x_bf.shape[0]
    ni = m // UP_TM
    assert UP_NJ * ni >= DOWN_BLOCKS
    col = lambda c0, j: pl.multiple_of(c0 + jnp.minimum(j * UP_TN, D_FF - UP_TN), LANES)
    window = lambda rows, r0, c0: pl.BlockSpec((pl.Element(rows), pl.Element(UP_TN)),
                                               lambda j, i: (r0, col(c0, j)))
    dblk = lambda j, i: jnp.minimum(j * ni + i, DOWN_BLOCKS - 1)
    return pl.pallas_call(
        _ffn_up_kernel,
        out_shape=(jax.ShapeDtypeStruct((m, FF_PAD), BF16),
                   jax.ShapeDtypeStruct((DOWN_BLOCKS, DOWN_BLOCK_ROWS, D_MODEL), BF16)),
        grid=(UP_NJ, ni),
        in_specs=[
            pl.BlockSpec((UP_TM, D_MODEL), lambda j, i: (i, 0)),
            window(D_MODEL, layer * D_MODEL, 0), window(D_MODEL, layer * D_MODEL, D_FF),
            window(CONV_ROWS, layer * CONV_ROWS, 0), window(CONV_ROWS, layer * CONV_ROWS, D_FF),
            pl.BlockSpec((None, None, DOWN_BLOCK_ROWS, D_MODEL), lambda j, i: (layer, dblk(j, i), 0, 0)),
        ],
        out_specs=(pl.BlockSpec((UP_TM, UP_TN), lambda j, i: (i, j)),
                   pl.BlockSpec((None, DOWN_BLOCK_ROWS, D_MODEL), lambda j, i: (dblk(j, i), 0, 0))),
        scratch_shapes=[pltpu.VMEM((D_MODEL, UP_TN), BF16), pltpu.VMEM((D_MODEL, UP_TN), BF16),
                        pltpu.VMEM((HALO + UP_TM, UP_TN), F32), pltpu.VMEM((HALO + UP_TM, UP_TN), F32)],
        compiler_params=_params("arbitrary", "arbitrary"),
        name="ffn_up_conv_gate",
    )(x_bf, w_up2d, w_up2d, conv8, conv8, w_down4d)


DOWN_TM = 512
DOWN_SUB = 256


def _ffn_down_kernel(alpha, a_ref, w_ref, x_ref, g_ref, b_ref, o_ref):
    for r in range(0, DOWN_TM, DOWN_SUB):
        f = jnp.dot(a_ref[r:r + DOWN_SUB, :], w_ref[...], preferred_element_type=F32)
        o_ref[r:r + DOWN_SUB, :] = _deepnorm_ln(x_ref[r:r + DOWN_SUB, :], f, g_ref[...], b_ref[...], alpha)


def _ffn_down(act, w_down_bf, x, g, b, alpha):
    m = x.shape[0]
    vec = pl.BlockSpec((1, D_MODEL), lambda i: (0, 0))
    return pl.pallas_call(
        functools.partial(_ffn_down_kernel, alpha),
        out_shape=jax.ShapeDtypeStruct((m, D_MODEL), F32),
        grid=(m // DOWN_TM,),
        in_specs=[
            pl.BlockSpec((DOWN_TM, D_FF), lambda i: (i, 0)),
            _resident((D_FF, D_MODEL), lambda i: (0, 0)),
            pl.BlockSpec((DOWN_TM, D_MODEL), lambda i: (i, 0)),
            vec, vec,
        ],
        out_specs=pl.BlockSpec((DOWN_TM, D_MODEL), lambda i: (i, 0)),
        compiler_params=_params("parallel"),
        name="ffn_down_ln",
    )(act, w_down_bf, x, g, b)


def kernel(x, w_in, attn_sinks, pool_w, pool_scale, ssm_lam_re, ssm_lam_im, ssm_log_dt, ssm_b_re, ssm_b_im,
           ssm_c_re, ssm_c_im, ssm_d, ssm_glu_w, w_out, ln1_g, ln1_b, ffn_w_up, ffn_conv_w, ffn_conv_b,
           ffn_w_down, ln2_g, ln2_b):
    bsz, s_len, _ = x.shape
    assert s_len == SEQ and x.shape[2] == D_MODEL
    depth = w_in.shape[0]
    alpha = (2 * depth) ** 0.25
    m = bsz * s_len
    cos_t, sa_t, sb_t = _rope_tables()
    xf = x.reshape(m, D_MODEL).astype(F32)

    w_out2d = w_out.astype(F32).reshape(depth * D_MODEL, D_MODEL)
    w_up2d = ffn_w_up.astype(F32).reshape(depth * D_MODEL, 2 * D_FF)
    w_down4d = ffn_w_down.astype(F32).reshape(depth, DOWN_BLOCKS, DOWN_BLOCK_ROWS, D_MODEL)
    conv8 = jnp.concatenate([ffn_conv_w.astype(F32), ffn_conv_b.astype(F32)[:, None, :],
                             jnp.zeros((depth, CONV_ROWS - 4, 2 * D_FF), F32)], axis=1)
    conv8 = conv8.reshape(depth * CONV_ROWS, 2 * D_FF)
    w_in2d = w_in.astype(F32).reshape(depth * D_MODEL, 2 * HALF_WIDTH)
    glu_w2d = ssm_glu_w.astype(F32).reshape(depth * SSM_WIDTH, 2 * SSM_WIDTH)
    pool_w3d = pool_w.astype(F32).reshape(depth * len(POOL_WINDOWS), POOL_GROUP, POOL_GROUP)
    vec = lambda a: a.astype(F32).reshape(1, -1)
    ssm_ops = _ssm_operators(ssm_lam_re, ssm_lam_im, ssm_log_dt, ssm_b_re, ssm_b_im, ssm_c_re, ssm_c_im, ssm_d)

    for l in range(depth):
        q, k2, v2, pu, su, w_out_bf = _in_proj(xf, w_in2d, w_out2d, l, cos_t, sa_t, sb_t)
        y_attn = _attention(q, k2, v2, attn_sinks[l].astype(F32))
        y_pool = _pool(pu, pool_w3d, l, vec(pool_scale[l]))

        y8 = _ssm(su, l, *ssm_ops)

        x1, x1_bf = _out_proj(y_attn, y_pool, y8, glu_w2d, l, w_out_bf, xf,
                              vec(ln1_g[l]), vec(ln1_b[l]), alpha)
        act, w_down_bf = _ffn_up(x1_bf, w_up2d, l, conv8, w_down4d)
        xf = _ffn_down(act, w_down_bf.reshape(D_FF, D_MODEL), x1, vec(ln2_g[l]), vec(ln2_b[l]), alpha)

    return xf.reshape(bsz, s_len, D_MODEL).astype(x.dtype)
```

```python
import functools

import jax
import numpy as np
import jax.numpy as jnp
from jax import lax
from jax.experimental import pallas as pl
from jax.experimental.pallas import tpu as pltpu

F32 = jnp.float32
BF16 = jnp.bfloat16

D_MODEL = 2048
SEQ = 2048
HEAD_DIM = 64
N_Q_HEADS = 16
N_KV_HEADS = 4
ATTN_WIDTH = N_Q_HEADS * HEAD_DIM
KV_WIDTH = N_KV_HEADS * HEAD_DIM
ATTN_BLOCK = 128
ROPE_THETA = 10000.0
POOL_WINDOWS = (2, 4, 8, 16)
POOL_GROUP = 128
POOL_WIDTH = 512
SSM_WIDTH = 512
SSM_GROUP = 16
SSM_N_GROUPS = 32
SSM_STATE = 64
SSM_CHUNK = 16
LN_EPS = 1e-5
D_FF = 5504

LANES = 128
SSM_BUNDLE = LANES // SSM_GROUP
SSM_N_BUNDLES = SSM_N_GROUPS // SSM_BUNDLE
SSM_BW = SSM_CHUNK * LANES
SSM_SW = SSM_BUNDLE * 2 * SSM_STATE
HALF_WIDTH = ATTN_WIDTH + KV_WIDTH
KV2_WIDTH = 2 * KV_WIDTH
VMEM_LIMIT = 56 * 1024 * 1024


def _params(*sem):
    return pltpu.CompilerParams(dimension_semantics=sem, vmem_limit_bytes=VMEM_LIMIT)


def _resident(block, index_map):
    return pl.BlockSpec(block, index_map, pipeline_mode=pl.Buffered(1))


IN_TM = 512
CAST_ROWS = 256


def _cast_weight_once(w_ref, wbf_ref):
    @pl.when(pl.program_id(0) == 0)
    def _():
        for r in range(0, w_ref.shape[0], CAST_ROWS):
            wbf_ref[r:r + CAST_ROWS, :] = w_ref[r:r + CAST_ROWS, :].astype(BF16)


def _dup_heads(pair):
    lo = lax.broadcasted_iota(jnp.int32, pair.shape, 1) < HEAD_DIM
    swapped = pltpu.roll(pair, HEAD_DIM, 1)
    return jnp.where(lo, pair, swapped), jnp.where(lo, swapped, pair)


def _inproj_kernel(x_ref, wa_ref, wb_ref, cos_ref, sa_ref, sb_ref, wo_ref, q_ref, k_ref, v_ref, p_ref, s_ref,
                   wobf_ref, wabf_ref, wbbf_ref, tok_ref):
    _cast_weight_once(wa_ref, wabf_ref)
    _cast_weight_once(wb_ref, wbbf_ref)
    wobf_ref[...] = wo_ref[...].astype(BF16)
    xb = x_ref[...].astype(BF16)
    acc = jnp.dot(xb, wabf_ref[...], preferred_element_type=F32)
    cos, sa, sb = cos_ref[...], sa_ref[...], sb_ref[...]
    nq = ATTN_WIDTH // LANES
    for c in range(HALF_WIDTH // LANES):
        a = acc[:, c * LANES:(c + 1) * LANES]
        r = a * cos + pltpu.roll(a, LANES - 32, 1) * sa + pltpu.roll(a, 32, 1) * sb
        if c < nq:
            q_ref[:, c * LANES:(c + 1) * LANES] = (r * (HEAD_DIM ** -0.5)).astype(BF16)
        else:
            ka, kb = _dup_heads(r)
            g = 2 * (c - nq)
            k_ref[:, g * LANES:(g + 1) * LANES] = ka.astype(BF16)
            k_ref[:, (g + 1) * LANES:(g + 2) * LANES] = kb.astype(BF16)

    acc = jnp.dot(xb, wbbf_ref[...], preferred_element_type=F32)
    for c in range(KV_WIDTH // LANES):
        va, vb = _dup_heads(acc[:, c * LANES:(c + 1) * LANES])
        v_ref[:, 2 * c * LANES:(2 * c + 1) * LANES] = va.astype(BF16)
        v_ref[:, (2 * c + 1) * LANES:(2 * c + 2) * LANES] = vb.astype(BF16)
    p_ref[...] = acc[:, KV_WIDTH:KV_WIDTH + POOL_WIDTH].astype(BF16)
    for b in range(SSM_N_BUNDLES):
        c0 = KV_WIDTH + POOL_WIDTH + b * LANES
        tok_ref[b] = acc[:, c0:c0 + LANES]
        for j in range(SSM_CHUNK):
            rows = tok_ref[b, pl.ds(j, IN_TM // SSM_CHUNK, stride=SSM_CHUNK), :]
            s_ref[b, :, j * LANES:(j + 1) * LANES] = rows.astype(BF16)


def _in_proj(x, w_in2d, w_out2d, layer, cos, sa, sb):
    m = x.shape[0]
    nseq = SEQ // IN_TM
    x_spec = pl.BlockSpec((IN_TM, D_MODEL), lambda i: (i, 0))
    tab = pl.BlockSpec((IN_TM, LANES), lambda i: (i % nseq, 0))
    row = lambda width: pl.BlockSpec((IN_TM, width), lambda i: (i, 0))
    steps = m // IN_TM
    wo_rows = D_MODEL // steps
    return pl.pallas_call(
        _inproj_kernel,
        out_shape=(jax.ShapeDtypeStruct((m, ATTN_WIDTH), BF16), jax.ShapeDtypeStruct((m, KV2_WIDTH), BF16),
                   jax.ShapeDtypeStruct((m, KV2_WIDTH), BF16), jax.ShapeDtypeStruct((m, POOL_WIDTH), BF16),
                   jax.ShapeDtypeStruct((SSM_N_BUNDLES, m // SSM_CHUNK, SSM_BW), BF16),
                   jax.ShapeDtypeStruct((D_MODEL, D_MODEL), BF16)),
        grid=(steps,),
        in_specs=[x_spec, _resident((D_MODEL, HALF_WIDTH), lambda i: (layer, 0)),
                  _resident((D_MODEL, HALF_WIDTH), lambda i: (layer, 1)), tab, tab, tab,
                  pl.BlockSpec((wo_rows, D_MODEL), lambda i: (layer * steps + i, 0))],
        out_specs=(row(ATTN_WIDTH), row(KV2_WIDTH), row(KV2_WIDTH), row(POOL_WIDTH),
                   pl.BlockSpec((SSM_N_BUNDLES, IN_TM // SSM_CHUNK, SSM_BW), lambda i: (0, i, 0)),
                   pl.BlockSpec((wo_rows, D_MODEL), lambda i: (i, 0))),
        scratch_shapes=[pltpu.VMEM((D_MODEL, HALF_WIDTH), BF16), pltpu.VMEM((D_MODEL, HALF_WIDTH), BF16),
                        pltpu.VMEM((SSM_N_BUNDLES, IN_TM, LANES), F32)],
        compiler_params=_params("arbitrary"),
        name="in_proj",
    )(x, w_in2d, w_in2d, cos, sa, sb, w_out2d)


def _rope_tables():
    half = HEAD_DIM // 2
    inv = ROPE_THETA ** (-jnp.arange(half, dtype=F32) / half)
    ang = jnp.arange(SEQ, dtype=F32)[:, None] * inv[None, :]
    cos, sin = jnp.cos(ang), jnp.sin(ang)
    zero = jnp.zeros_like(sin)
    reps = LANES // HEAD_DIM
    cos_t = jnp.tile(jnp.concatenate([cos, cos], -1), (1, reps))
    sa_t = jnp.tile(jnp.concatenate([-sin, zero], -1), (1, reps))
    sb_t = jnp.tile(jnp.concatenate([zero, sin], -1), (1, reps))
    return cos_t, sa_t, sb_t


ATTN_TQ = 512
ATTN_QB = ATTN_TQ // ATTN_BLOCK
KEYS = 2 * ATTN_BLOCK


def _attn_kernel(sink_ref, q_ref, kc_ref, vc_ref, kp_ref, vp_ref, o_ref, kbuf, vbuf):
    kbuf[0:ATTN_BLOCK, :] = kp_ref[...]
    kbuf[ATTN_BLOCK:, :] = kc_ref[...]
    vbuf[0:ATTN_BLOCK, :] = vp_ref[...]
    vbuf[ATTN_BLOCK:, :] = vc_ref[...]
    seq_start = pl.program_id(1) == 0

    row = lax.broadcasted_iota(jnp.int32, (ATTN_BLOCK, KEYS), 0)
    col = lax.broadcasted_iota(jnp.int32, (ATTN_BLOCK, KEYS), 1)
    dist = row + ATTN_BLOCK - col
    band = (dist >= 0) & (dist < ATTN_BLOCK)
    band_first = band & (jnp.logical_not(seq_start) | (col >= ATTN_BLOCK))
    lo_kv = lax.broadcasted_iota(jnp.int32, (KEYS, LANES), 1) < HEAD_DIM
    lo_out = lax.broadcasted_iota(jnp.int32, (ATTN_BLOCK, LANES), 1) < HEAD_DIM
    zero_kv = jnp.zeros((KEYS, LANES), BF16)

    def split_heads(x):
        return jnp.concatenate([jnp.where(lo_kv, x, zero_kv), jnp.where(lo_kv, zero_kv, x)], axis=0)

    for qb in range(ATTN_QB):
        valid = band_first if qb == 0 else band
        r0 = qb * ATTN_BLOCK
        for g in range(N_KV_HEADS):
            k2 = split_heads(kbuf[r0:r0 + KEYS, g * LANES:(g + 1) * LANES])
            v2 = split_heads(vbuf[r0:r0 + KEYS, g * LANES:(g + 1) * LANES])
            c0 = 2 * g * LANES
            qq = jnp.concatenate([q_ref[r0:r0 + ATTN_BLOCK, c0:c0 + LANES],
                                  q_ref[r0:r0 + ATTN_BLOCK, c0 + LANES:c0 + 2 * LANES]], axis=0)
            s4 = lax.dot_general(qq, k2, (((1,), (1,)), ((), ())), preferred_element_type=F32)
            p_rows, dens = [], []
            for pr in range(2):
                ps = []
                for hh in range(2):
                    sink = sink_ref[4 * g + 2 * pr + hh]
                    s = jnp.where(valid, s4[pr * ATTN_BLOCK:(pr + 1) * ATTN_BLOCK, hh * KEYS:(hh + 1) * KEYS], -1e30)
                    mx = jnp.maximum(s.max(-1, keepdims=True), sink)
                    p = jnp.exp(s - mx)
                    dens.append(p.sum(-1, keepdims=True) + jnp.exp(sink - mx))
                    ps.append(p.astype(BF16))
                p_rows.append(jnp.concatenate(ps, axis=1))
            o4 = jnp.dot(jnp.concatenate(p_rows, axis=0), v2, preferred_element_type=F32)
            for pr in range(2):
                o = o4[pr * ATTN_BLOCK:(pr + 1) * ATTN_BLOCK] / jnp.where(lo_out, dens[2 * pr], dens[2 * pr + 1])
                o_ref[r0:r0 + ATTN_BLOCK, c0 + pr * LANES:c0 + (pr + 1) * LANES] = o.astype(BF16)


def _attention(q, k2, v2, sinks):
    m = q.shape[0]
    nq = SEQ // ATTN_TQ
    cur = lambda b, i: (b * nq + i, 0)
    prev = lambda b, i: (jnp.maximum((b * nq + i) * ATTN_QB - 1, 0), 0)
    return pl.pallas_call(
        _attn_kernel,
        out_shape=jax.ShapeDtypeStruct((m, ATTN_WIDTH), BF16),
        grid=(m // SEQ, nq),
        in_specs=[
            pl.BlockSpec(memory_space=pltpu.SMEM),
            pl.BlockSpec((ATTN_TQ, ATTN_WIDTH), cur),
            pl.BlockSpec((ATTN_TQ, KV2_WIDTH), cur),
            pl.BlockSpec((ATTN_TQ, KV2_WIDTH), cur),
            pl.BlockSpec((ATTN_BLOCK, KV2_WIDTH), prev),
            pl.BlockSpec((ATTN_BLOCK, KV2_WIDTH), prev),
        ],
        out_specs=pl.BlockSpec((ATTN_TQ, ATTN_WIDTH), cur),
        scratch_shapes=[pltpu.VMEM((ATTN_TQ + ATTN_BLOCK, KV2_WIDTH), BF16),
                        pltpu.VMEM((ATTN_TQ + ATTN_BLOCK, KV2_WIDTH), BF16)],
        compiler_params=_params("parallel", "arbitrary"),
        name="swa_attention",
    )(sinks, q, k2, v2, k2, v2)


def _shift_rows(x, d, rows):
    return jnp.where(rows >= d, pltpu.roll(x, d, 0), 0.0)


def _pool_kernel(u_ref, w_ref, scale_ref, o_ref):
    gi = pl.program_id(1)
    u = u_ref[...].astype(F32)
    rows = lax.broadcasted_iota(jnp.int32, u.shape, 0)
    t1 = lax.broadcasted_iota(jnp.int32, (SEQ, 1), 0).astype(F32) + 1.0
    for idx, w in enumerate(POOL_WINDOWS):
        @pl.when(gi == idx)
        def _(w=w):
            s, d = u, 1
            while d < w:
                s = s + _shift_rows(s, d, rows)
                d *= 2
            mean = s / jnp.minimum(t1, float(w))
            y = jnp.dot((mean - u).astype(BF16), w_ref[0].astype(BF16), preferred_element_type=F32)
            o_ref[...] = (y * scale_ref[...]).astype(BF16)


def _pool(pu, pool_w3d, layer, pool_scale_l):
    m = pu.shape[0]
    return pl.pallas_call(
        _pool_kernel,
        out_shape=jax.ShapeDtypeStruct((m, POOL_WIDTH), BF16),
        grid=(m // SEQ, len(POOL_WINDOWS)),
        in_specs=[
            pl.BlockSpec((SEQ, POOL_GROUP), lambda b, g: (b, g)),
            pl.BlockSpec((1, POOL_GROUP, POOL_GROUP), lambda b, g: (layer * len(POOL_WINDOWS) + g, 0, 0)),
            pl.BlockSpec((1, POOL_GROUP), lambda b, g: (0, g)),
        ],
        out_specs=pl.BlockSpec((SEQ, POOL_GROUP), lambda b, g: (b, g)),
        compiler_params=_params("parallel", "arbitrary"),
        name="multiscale_pool",
    )(pu, pool_w3d, pool_scale_l)


SSM_SCAN_STEPS = 7
SSM_COLS = 2 * LANES


def _ssm_kernel(u_ref, a_ref, eg_ref, cg_ref, x_ref, ar_ref, ai_ref, o_ref, km_ref, em_ref, cm_ref):
    T, GB, H = SSM_CHUNK, SSM_BUNDLE, SSM_GROUP
    sh = H.bit_length() - 1

    @pl.when(pl.program_id(0) == 0)
    def _():
        em_ref[...] = jnp.zeros_like(em_ref)

    row_g = lax.broadcasted_iota(jnp.int32, (LANES, LANES), 0) >> sh
    col_g = lax.broadcasted_iota(jnp.int32, (LANES, LANES), 1) >> sh
    zero_blk = jnp.zeros((LANES, LANES), BF16)
    lag_blk = [jnp.where(row_g == col_g, jnp.concatenate([a_ref[0, t]] * GB, axis=0), zero_blk) for t in range(T)]
    for i in range(T):
        for j in range(min(i | 1, T - 1) + 1):
            km_ref[j * LANES:(j + 1) * LANES, i * LANES:(i + 1) * LANES] = lag_blk[i - j] if j <= i else zero_blk

    for j in range(T):
        for g in range(GB):
            r0 = j * LANES + g * H
            em_ref[r0:r0 + H, g * LANES:(g + 1) * LANES] = eg_ref[0, g, j * H:(j + 1) * H, :]

    lane_g = (lax.broadcasted_iota(jnp.int32, (LANES, SSM_BW), 1) & (LANES - 1)) >> sh
    for g in range(GB):
        spread = jnp.dot(cg_ref[0, g], x_ref[...], preferred_element_type=F32)
        cm_ref[g * LANES:(g + 1) * LANES, :] = jnp.where(lane_g == g, spread, 0.0).astype(BF16)

    u = u_ref[0]
    e = jnp.dot(u, em_ref[...], preferred_element_type=F32)
    nchunk = SEQ // SSM_CHUNK
    rows = lax.broadcasted_iota(jnp.int32, (e.shape[0], LANES), 0) % nchunk
    zprev = []
    for g in range(SSM_BUNDLE):
        eg = e[:, g * LANES:(g + 1) * LANES]
        for k in range(SSM_SCAN_STEPS):
            s = _shift_rows(eg, 1 << k, rows)
            sl = slice(g * LANES, (g + 1) * LANES)
            eg = eg + ar_ref[0, k:k + 1, sl] * s + ai_ref[0, k:k + 1, sl] * pltpu.roll(s, SSM_STATE, 1)
        zprev.append(_shift_rows(eg, 1, rows).astype(BF16))
    zp = jnp.concatenate(zprev, axis=1)

    for mblk in range(SSM_BW // SSM_COLS):
        c0, kdim = mblk * SSM_COLS, (mblk + 1) * SSM_COLS
        y = jnp.dot(u[:, :kdim], km_ref[0:kdim, c0:c0 + SSM_COLS], preferred_element_type=F32)
        y = y + jnp.dot(zp, cm_ref[:, c0:c0 + SSM_COLS], preferred_element_type=F32)
        o_ref[0, :, c0:c0 + SSM_COLS] = jax.nn.gelu(y).astype(BF16)


def _ssm(u8, layer, a_lag, eg, cg, spread, ar, ai):
    nb, rows, _ = u8.shape
    T, GB, H, Q = SSM_CHUNK, SSM_BUNDLE, SSM_GROUP, 2 * SSM_STATE
    par = lambda *dims: pl.BlockSpec((1,) + dims, lambda i: (layer * nb + i,) + (0,) * len(dims))
    act = pl.BlockSpec((1, rows, SSM_BW), lambda i: (i, 0, 0))
    return pl.pallas_call(
        _ssm_kernel,
        out_shape=jax.ShapeDtypeStruct((nb, rows, SSM_BW), BF16),
        grid=(nb,),
        in_specs=[act, par(T, H, LANES), par(GB, T * H, Q), par(GB, Q, T * H),
                  pl.BlockSpec((T * H, SSM_BW), lambda i: (0, 0)), par(8, SSM_SW), par(8, SSM_SW)],
        out_specs=act,
        scratch_shapes=[pltpu.VMEM((SSM_BW, SSM_BW), BF16), pltpu.VMEM((SSM_BW, SSM_SW), BF16),
                        pltpu.VMEM((SSM_SW, SSM_BW), BF16)],
        compiler_params=_params("arbitrary"),
        name="s5_chunked_ssm",
    )(u8, a_lag, eg, cg, spread, ar, ai)


def _cmul(xr, xi, yr, yi):
    return xr * yr - xi * yi, xr * yi + xi * yr


def _ssm_group_kernel(cr_ref, ci_ref, br_ref, bi_ref, pr_ref, pi_ref, qr_ref, qi_ref, d_ref, a_ref, e_ref, c_ref):
    T, H, P = SSM_CHUNK, SSM_GROUP, SSM_STATE
    sh = H.bit_length() - 1
    lane = lax.broadcasted_iota(jnp.int32, (H, LANES), 1)
    row = lax.broadcasted_iota(jnp.int32, (H, LANES), 0)
    lags = [jnp.zeros((H, LANES), F32)] * T
    for g in range(cr_ref.shape[0]):
        cr, ci, br, bi = cr_ref[g], ci_ref[g], br_ref[g], bi_ref[g]
        pr, pi = pr_ref[g][:, None, :], pi_ref[g][:, None, :]
        car = cr[None] * pr - ci[None] * pi
        cai = cr[None] * pi + ci[None] * pr
        ca = jnp.concatenate([car[:T].reshape(T * H, P), cai[:T].reshape(T * H, P)], axis=1)
        bb = jnp.concatenate([br, -bi], axis=1)
        k = lax.dot_general(bb, ca, (((1,), (1,)), ((), ())),
                            precision=lax.Precision.HIGHEST, preferred_element_type=F32)
        per_col = LANES // H
        for t in range(T):
            col = k[:, (t // per_col) * LANES:(t // per_col + 1) * LANES]
            shift = ((g - t % per_col) * H) % LANES
            moved = pltpu.roll(col, shift, 1) if shift else col
            lags[t] = jnp.where((lane >> sh) == g, moved, lags[t])
        qr, qi = qr_ref[g][:, None, :], qi_ref[g][:, None, :]
        er = (qr * br[None] - qi * bi[None]).reshape(T * H, P)
        ei = (qr * bi[None] + qi * br[None]).reshape(T * H, P)
        e_ref[g] = jnp.concatenate([er, ei], axis=1).astype(BF16)
        c_ref[g] = jnp.concatenate([car[1:].reshape(T * H, P).T, -cai[1:].reshape(T * H, P).T], axis=0).astype(BF16)
    lags[0] = lags[0] + jnp.where((lane & (H - 1)) == row, d_ref[0], 0.0)
    for t in range(T):
        a_ref[0, t] = lags[t].astype(BF16)


def _ssm_group_operators(cr, ci, brt, bit, pr, pi, qr, qi, d_lane):
    n = cr.shape[0]
    T, H, P = SSM_CHUNK, SSM_GROUP, SSM_STATE
    spec = lambda a, b: pl.BlockSpec((SSM_BUNDLE, a, b), lambda i: (i, 0, 0))
    return pl.pallas_call(
        _ssm_group_kernel,
        out_shape=(jax.ShapeDtypeStruct((n // SSM_BUNDLE, T, H, LANES), BF16),
                   jax.ShapeDtypeStruct((n, T * H, 2 * P), BF16), jax.ShapeDtypeStruct((n, 2 * P, T * H), BF16)),
        grid=(n // SSM_BUNDLE,),
        in_specs=[spec(H, P)] * 4 + [spec(T + 1, P)] * 2 + [spec(T, P)] * 2
                 + [pl.BlockSpec((1, 1, LANES), lambda i: (i, 0, 0))],
        out_specs=(pl.BlockSpec((1, T, H, LANES), lambda i: (i, 0, 0, 0)), spec(T * H, 2 * P), spec(2 * P, T * H)),
        compiler_params=_params("parallel"),
        name="ssm_group_operators",
    )(cr, ci, brt, bit, pr, pi, qr, qi, d_lane)


def _ssm_operators(lam_re, lam_im, log_dt, b_re, b_im, c_re, c_im, d):
    lr, li = lam_re.astype(F32), lam_im.astype(F32)
    dt = jnp.exp(log_dt.astype(F32))[..., None]
    mag = jnp.exp(lr * dt)
    ar, ai = mag * jnp.cos(li * dt), mag * jnp.sin(li * dt)
    nr, ni = ar - 1.0, ai
    den = lr * lr + li * li
    zr = (nr * lr + ni * li) / den
    zi = (ni * lr - nr * li) / den
    br, bi = b_re.astype(F32), b_im.astype(F32)
    bbr = zr[..., None] * br - zi[..., None] * bi
    bbi = zr[..., None] * bi + zi[..., None] * br
    cr, ci = c_re.astype(F32), c_im.astype(F32)

    pr, pi = [jnp.ones_like(ar)], [jnp.zeros_like(ar)]
    for _ in range(SSM_CHUNK):
        nr_, ni_ = _cmul(pr[-1], pi[-1], ar, ai)
        pr.append(nr_)
        pi.append(ni_)
    prr, pir = jnp.stack(pr[SSM_CHUNK - 1::-1]), jnp.stack(pi[SSM_CHUNK - 1::-1])
    pr, pi = jnp.stack(pr), jnp.stack(pi)

    T, G, H, P = SSM_CHUNK, SSM_N_GROUPS, SSM_GROUP, SSM_STATE
    NB, GB = SSM_N_BUNDLES, SSM_BUNDLE
    L = lr.shape[0]
    per_group = lambda t: jnp.moveaxis(t, 0, 2).reshape(L * G, t.shape[0], P)
    a_lag, eg, cg = _ssm_group_operators(
        cr.reshape(L * G, H, P), ci.reshape(L * G, H, P),
        bbr.transpose(0, 1, 3, 2).reshape(L * G, H, P), bbi.transpose(0, 1, 3, 2).reshape(L * G, H, P),
        per_group(pr), per_group(pi), per_group(prr), per_group(pir), d.astype(F32).reshape(L * NB, 1, GB * H))
    eg = eg.reshape(L * NB, GB, T * H, 2 * P)
    cg = cg.reshape(L * NB, GB, 2 * P, T * H)

    sr, si = [pr[T]], [pi[T]]
    for _ in range(SSM_SCAN_STEPS - 1):
        nr_, ni_ = _cmul(sr[-1], si[-1], sr[-1], si[-1])
        sr.append(nr_)
        si.append(ni_)
    sr.append(jnp.zeros_like(ar))
    si.append(jnp.zeros_like(ar))
    sr, si = jnp.stack(sr, 2), jnp.stack(si, 2)
    lay = lambda t: t.reshape(L, NB, GB, 8, 2 * P).transpose(0, 1, 3, 2, 4).reshape(L * NB, 8, GB * 2 * P)
    ar_t = lay(jnp.concatenate([sr, sr], -1))
    ai_t = lay(jnp.concatenate([-si, si], -1))

    spread = np.zeros((T, H, T, GB, H), np.float32)
    for i in range(T):
        for h in range(H):
            spread[i, h, i, :, h] = 1.0
    spread = jnp.asarray(spread.reshape(T * H, T * GB * H), BF16)
    return a_lag, eg, cg, spread, ar_t, ai_t


def _deepnorm_ln(x, branch, g, b, alpha):
    y = alpha * x + branch
    mu = y.mean(-1, keepdims=True)
    yc = y - mu
    var = jnp.square(yc).mean(-1, keepdims=True)
    return yc * lax.rsqrt(var + LN_EPS) * g + b


OUT_TM = 512
OUT_SUB = 256


def _outproj_kernel(alpha, ya_ref, yp_ref, ys_ref, glu_ref, w_ref, x_ref, g_ref, b_ref, o_ref, ob_ref, tok_ref):
    for b in range(SSM_N_BUNDLES):
        for j in range(SSM_CHUNK):
            rows = ys_ref[b, :, j * LANES:(j + 1) * LANES].astype(F32)
            tok_ref[b, pl.ds(j, OUT_TM // SSM_CHUNK, stride=SSM_CHUNK), :] = rows
    ys = jnp.concatenate([tok_ref[b].astype(BF16) for b in range(SSM_N_BUNDLES)], axis=1)
    ab = jnp.dot(ys, glu_ref[...].astype(BF16), preferred_element_type=F32)
    y_ssm = (ab[:, :SSM_WIDTH] * jax.nn.sigmoid(ab[:, SSM_WIDTH:])).astype(BF16)
    o_p, o_s = ATTN_WIDTH, ATTN_WIDTH + POOL_WIDTH
    for r in range(0, OUT_TM, OUT_SUB):
        rs = slice(r, r + OUT_SUB)
        acc = jnp.dot(ya_ref[rs, :], w_ref[0:o_p, :], preferred_element_type=F32)
        acc += jnp.dot(yp_ref[rs, :], w_ref[o_p:o_s, :], preferred_element_type=F32)
        acc += jnp.dot(y_ssm[rs, :], w_ref[o_s:, :], preferred_element_type=F32)
        x1 = _deepnorm_ln(x_ref[rs, :], acc, g_ref[...], b_ref[...], alpha)
        o_ref[rs, :] = x1
        ob_ref[rs, :] = x1.astype(BF16)


def _out_proj(ya, yp, ys8, glu_w2d, layer, w_out_bf, x, g, b, alpha):
    m = x.shape[0]
    row = lambda width: pl.BlockSpec((OUT_TM, width), lambda i: (i, 0))
    vec = pl.BlockSpec((1, D_MODEL), lambda i: (0, 0))
    return pl.pallas_call(
        functools.partial(_outproj_kernel, alpha),
        out_shape=(jax.ShapeDtypeStruct((m, D_MODEL), F32), jax.ShapeDtypeStruct((m, D_MODEL), BF16)),
        grid=(m // OUT_TM,),
        in_specs=[
            row(ATTN_WIDTH), row(POOL_WIDTH),
            pl.BlockSpec((SSM_N_BUNDLES, OUT_TM // SSM_CHUNK, SSM_BW), lambda i: (0, i, 0)),
            pl.BlockSpec((SSM_WIDTH, 2 * SSM_WIDTH), lambda i: (layer, 0)),
            _resident((D_MODEL, D_MODEL), lambda i: (0, 0)),
            row(D_MODEL), vec, vec,
        ],
        out_specs=(row(D_MODEL), row(D_MODEL)),
        scratch_shapes=[pltpu.VMEM((SSM_N_BUNDLES, OUT_TM, LANES), F32)],
        compiler_params=_params("parallel"),
        name="out_proj_ln",
    )(ya, yp, ys8, glu_w2d, w_out_bf, x, g, b)


UP_TM = 1024
UP_TN = 512
UP_NJ = -(-D_FF // UP_TN)
FF_PAD = UP_NJ * UP_TN
UP_SHIFT = FF_PAD - D_FF
HALO = 8
CONV_ROWS = 8
DOWN_BLOCK_ROWS = 128
DOWN_BLOCKS = D_FF // DOWN_BLOCK_ROWS


def _ffn_up_kernel(x_ref, wv_ref, wg_ref, cv_ref, cg_ref, wd_ref, o_ref, wdb_ref, wvb_ref, wgb_ref, hv_ref, hg_ref):
    j, i = pl.program_id(0), pl.program_id(1)

    @pl.when(j * pl.num_programs(1) + i < DOWN_BLOCKS)
    def _():
        wdb_ref[...] = wd_ref[...].astype(BF16)

    @pl.when(i == 0)
    def _():
        for r in range(0, D_MODEL, CAST_ROWS):
            wvb_ref[r:r + CAST_ROWS, :] = wv_ref[r:r + CAST_ROWS, :].astype(BF16)
            wgb_ref[r:r + CAST_ROWS, :] = wg_ref[r:r + CAST_ROWS, :].astype(BF16)
        hv_ref[0:HALO, :] = jnp.zeros((HALO, UP_TN), F32)
        hg_ref[0:HALO, :] = jnp.zeros((HALO, UP_TN), F32)

    @pl.when(i != 0)
    def _():
        hv_ref[0:HALO, :] = hv_ref[UP_TM:, :]
        hg_ref[0:HALO, :] = hg_ref[UP_TM:, :]

    x = x_ref[...]
    seq_start = (i % (SEQ // UP_TM)) == 0
    rows = lax.broadcasted_iota(jnp.int32, (HALO, 1), 0)

    def conv(wb_ref, c_ref, h_ref):
        h_ref[HALO:, :] = jnp.dot(x, wb_ref[...], preferred_element_type=F32)

        def tap(shift):
            h = h_ref[HALO - shift:HALO - shift + UP_TM, :]
            head = jnp.where(seq_start & (rows < shift), 0.0, h[:HALO])
            return jnp.concatenate([head, h[HALO:]], axis=0)

        return c_ref[3:4, :] + tap(2) * c_ref[0:1, :] + tap(1) * c_ref[1:2, :] + h_ref[HALO:, :] * c_ref[2:3, :]

    gate = jax.nn.silu(conv(wgb_ref, cg_ref, hg_ref))
    act = (gate * conv(wvb_ref, cv_ref, hv_ref)).astype(BF16)
    last = pl.num_programs(0) - 1

    @pl.when(j != last)
    def _():
        o_ref[...] = act

    @pl.when(j == last)
    def _():
        o_ref[...] = jnp.concatenate([act[:, UP_SHIFT:], jnp.zeros((UP_TM, UP_SHIFT), BF16)], axis=1)


def _ffn_up(x_bf, w_up2d, layer, conv8, w_down4d):
    m = x_bf.shape[0]
    ni = m // UP_TM
    assert UP_NJ * ni >= DOWN_BLOCKS
    col = lambda c0, j: pl.multiple_of(c0 + jnp.minimum(j * UP_TN, D_FF - UP_TN), LANES)
    window = lambda rows, r0, c0: pl.BlockSpec((pl.Element(rows), pl.Element(UP_TN)),
                                               lambda j, i: (r0, col(c0, j)))
    dblk = lambda j, i: jnp.minimum(j * ni + i, DOWN_BLOCKS - 1)
    return pl.pallas_call(
        _ffn_up_kernel,
        out_shape=(jax.ShapeDtypeStruct((m, FF_PAD), BF16),
                   jax.ShapeDtypeStruct((DOWN_BLOCKS, DOWN_BLOCK_ROWS, D_MODEL), BF16)),
        grid=(UP_NJ, ni),
        in_specs=[
            pl.BlockSpec((UP_TM, D_MODEL), lambda j, i: (i, 0)),
            window(D_MODEL, layer * D_MODEL, 0), window(D_MODEL, layer * D_MODEL, D_FF),
            window(CONV_ROWS, layer * CONV_ROWS, 0), window(CONV_ROWS, layer * CONV_ROWS, D_FF),
            pl.BlockSpec((None, None, DOWN_BLOCK_ROWS, D_MODEL), lambda j, i: (layer, dblk(j, i), 0, 0)),
        ],
        out_specs=(pl.BlockSpec((UP_TM, UP_TN), lambda j, i: (i, j)),
                   pl.BlockSpec((None, DOWN_BLOCK_ROWS, D_MODEL), lambda j, i: (dblk(j, i), 0, 0))),
        scratch_shapes=[pltpu.VMEM((D_MODEL, UP_TN), BF16), pltpu.VMEM((D_MODEL, UP_TN), BF16),
                        pltpu.VMEM((HALO + UP_TM, UP_TN), F32), pltpu.VMEM((HALO + UP_TM, UP_TN), F32)],
        compiler_params=_params("arbitrary", "arbitrary"),
        name="ffn_up_conv_gate",
    )(x_bf, w_up2d, w_up2d, conv8, conv8, w_down4d)


DOWN_TM = 512
DOWN_SUB = 256


def _ffn_down_kernel(alpha, a_ref, w_ref, x_ref, g_ref, b_ref, o_ref):
    for r in range(0, DOWN_TM, DOWN_SUB):
        f = jnp.dot(a_ref[r:r + DOWN_SUB, :], w_ref[...], preferred_element_type=F32)
        o_ref[r:r + DOWN_SUB, :] = _deepnorm_ln(x_ref[r:r + DOWN_SUB, :], f, g_ref[...], b_ref[...], alpha)


def _ffn_down(act, w_down_bf, x, g, b, alpha):
    m = x.shape[0]
    vec = pl.BlockSpec((1, D_MODEL), lambda i: (0, 0))
    return pl.pallas_call(
        functools.partial(_ffn_down_kernel, alpha),
        out_shape=jax.ShapeDtypeStruct((m, D_MODEL), F32),
        grid=(m // DOWN_TM,),
        in_specs=[
            pl.BlockSpec((DOWN_TM, D_FF), lambda i: (i, 0)),
            _resident((D_FF, D_MODEL), lambda i: (0, 0)),
            pl.BlockSpec((DOWN_TM, D_MODEL), lambda i: (i, 0)),
            vec, vec,
        ],
        out_specs=pl.BlockSpec((DOWN_TM, D_MODEL), lambda i: (i, 0)),
        compiler_params=_params("parallel"),
        name="ffn_down_ln",
    )(act, w_down_bf, x, g, b)


def kernel(x, w_in, attn_sinks, pool_w, pool_scale, ssm_lam_re, ssm_lam_im, ssm_log_dt, ssm_b_re, ssm_b_im,
           ssm_c_re, ssm_c_im, ssm_d, ssm_glu_w, w_out, ln1_g, ln1_b, ffn_w_up, ffn_conv_w, ffn_conv_b,
           ffn_w_down, ln2_g, ln2_b):
    bsz, s_len, _ = x.shape
    assert s_len == SEQ and x.shape[2] == D_MODEL
    depth = w_in.shape[0]
    alpha = (2 * depth) ** 0.25
    m = bsz * s_len
    cos_t, sa_t, sb_t = _rope_tables()
    xf = x.reshape(m, D_MODEL).astype(F32)

    w_out2d = w_out.astype(F32).reshape(depth * D_MODEL, D_MODEL)
    w_up2d = ffn_w_up.astype(F32).reshape(depth * D_MODEL, 2 * D_FF)
    w_down4d = ffn_w_down.astype(F32).reshape(depth, DOWN_BLOCKS, DOWN_BLOCK_ROWS, D_MODEL)
    conv8 = jnp.concatenate([ffn_conv_w.astype(F32), ffn_conv_b.astype(F32)[:, None, :],
                             jnp.zeros((depth, CONV_ROWS - 4, 2 * D_FF), F32)], axis=1)
    conv8 = conv8.reshape(depth * CONV_ROWS, 2 * D_FF)
    w_in2d = w_in.astype(F32).reshape(depth * D_MODEL, 2 * HALF_WIDTH)
    glu_w2d = ssm_glu_w.astype(F32).reshape(depth * SSM_WIDTH, 2 * SSM_WIDTH)
    pool_w3d = pool_w.astype(F32).reshape(depth * len(POOL_WINDOWS), POOL_GROUP, POOL_GROUP)
    vec = lambda a: a.astype(F32).reshape(1, -1)
    ssm_ops = _ssm_operators(ssm_lam_re, ssm_lam_im, ssm_log_dt, ssm_b_re, ssm_b_im, ssm_c_re, ssm_c_im, ssm_d)

    for l in range(depth):
        q, k2, v2, pu, su, w_out_bf = _in_proj(xf, w_in2d, w_out2d, l, cos_t, sa_t, sb_t)
        y_attn = _attention(q, k2, v2, attn_sinks[l].astype(F32))
        y_pool = _pool(pu, pool_w3d, l, vec(pool_scale[l]))

        y8 = _ssm(su, l, *ssm_ops)

        x1, x1_bf = _out_proj(y_attn, y_pool, y8, glu_w2d, l, w_out_bf, xf,
                              vec(ln1_g[l]), vec(ln1_b[l]), alpha)
        act, w_down_bf = _ffn_up(x1_bf, w_up2d, l, conv8, w_down4d)
        xf = _ffn_down(act, w_down_bf.reshape(D_FF, D_MODEL), x1, vec(ln2_g[l]), vec(ln2_b[l]), alpha)

    return xf.reshape(bsz, s_len, D_MODEL).astype(x.dtype)
```

```python
import functools

import jax
import numpy as np
import jax.numpy as jnp
from jax import lax
from jax.experimental import pallas as pl
from jax.experimental.pallas import tpu as pltpu

F32 = jnp.float32
BF16 = jnp.bfloat16

D_MODEL = 2048
SEQ = 2048
HEAD_DIM = 64
N_Q_HEADS = 16
N_KV_HEADS = 4
ATTN_WIDTH = N_Q_HEADS * HEAD_DIM
KV_WIDTH = N_KV_HEADS * HEAD_DIM
ATTN_BLOCK = 128
ROPE_THETA = 10000.0
POOL_WINDOWS = (2, 4, 8, 16)
POOL_GROUP = 128
POOL_WIDTH = 512
SSM_WIDTH = 512
SSM_GROUP = 16
SSM_N_GROUPS = 32
SSM_STATE = 64
SSM_CHUNK = 16
LN_EPS = 1e-5
D_FF = 5504

LANES = 128
SSM_BUNDLE = LANES // SSM_GROUP
SSM_N_BUNDLES = SSM_N_GROUPS // SSM_BUNDLE
SSM_BW = SSM_CHUNK * LANES
SSM_SW = SSM_BUNDLE * 2 * SSM_STATE
HALF_WIDTH = ATTN_WIDTH + KV_WIDTH
KV2_WIDTH = 2 * KV_WIDTH
VMEM_LIMIT = 56 * 1024 * 1024


def _params(*sem):
    return pltpu.CompilerParams(dimension_semantics=sem, vmem_limit_bytes=VMEM_LIMIT)


def _resident(block, index_map):
    return pl.BlockSpec(block, index_map, pipeline_mode=pl.Buffered(1))


IN_TM = 512
CAST_ROWS = 256


def _cast_weight_once(w_ref, wbf_ref):
    @pl.when(pl.program_id(0) == 0)
    def _():
        for r in range(0, w_ref.shape[0], CAST_ROWS):
            wbf_ref[r:r + CAST_ROWS, :] = w_ref[r:r + CAST_ROWS, :].astype(BF16)


def _dup_heads(pair):
    lo = lax.broadcasted_iota(jnp.int32, pair.shape, 1) < HEAD_DIM
    swapped = pltpu.roll(pair, HEAD_DIM, 1)
    return jnp.where(lo, pair, swapped), jnp.where(lo, swapped, pair)


def _inproj_kernel(x_ref, wa_ref, wb_ref, cos_ref, sa_ref, sb_ref, wo_ref, q_ref, k_ref, v_ref, p_ref, s_ref,
                   wobf_ref, wabf_ref, wbbf_ref, tok_ref):
    _cast_weight_once(wa_ref, wabf_ref)
    _cast_weight_once(wb_ref, wbbf_ref)
    wobf_ref[...] = wo_ref[...].astype(BF16)
    xb = x_ref[...].astype(BF16)
    acc = jnp.dot(xb, wabf_ref[...], preferred_element_type=F32)
    cos, sa, sb = cos_ref[...], sa_ref[...], sb_ref[...]
    nq = ATTN_WIDTH // LANES
    for c in range(HALF_WIDTH // LANES):
        a = acc[:, c * LANES:(c + 1) * LANES]
        r = a * cos + pltpu.roll(a, LANES - 32, 1) * sa + pltpu.roll(a, 32, 1) * sb
        if c < nq:
            q_ref[:, c * LANES:(c + 1) * LANES] = (r * (HEAD_DIM ** -0.5)).astype(BF16)
        else:
            ka, kb = _dup_heads(r)
            g = 2 * (c - nq)
            k_ref[:, g * LANES:(g + 1) * LANES] = ka.astype(BF16)
            k_ref[:, (g + 1) * LANES:(g + 2) * LANES] = kb.astype(BF16)

    acc = jnp.dot(xb, wbbf_ref[...], preferred_element_type=F32)
    for c in range(KV_WIDTH // LANES):
        va, vb = _dup_heads(acc[:, c * LANES:(c + 1) * LANES])
        v_ref[:, 2 * c * LANES:(2 * c + 1) * LANES] = va.astype(BF16)
        v_ref[:, (2 * c + 1) * LANES:(2 * c + 2) * LANES] = vb.astype(BF16)
    p_ref[...] = acc[:, KV_WIDTH:KV_WIDTH + POOL_WIDTH].astype(BF16)
    for b in range(SSM_N_BUNDLES):
        c0 = KV_WIDTH + POOL_WIDTH + b * LANES
        tok_ref[b] = acc[:, c0:c0 + LANES]
        for j in range(SSM_CHUNK):
            rows = tok_ref[b, pl.ds(j, IN_TM // SSM_CHUNK, stride=SSM_CHUNK), :]
            s_ref[b, :, j * LANES:(j + 1) * LANES] = rows.astype(BF16)


def _in_proj(x, w_in2d, w_out2d, layer, cos, sa, sb):
    m = x.shape[0]
    nseq = SEQ // IN_TM
    x_spec = pl.BlockSpec((IN_TM, D_MODEL), lambda i: (i, 0))
    tab = pl.BlockSpec((IN_TM, LANES), lambda i: (i % nseq, 0))
    row = lambda width: pl.BlockSpec((IN_TM, width), lambda i: (i, 0))
    steps = m // IN_TM
    wo_rows = D_MODEL // steps
    return pl.pallas_call(
        _inproj_kernel,
        out_shape=(jax.ShapeDtypeStruct((m, ATTN_WIDTH), BF16), jax.ShapeDtypeStruct((m, KV2_WIDTH), BF16),
                   jax.ShapeDtypeStruct((m, KV2_WIDTH), BF16), jax.ShapeDtypeStruct((m, POOL_WIDTH), BF16),
                   jax.ShapeDtypeStruct((SSM_N_BUNDLES, m // SSM_CHUNK, SSM_BW), BF16),
                   jax.ShapeDtypeStruct((D_MODEL, D_MODEL), BF16)),
        grid=(steps,),
        in_specs=[x_spec, _resident((D_MODEL, HALF_WIDTH), lambda i: (layer, 0)),
                  _resident((D_MODEL, HALF_WIDTH), lambda i: (layer, 1)), tab, tab, tab,
                  pl.BlockSpec((wo_rows, D_MODEL), lambda i: (layer * steps + i, 0))],
        out_specs=(row(ATTN_WIDTH), row(KV2_WIDTH), row(KV2_WIDTH), row(POOL_WIDTH),
                   pl.BlockSpec((SSM_N_BUNDLES, IN_TM // SSM_CHUNK, SSM_BW), lambda i: (0, i, 0)),
                   pl.BlockSpec((wo_rows, D_MODEL), lambda i: (i, 0))),
        scratch_shapes=[pltpu.VMEM((D_MODEL, HALF_WIDTH), BF16), pltpu.VMEM((D_MODEL, HALF_WIDTH), BF16),
                        pltpu.VMEM((SSM_N_BUNDLES, IN_TM, LANES), F32)],
        compiler_params=_params("arbitrary"),
        name="in_proj",
    )(x, w_in2d, w_in2d, cos, sa, sb, w_out2d)


def _rope_tables():
    half = HEAD_DIM // 2
    inv = ROPE_THETA ** (-jnp.arange(half, dtype=F32) / half)
    ang = jnp.arange(SEQ, dtype=F32)[:, None] * inv[None, :]
    cos, sin = jnp.cos(ang), jnp.sin(ang)
    zero = jnp.zeros_like(sin)
    reps = LANES // HEAD_DIM
    cos_t = jnp.tile(jnp.concatenate([cos, cos], -1), (1, reps))
    sa_t = jnp.tile(jnp.concatenate([-sin, zero], -1), (1, reps))
    sb_t = jnp.tile(jnp.concatenate([zero, sin], -1), (1, reps))
    return cos_t, sa_t, sb_t


ATTN_TQ = 512
ATTN_QB = ATTN_TQ // ATTN_BLOCK
KEYS = 2 * ATTN_BLOCK


def _attn_kernel(sink_ref, q_ref, kc_ref, vc_ref, kp_ref, vp_ref, o_ref, kbuf, vbuf):
    kbuf[0:ATTN_BLOCK, :] = kp_ref[...]
    kbuf[ATTN_BLOCK:, :] = kc_ref[...]
    vbuf[0:ATTN_BLOCK, :] = vp_ref[...]
    vbuf[ATTN_BLOCK:, :] = vc_ref[...]
    seq_start = pl.program_id(1) == 0

    row = lax.broadcasted_iota(jnp.int32, (ATTN_BLOCK, KEYS), 0)
    col = lax.broadcasted_iota(jnp.int32, (ATTN_BLOCK, KEYS), 1)
    dist = row + ATTN_BLOCK - col
    band = (dist >= 0) & (dist < ATTN_BLOCK)
    band_first = band & (jnp.logical_not(seq_start) | (col >= ATTN_BLOCK))
    lo_kv = lax.broadcasted_iota(jnp.int32, (KEYS, LANES), 1) < HEAD_DIM
    lo_out = lax.broadcasted_iota(jnp.int32, (ATTN_BLOCK, LANES), 1) < HEAD_DIM
    zero_kv = jnp.zeros((KEYS, LANES), BF16)

    def split_heads(x):
        return jnp.concatenate([jnp.where(lo_kv, x, zero_kv), jnp.where(lo_kv, zero_kv, x)], axis=0)

    for qb in range(ATTN_QB):
        valid = band_first if qb == 0 else band
        r0 = qb * ATTN_BLOCK
        for g in range(N_KV_HEADS):
            k2 = split_heads(kbuf[r0:r0 + KEYS, g * LANES:(g + 1) * LANES])
            v2 = split_heads(vbuf[r0:r0 + KEYS, g * LANES:(g + 1) * LANES])
            c0 = 2 * g * LANES
            qq = jnp.concatenate([q_ref[r0:r0 + ATTN_BLOCK, c0:c0 + LANES],
                                  q_ref[r0:r0 + ATTN_BLOCK, c0 + LANES:c0 + 2 * LANES]], axis=0)
            s4 = lax.dot_general(qq, k2, (((1,), (1,)), ((), ())), preferred_element_type=F32)
            p_rows, dens = [], []
            for pr in range(2):
                ps = []
                for hh in range(2):
                    sink = sink_ref[4 * g + 2 * pr + hh]
                    s = jnp.where(valid, s4[pr * ATTN_BLOCK:(pr + 1) * ATTN_BLOCK, hh * KEYS:(hh + 1) * KEYS], -1e30)
                    mx = jnp.maximum(s.max(-1, keepdims=True), sink)
                    p = jnp.exp(s - mx)
                    dens.append(p.sum(-1, keepdims=True) + jnp.exp(sink - mx))
                    ps.append(p.astype(BF16))
                p_rows.append(jnp.concatenate(ps, axis=1))
            o4 = jnp.dot(jnp.concatenate(p_rows, axis=0), v2, preferred_element_type=F32)
            for pr in range(2):
                o = o4[pr * ATTN_BLOCK:(pr + 1) * ATTN_BLOCK] / jnp.where(lo_out, dens[2 * pr], dens[2 * pr + 1])
                o_ref[r0:r0 + ATTN_BLOCK, c0 + pr * LANES:c0 + (pr + 1) * LANES] = o.astype(BF16)


def _attention(q, k2, v2, sinks):
    m = q.shape[0]
    nq = SEQ // ATTN_TQ
    cur = lambda b, i: (b * nq + i, 0)
    prev = lambda b, i: (jnp.maximum((b * nq + i) * ATTN_QB - 1, 0), 0)
    return pl.pallas_call(
        _attn_kernel,
        out_shape=jax.ShapeDtypeStruct((m, ATTN_WIDTH), BF16),
        grid=(m // SEQ, nq),
        in_specs=[
            pl.BlockSpec(memory_space=pltpu.SMEM),
            pl.BlockSpec((ATTN_TQ, ATTN_WIDTH), cur),
            pl.BlockSpec((ATTN_TQ, KV2_WIDTH), cur),
            pl.BlockSpec((ATTN_TQ, KV2_WIDTH), cur),
            pl.BlockSpec((ATTN_BLOCK, KV2_WIDTH), prev),
            pl.BlockSpec((ATTN_BLOCK, KV2_WIDTH), prev),
        ],
        out_specs=pl.BlockSpec((ATTN_TQ, ATTN_WIDTH), cur),
        scratch_shapes=[pltpu.VMEM((ATTN_TQ + ATTN_BLOCK, KV2_WIDTH), BF16),
                        pltpu.VMEM((ATTN_TQ + ATTN_BLOCK, KV2_WIDTH), BF16)],
        compiler_params=_params("parallel", "arbitrary"),
        name="swa_attention",
    )(sinks, q, k2, v2, k2, v2)


def _shift_rows(x, d, rows):
    return jnp.where(rows >= d, pltpu.roll(x, d, 0), 0.0)


def _pool_kernel(u_ref, w_ref, scale_ref, o_ref):
    rows = lax.broadcasted_iota(jnp.int32, (SEQ, POOL_GROUP), 0)
    t1 = lax.broadcasted_iota(jnp.int32, (SEQ, 1), 0).astype(F32) + 1.0
    for idx, w in enumerate(POOL_WINDOWS):
        cols = slice(idx * POOL_GROUP, (idx + 1) * POOL_GROUP)
        u = u_ref[:, cols].astype(F32)
        s, d = u, 1
        while d < w:
            s = s + _shift_rows(s, d, rows)
            d *= 2
        mean = s / jnp.minimum(t1, float(w))
        y = jnp.dot((mean - u).astype(BF16), w_ref[idx].astype(BF16), preferred_element_type=F32)
        o_ref[:, cols] = (y * scale_ref[:, cols]).astype(BF16)


def _pool(pu, pool_w3d, layer, pool_scale_l):
    m = pu.shape[0]
    nw = len(POOL_WINDOWS)
    return pl.pallas_call(
        _pool_kernel,
        out_shape=jax.ShapeDtypeStruct((m, POOL_WIDTH), BF16),
        grid=(m // SEQ,),
        in_specs=[
            pl.BlockSpec((SEQ, POOL_WIDTH), lambda b: (b, 0)),
            pl.BlockSpec((nw, POOL_GROUP, POOL_GROUP), lambda b: (layer, 0, 0)),
            pl.BlockSpec((1, POOL_WIDTH), lambda b: (0, 0)),
        ],
        out_specs=pl.BlockSpec((SEQ, POOL_WIDTH), lambda b: (b, 0)),
        compiler_params=_params("parallel"),
        name="multiscale_pool",
    )(pu, pool_w3d, pool_scale_l)


SSM_SCAN_STEPS = 7
SSM_COLS = 2 * LANES


def _ssm_kernel(u_ref, a_ref, eg_ref, cg_ref, x_ref, ar_ref, ai_ref, o_ref, km_ref, em_ref, cm_ref):
    T, GB, H = SSM_CHUNK, SSM_BUNDLE, SSM_GROUP
    sh = H.bit_length() - 1

    @pl.when(pl.program_id(0) == 0)
    def _():
        em_ref[...] = jnp.zeros_like(em_ref)

    row_g = lax.broadcasted_iota(jnp.int32, (LANES, LANES), 0) >> sh
    col_g = lax.broadcasted_iota(jnp.int32, (LANES, LANES), 1) >> sh
    zero_blk = jnp.zeros((LANES, LANES), BF16)
    lag_blk = [jnp.where(row_g == col_g, jnp.concatenate([a_ref[0, t]] * GB, axis=0), zero_blk) for t in range(T)]
    for i in range(T):
        for j in range(min(i | 1, T - 1) + 1):
            km_ref[j * LANES:(j + 1) * LANES, i * LANES:(i + 1) * LANES] = lag_blk[i - j] if j <= i else zero_blk

    for j in range(T):
        for g in range(GB):
            r0 = j * LANES + g * H
            em_ref[r0:r0 + H, g * LANES:(g + 1) * LANES] = eg_ref[0, g, j * H:(j + 1) * H, :]

    lane_g = (lax.broadcasted_iota(jnp.int32, (LANES, SSM_BW), 1) & (LANES - 1)) >> sh
    for g in range(GB):
        spread = jnp.dot(cg_ref[0, g], x_ref[...], preferred_element_type=F32)
        cm_ref[g * LANES:(g + 1) * LANES, :] = jnp.where(lane_g == g, spread, 0.0).astype(BF16)

    u = u_ref[0]
    e = jnp.dot(u, em_ref[...], preferred_element_type=F32)
    nchunk = SEQ // SSM_CHUNK
    rows = lax.broadcasted_iota(jnp.int32, (e.shape[0], LANES), 0) % nchunk
    zprev = []
    for g in range(SSM_BUNDLE):
        eg = e[:, g * LANES:(g + 1) * LANES]
        for k in range(SSM_SCAN_STEPS):
            s = _shift_rows(eg, 1 << k, rows)
            sl = slice(g * LANES, (g + 1) * LANES)
            eg = eg + ar_ref[0, k:k + 1, sl] * s + ai_ref[0, k:k + 1, sl] * pltpu.roll(s, SSM_STATE, 1)
        zprev.append(_shift_rows(eg, 1, rows).astype(BF16))
    zp = jnp.concatenate(zprev, axis=1)

    for mblk in range(SSM_BW // SSM_COLS):
        c0, kdim = mblk * SSM_COLS, (mblk + 1) * SSM_COLS
        y = jnp.dot(u[:, :kdim], km_ref[0:kdim, c0:c0 + SSM_COLS], preferred_element_type=F32)
        y = y + jnp.dot(zp, cm_ref[:, c0:c0 + SSM_COLS], preferred_element_type=F32)
        o_ref[0, :, c0:c0 + SSM_COLS] = jax.nn.gelu(y).astype(BF16)


def _ssm(u8, layer, a_lag, eg, cg, spread, ar, ai):
    nb, rows, _ = u8.shape
    T, GB, H, Q = SSM_CHUNK, SSM_BUNDLE, SSM_GROUP, 2 * SSM_STATE
    par = lambda *dims: pl.BlockSpec((1,) + dims, lambda i: (layer * nb + i,) + (0,) * len(dims))
    act = pl.BlockSpec((1, rows, SSM_BW), lambda i: (i, 0, 0))
    return pl.pallas_call(
        _ssm_kernel,
        out_shape=jax.ShapeDtypeStruct((nb, rows, SSM_BW), BF16),
        grid=(nb,),
        in_specs=[act, par(T, H, LANES), par(GB, T * H, Q), par(GB, Q, T * H),
                  pl.BlockSpec((T * H, SSM_BW), lambda i: (0, 0)), par(8, SSM_SW), par(8, SSM_SW)],
        out_specs=act,
        scratch_shapes=[pltpu.VMEM((SSM_BW, SSM_BW), BF16), pltpu.VMEM((SSM_BW, SSM_SW), BF16),
                        pltpu.VMEM((SSM_SW, SSM_BW), BF16)],
        compiler_params=_params("arbitrary"),
        name="s5_chunked_ssm",
    )(u8, a_lag, eg, cg, spread, ar, ai)


def _cmul(xr, xi, yr, yi):
    return xr * yr - xi * yi, xr * yi + xi * yr


def _ssm_group_kernel(cr_ref, ci_ref, br_ref, bi_ref, pr_ref, pi_ref, qr_ref, qi_ref, d_ref, a_ref, e_ref, c_ref):
    T, H, P = SSM_CHUNK, SSM_GROUP, SSM_STATE
    sh = H.bit_length() - 1
    lane = lax.broadcasted_iota(jnp.int32, (H, LANES), 1)
    row = lax.broadcasted_iota(jnp.int32, (H, LANES), 0)
    per_col = LANES // H
    for nb in range(a_ref.shape[0]):
        lags = [jnp.zeros((H, LANES), F32)] * T
        for g in range(SSM_BUNDLE):
            n = nb * SSM_BUNDLE + g
            cr, ci, br, bi = cr_ref[n], ci_ref[n], br_ref[n], bi_ref[n]
            pr, pi = pr_ref[n][:, None, :], pi_ref[n][:, None, :]
            car = cr[None] * pr - ci[None] * pi
            cai = cr[None] * pi + ci[None] * pr
            ca = jnp.concatenate([car[:T].reshape(T * H, P), cai[:T].reshape(T * H, P)], axis=1)
            bb = jnp.concatenate([br, -bi], axis=1)
            k = lax.dot_general(bb, ca, (((1,), (1,)), ((), ())),
                                precision=lax.Precision.HIGHEST, preferred_element_type=F32)
            for t in range(T):
                col = k[:, (t // per_col) * LANES:(t // per_col + 1) * LANES]
                shift = ((g - t % per_col) * H) % LANES
                moved = pltpu.roll(col, shift, 1) if shift else col
                lags[t] = jnp.where((lane >> sh) == g, moved, lags[t])
            qr, qi = qr_ref[n][:, None, :], qi_ref[n][:, None, :]
            er = (qr * br[None] - qi * bi[None]).reshape(T * H, P)
            ei = (qr * bi[None] + qi * br[None]).reshape(T * H, P)
            e_ref[n] = jnp.concatenate([er, ei], axis=1).astype(BF16)
            c_ref[n] = jnp.concatenate([car[1:].reshape(T * H, P).T, -cai[1:].reshape(T * H, P).T],
                                       axis=0).astype(BF16)
        lags[0] = lags[0] + jnp.where((lane & (H - 1)) == row, d_ref[nb], 0.0)
        for t in range(T):
            a_ref[nb, t] = lags[t].astype(BF16)


PREP_BUNDLES = 1


def _ssm_group_operators(cr, ci, brt, bit, pr, pi, qr, qi, d_lane):
    n = cr.shape[0]
    T, H, P = SSM_CHUNK, SSM_GROUP, SSM_STATE
    ng = PREP_BUNDLES * SSM_BUNDLE
    spec = lambda a, b: pl.BlockSpec((ng, a, b), lambda i: (i, 0, 0))
    return pl.pallas_call(
        _ssm_group_kernel,
        out_shape=(jax.ShapeDtypeStruct((n // SSM_BUNDLE, T, H, LANES), BF16),
                   jax.ShapeDtypeStruct((n, T * H, 2 * P), BF16), jax.ShapeDtypeStruct((n, 2 * P, T * H), BF16)),
        grid=(n // ng,),
        in_specs=[spec(H, P)] * 4 + [spec(T + 1, P)] * 2 + [spec(T, P)] * 2
                 + [pl.BlockSpec((PREP_BUNDLES, 1, LANES), lambda i: (i, 0, 0))],
        out_specs=(pl.BlockSpec((PREP_BUNDLES, T, H, LANES), lambda i: (i, 0, 0, 0)),
                   spec(T * H, 2 * P), spec(2 * P, T * H)),
        compiler_params=_params("parallel"),
        name="ssm_group_operators",
    )(cr, ci, brt, bit, pr, pi, qr, qi, d_lane)


def _ssm_operators(lam_re, lam_im, log_dt, b_re, b_im, c_re, c_im, d):
    lr, li = lam_re.astype(F32), lam_im.astype(F32)
    dt = jnp.exp(log_dt.astype(F32))[..., None]
    mag = jnp.exp(lr * dt)
    ar, ai = mag * jnp.cos(li * dt), mag * jnp.sin(li * dt)
    nr, ni = ar - 1.0, ai
    den = lr * lr + li * li
    zr = (nr * lr + ni * li) / den
    zi = (ni * lr - nr * li) / den
    br, bi = b_re.astype(F32), b_im.astype(F32)
    bbr = zr[..., None] * br - zi[..., None] * bi
    bbi = zr[..., None] * bi + zi[..., None] * br
    cr, ci = c_re.astype(F32), c_im.astype(F32)

    pr, pi = [jnp.ones_like(ar)], [jnp.zeros_like(ar)]
    for _ in range(SSM_CHUNK):
        nr_, ni_ = _cmul(pr[-1], pi[-1], ar, ai)
        pr.append(nr_)
        pi.append(ni_)
    prr, pir = jnp.stack(pr[SSM_CHUNK - 1::-1]), jnp.stack(pi[SSM_CHUNK - 1::-1])
    pr, pi = jnp.stack(pr), jnp.stack(pi)

    T, G, H, P = SSM_CHUNK, SSM_N_GROUPS, SSM_GROUP, SSM_STATE
    NB, GB = SSM_N_BUNDLES, SSM_BUNDLE
    L = lr.shape[0]
    per_group = lambda t: jnp.moveaxis(t, 0, 2).reshape(L * G, t.shape[0], P)
    a_lag, eg, cg = _ssm_group_operators(
        cr.reshape(L * G, H, P), ci.reshape(L * G, H, P),
        bbr.transpose(0, 1, 3, 2).reshape(L * G, H, P), bbi.transpose(0, 1, 3, 2).reshape(L * G, H, P),
        per_group(pr), per_group(pi), per_group(prr), per_group(pir), d.astype(F32).reshape(L * NB, 1, GB * H))
    eg = eg.reshape(L * NB, GB, T * H, 2 * P)
    cg = cg.reshape(L * NB, GB, 2 * P, T * H)

    sr, si = [pr[T]], [pi[T]]
    for _ in range(SSM_SCAN_STEPS - 1):
        nr_, ni_ = _cmul(sr[-1], si[-1], sr[-1], si[-1])
        sr.append(nr_)
        si.append(ni_)
    sr.append(jnp.zeros_like(ar))
    si.append(jnp.zeros_like(ar))
    sr, si = jnp.stack(sr, 2), jnp.stack(si, 2)
    lay = lambda t: t.reshape(L, NB, GB, 8, 2 * P).transpose(0, 1, 3, 2, 4).reshape(L * NB, 8, GB * 2 * P)
    ar_t = lay(jnp.concatenate([sr, sr], -1))
    ai_t = lay(jnp.concatenate([-si, si], -1))

    spread = np.zeros((T, H, T, GB, H), np.float32)
    for i in range(T):
        for h in range(H):
            spread[i, h, i, :, h] = 1.0
    spread = jnp.asarray(spread.reshape(T * H, T * GB * H), BF16)
    return a_lag, eg, cg, spread, ar_t, ai_t


def _deepnorm_ln(x, branch, g, b, alpha):
    y = alpha * x + branch
    mu = y.mean(-1, keepdims=True)
    yc = y - mu
    var = jnp.square(yc).mean(-1, keepdims=True)
    return yc * lax.rsqrt(var + LN_EPS) * g + b


OUT_TM = 512
OUT_SUB = 256


def _outproj_kernel(alpha, ya_ref, yp_ref, ys_ref, glu_ref, w_ref, x_ref, g_ref, b_ref, o_ref, ob_ref, tok_ref):
    for b in range(SSM_N_BUNDLES):
        for j in range(SSM_CHUNK):
            rows = ys_ref[b, :, j * LANES:(j + 1) * LANES].astype(F32)
            tok_ref[b, pl.ds(j, OUT_TM // SSM_CHUNK, stride=SSM_CHUNK), :] = rows
    ys = jnp.concatenate([tok_ref[b].astype(BF16) for b in range(SSM_N_BUNDLES)], axis=1)
    ab = jnp.dot(ys, glu_ref[...].astype(BF16), preferred_element_type=F32)
    y_ssm = (ab[:, :SSM_WIDTH] * jax.nn.sigmoid(ab[:, SSM_WIDTH:])).astype(BF16)
    o_p, o_s = ATTN_WIDTH, ATTN_WIDTH + POOL_WIDTH
    for r in range(0, OUT_TM, OUT_SUB):
        rs = slice(r, r + OUT_SUB)
        acc = jnp.dot(ya_ref[rs, :], w_ref[0:o_p, :], preferred_element_type=F32)
        acc += jnp.dot(yp_ref[rs, :], w_ref[o_p:o_s, :], preferred_element_type=F32)
        acc += jnp.dot(y_ssm[rs, :], w_ref[o_s:, :], preferred_element_type=F32)
        x1 = _deepnorm_ln(x_ref[rs, :], acc, g_ref[...], b_ref[...], alpha)
        o_ref[rs, :] = x1
        ob_ref[rs, :] = x1.astype(BF16)


def _out_proj(ya, yp, ys8, glu_w2d, layer, w_out_bf, x, g, b, alpha):
    m = x.shape[0]
    row = lambda width: pl.BlockSpec((OUT_TM, width), lambda i: (i, 0))
    vec = pl.BlockSpec((1, D_MODEL), lambda i: (0, 0))
    return pl.pallas_call(
        functools.partial(_outproj_kernel, alpha),
        out_shape=(jax.ShapeDtypeStruct((m, D_MODEL), F32), jax.ShapeDtypeStruct((m, D_MODEL), BF16)),
        grid=(m // OUT_TM,),
        in_specs=[
            row(ATTN_WIDTH), row(POOL_WIDTH),
            pl.BlockSpec((SSM_N_BUNDLES, OUT_TM // SSM_CHUNK, SSM_BW), lambda i: (0, i, 0)),
            pl.BlockSpec((SSM_WIDTH, 2 * SSM_WIDTH), lambda i: (layer, 0)),
            _resident((D_MODEL, D_MODEL), lambda i: (0, 0)),
            row(D_MODEL), vec, vec,
        ],
        out_specs=(row(D_MODEL), row(D_MODEL)),
        scratch_shapes=[pltpu.VMEM((SSM_N_BUNDLES, OUT_TM, LANES), F32)],
        compiler_params=_params("parallel"),
        name="out_proj_ln",
    )(ya, yp, ys8, glu_w2d, w_out_bf, x, g, b)


UP_TM = 1024
UP_TN = 512
UP_NJ = -(-D_FF // UP_TN)
FF_PAD = UP_NJ * UP_TN
UP_SHIFT = FF_PAD - D_FF
HALO = 8
CONV_ROWS = 8
DOWN_BLOCK_ROWS = 128
DOWN_BLOCKS = D_FF // DOWN_BLOCK_ROWS


def _ffn_up_kernel(x_ref, wv_ref, wg_ref, cv_ref, cg_ref, wd_ref, o_ref, wdb_ref, wvb_ref, wgb_ref, hv_ref, hg_ref):
    j, i = pl.program_id(0), pl.program_id(1)

    @pl.when(j * pl.num_programs(1) + i < DOWN_BLOCKS)
    def _():
        wdb_ref[...] = wd_ref[...].astype(BF16)

    @pl.when(i == 0)
    def _():
        for r in range(0, D_MODEL, CAST_ROWS):
            wvb_ref[r:r + CAST_ROWS, :] = wv_ref[r:r + CAST_ROWS, :].astype(BF16)
            wgb_ref[r:r + CAST_ROWS, :] = wg_ref[r:r + CAST_ROWS, :].astype(BF16)
        hv_ref[0:HALO, :] = jnp.zeros((HALO, UP_TN), F32)
        hg_ref[0:HALO, :] = jnp.zeros((HALO, UP_TN), F32)

    @pl.when(i != 0)
    def _():
        hv_ref[0:HALO, :] = hv_ref[UP_TM:, :]
        hg_ref[0:HALO, :] = hg_ref[UP_TM:, :]

    x = x_ref[...]
    seq_start = (i % (SEQ // UP_TM)) == 0
    rows = lax.broadcasted_iota(jnp.int32, (HALO, 1), 0)

    def conv(wb_ref, c_ref, h_ref):
        h_ref[HALO:, :] = jnp.dot(x, wb_ref[...], preferred_element_type=F32)

        def tap(shift):
            h = h_ref[HALO - shift:HALO - shift + UP_TM, :]
            head = jnp.where(seq_start & (rows < shift), 0.0, h[:HALO])
            return jnp.concatenate([head, h[HALO:]], axis=0)

        return c_ref[3:4, :] + tap(2) * c_ref[0:1, :] + tap(1) * c_ref[1:2, :] + h_ref[HALO:, :] * c_ref[2:3, :]

    gate = jax.nn.silu(conv(wgb_ref, cg_ref, hg_ref))
    act = (gate * conv(wvb_ref, cv_ref, hv_ref)).astype(BF16)
    last = pl.num_programs(0) - 1

    @pl.when(j != last)
    def _():
        o_ref[...] = act

    @pl.when(j == last)
    def _():
        o_ref[...] = jnp.concatenate([act[:, UP_SHIFT:], jnp.zeros((UP_TM, UP_SHIFT), BF16)], axis=1)


def _ffn_up(x_bf, w_up2d, layer, conv8, w_down4d):
    m = x_bf.shape[0]
    ni = m // UP_TM
    assert UP_NJ * ni >= DOWN_BLOCKS
    col = lambda c0, j: pl.multiple_of(c0 + jnp.minimum(j * UP_TN, D_FF - UP_TN), LANES)
    window = lambda rows, r0, c0: pl.BlockSpec((pl.Element(rows), pl.Element(UP_TN)),
                                               lambda j, i: (r0, col(c0, j)))
    dblk = lambda j, i: jnp.minimum(j * ni + i, DOWN_BLOCKS - 1)
    return pl.pallas_call(
        _ffn_up_kernel,
        out_shape=(jax.ShapeDtypeStruct((m, FF_PAD), BF16),
                   jax.ShapeDtypeStruct((DOWN_BLOCKS, DOWN_BLOCK_ROWS, D_MODEL), BF16)),
        grid=(UP_NJ, ni),
        in_specs=[
            pl.BlockSpec((UP_TM, D_MODEL), lambda j, i: (i, 0)),
            window(D_MODEL, layer * D_MODEL, 0), window(D_MODEL, layer * D_MODEL, D_FF),
            window(CONV_ROWS, layer * CONV_ROWS, 0), window(CONV_ROWS, layer * CONV_ROWS, D_FF),
            pl.BlockSpec((None, None, DOWN_BLOCK_ROWS, D_MODEL), lambda j, i: (layer, dblk(j, i), 0, 0)),
        ],
        out_specs=(pl.BlockSpec((UP_TM, UP_TN), lambda j, i: (i, j)),
                   pl.BlockSpec((None, DOWN_BLOCK_ROWS, D_MODEL), lambda j, i: (dblk(j, i), 0, 0))),
        scratch_shapes=[pltpu.VMEM((D_MODEL, UP_TN), BF16), pltpu.VMEM((D_MODEL, UP_TN), BF16),
                        pltpu.VMEM((HALO + UP_TM, UP_TN), F32), pltpu.VMEM((HALO + UP_TM, UP_TN), F32)],
        compiler_params=_params("arbitrary", "arbitrary"),
        name="ffn_up_conv_gate",
    )(x_bf, w_up2d, w_up2d, conv8, conv8, w_down4d)


DOWN_TM = 512
DOWN_SUB = 256


def _ffn_down_kernel(alpha, a_ref, w_ref, x_ref, g_ref, b_ref, o_ref):
    for r in range(0, DOWN_TM, DOWN_SUB):
        f = jnp.dot(a_ref[r:r + DOWN_SUB, :], w_ref[...], preferred_element_type=F32)
        o_ref[r:r + DOWN_SUB, :] = _deepnorm_ln(x_ref[r:r + DOWN_SUB, :], f, g_ref[...], b_ref[...], alpha)


def _ffn_down(act, w_down_bf, x, g, b, alpha):
    m = x.shape[0]
    vec = pl.BlockSpec((1, D_MODEL), lambda i: (0, 0))
    return pl.pallas_call(
        functools.partial(_ffn_down_kernel, alpha),
        out_shape=jax.ShapeDtypeStruct((m, D_MODEL), F32),
        grid=(m // DOWN_TM,),
        in_specs=[
            pl.BlockSpec((DOWN_TM, D_FF), lambda i: (i, 0)),
            _resident((D_FF, D_MODEL), lambda i: (0, 0)),
            pl.BlockSpec((DOWN_TM, D_MODEL), lambda i: (i, 0)),
            vec, vec,
        ],
        out_specs=pl.BlockSpec((DOWN_TM, D_MODEL), lambda i: (i, 0)),
        compiler_params=_params("parallel"),
        name="ffn_down_ln",
    )(act, w_down_bf, x, g, b)


def kernel(x, w_in, attn_sinks, pool_w, pool_scale, ssm_lam_re, ssm_lam_im, ssm_log_dt, ssm_b_re, ssm_b_im,
           ssm_c_re, ssm_c_im, ssm_d, ssm_glu_w, w_out, ln1_g, ln1_b, ffn_w_up, ffn_conv_w, ffn_conv_b,
           ffn_w_down, ln2_g, ln2_b):
    bsz, s_len, _ = x.shape
    assert s_len == SEQ and x.shape[2] == D_MODEL
    depth = w_in.shape[0]
    alpha = (2 * depth) ** 0.25
    m = bsz * s_len
    cos_t, sa_t, sb_t = _rope_tables()
    xf = x.reshape(m, D_MODEL).astype(F32)

    w_out2d = w_out.astype(F32).reshape(depth * D_MODEL, D_MODEL)
    w_up2d = ffn_w_up.astype(F32).reshape(depth * D_MODEL, 2 * D_FF)
    w_down4d = ffn_w_down.astype(F32).reshape(depth, DOWN_BLOCKS, DOWN_BLOCK_ROWS, D_MODEL)
    conv8 = jnp.concatenate([ffn_conv_w.astype(F32), ffn_conv_b.astype(F32)[:, None, :],
                             jnp.zeros((depth, CONV_ROWS - 4, 2 * D_FF), F32)], axis=1)
    conv8 = conv8.reshape(depth * CONV_ROWS, 2 * D_FF)
    w_in2d = w_in.astype(F32).reshape(depth * D_MODEL, 2 * HALF_WIDTH)
    glu_w2d = ssm_glu_w.astype(F32).reshape(depth * SSM_WIDTH, 2 * SSM_WIDTH)
    pool_w3d = pool_w.astype(F32).reshape(depth * len(POOL_WINDOWS), POOL_GROUP, POOL_GROUP)
    vec = lambda a: a.astype(F32).reshape(1, -1)
    ssm_ops = _ssm_operators(ssm_lam_re, ssm_lam_im, ssm_log_dt, ssm_b_re, ssm_b_im, ssm_c_re, ssm_c_im, ssm_d)

    for l in range(depth):
        q, k2, v2, pu, su, w_out_bf = _in_proj(xf, w_in2d, w_out2d, l, cos_t, sa_t, sb_t)
        y_attn = _attention(q, k2, v2, attn_sinks[l].astype(F32))
        y_pool = _pool(pu, pool_w3d, l, vec(pool_scale[l]))

        y8 = _ssm(su, l, *ssm_ops)

        x1, x1_bf = _out_proj(y_attn, y_pool, y8, glu_w2d, l, w_out_bf, xf,
                              vec(ln1_g[l]), vec(ln1_b[l]), alpha)
        act, w_down_bf = _ffn_up(x1_bf, w_up2d, l, conv8, w_down4d)
        xf = _ffn_down(act, w_down_bf.reshape(D_FF, D_MODEL), x1, vec(ln2_g[l]), vec(ln2_b[l]), alpha)

    return xf.reshape(bsz, s_len, D_MODEL).astype(x.dtype)
```

```python
import functools

import jax
import numpy as np
import jax.numpy as jnp
from jax import lax
from jax.experimental import pallas as pl
from jax.experimental.pallas import tpu as pltpu

F32 = jnp.float32
BF16 = jnp.bfloat16

D_MODEL = 2048
SEQ = 2048
HEAD_DIM = 64
N_Q_HEADS = 16
N_KV_HEADS = 4
ATTN_WIDTH = N_Q_HEADS * HEAD_DIM
KV_WIDTH = N_KV_HEADS * HEAD_DIM
ATTN_BLOCK = 128
ROPE_THETA = 10000.0
POOL_WINDOWS = (2, 4, 8, 16)
POOL_GROUP = 128
POOL_WIDTH = 512
SSM_WIDTH = 512
SSM_GROUP = 16
SSM_N_GROUPS = 32
SSM_STATE = 64
SSM_CHUNK = 16
LN_EPS = 1e-5
D_FF = 5504

LANES = 128
SSM_BUNDLE = LANES // SSM_GROUP
SSM_N_BUNDLES = SSM_N_GROUPS // SSM_BUNDLE
SSM_BW = SSM_CHUNK * LANES
SSM_SW = SSM_BUNDLE * 2 * SSM_STATE
HALF_WIDTH = ATTN_WIDTH + KV_WIDTH
KV2_WIDTH = 2 * KV_WIDTH
VMEM_LIMIT = 56 * 1024 * 1024


def _params(*sem):
    return pltpu.CompilerParams(dimension_semantics=sem, vmem_limit_bytes=VMEM_LIMIT)


def _resident(block, index_map):
    return pl.BlockSpec(block, index_map, pipeline_mode=pl.Buffered(1))


IN_TM = 512
CAST_ROWS = 256


def _cast_weight_once(w_ref, wbf_ref):
    @pl.when(pl.program_id(0) == 0)
    def _():
        for r in range(0, w_ref.shape[0], CAST_ROWS):
            wbf_ref[r:r + CAST_ROWS, :] = w_ref[r:r + CAST_ROWS, :].astype(BF16)


def _dup_heads(pair):
    lo = lax.broadcasted_iota(jnp.int32, pair.shape, 1) < HEAD_DIM
    swapped = pltpu.roll(pair, HEAD_DIM, 1)
    return jnp.where(lo, pair, swapped), jnp.where(lo, swapped, pair)


def _inproj_kernel(x_ref, wa_ref, wb_ref, cos_ref, sa_ref, sb_ref, wo_ref, q_ref, k_ref, v_ref, p_ref, s_ref,
                   wobf_ref, wabf_ref, wbbf_ref, tok_ref):
    _cast_weight_once(wa_ref, wabf_ref)
    _cast_weight_once(wb_ref, wbbf_ref)
    wobf_ref[...] = wo_ref[...].astype(BF16)
    xb = x_ref[...].astype(BF16)
    acc = jnp.dot(xb, wabf_ref[...], preferred_element_type=F32)
    cos, sa, sb = cos_ref[...], sa_ref[...], sb_ref[...]
    nq = ATTN_WIDTH // LANES
    for c in range(HALF_WIDTH // LANES):
        a = acc[:, c * LANES:(c + 1) * LANES]
        r = a * cos + pltpu.roll(a, LANES - 32, 1) * sa + pltpu.roll(a, 32, 1) * sb
        if c < nq:
            q_ref[:, c * LANES:(c + 1) * LANES] = (r * (HEAD_DIM ** -0.5)).astype(BF16)
        else:
            ka, kb = _dup_heads(r)
            g = 2 * (c - nq)
            k_ref[:, g * LANES:(g + 1) * LANES] = ka.astype(BF16)
            k_ref[:, (g + 1) * LANES:(g + 2) * LANES] = kb.astype(BF16)

    acc = jnp.dot(xb, wbbf_ref[...], preferred_element_type=F32)
    for c in range(KV_WIDTH // LANES):
        va, vb = _dup_heads(acc[:, c * LANES:(c + 1) * LANES])
        v_ref[:, 2 * c * LANES:(2 * c + 1) * LANES] = va.astype(BF16)
        v_ref[:, (2 * c + 1) * LANES:(2 * c + 2) * LANES] = vb.astype(BF16)
    p_ref[...] = acc[:, KV_WIDTH:KV_WIDTH + POOL_WIDTH].astype(BF16)
    for b in range(SSM_N_BUNDLES):
        c0 = KV_WIDTH + POOL_WIDTH + b * LANES
        tok_ref[b] = acc[:, c0:c0 + LANES]
        for j in range(SSM_CHUNK):
            rows = tok_ref[b, pl.ds(j, IN_TM // SSM_CHUNK, stride=SSM_CHUNK), :]
            s_ref[b, :, j * LANES:(j + 1) * LANES] = rows.astype(BF16)


def _in_proj(x, w_in2d, w_out2d, layer, cos, sa, sb):
    m = x.shape[0]
    nseq = SEQ // IN_TM
    x_spec = pl.BlockSpec((IN_TM, D_MODEL), lambda i: (i, 0))
    tab = pl.BlockSpec((IN_TM, LANES), lambda i: (i % nseq, 0))
    row = lambda width: pl.BlockSpec((IN_TM, width), lambda i: (i, 0))
    steps = m // IN_TM
    wo_rows = D_MODEL // steps
    return pl.pallas_call(
        _inproj_kernel,
        out_shape=(jax.ShapeDtypeStruct((m, ATTN_WIDTH), BF16), jax.ShapeDtypeStruct((m, KV2_WIDTH), BF16),
                   jax.ShapeDtypeStruct((m, KV2_WIDTH), BF16), jax.ShapeDtypeStruct((m, POOL_WIDTH), BF16),
                   jax.ShapeDtypeStruct((SSM_N_BUNDLES, m // SSM_CHUNK, SSM_BW), BF16),
                   jax.ShapeDtypeStruct((D_MODEL, D_MODEL), BF16)),
        grid=(steps,),
        in_specs=[x_spec, _resident((D_MODEL, HALF_WIDTH), lambda i: (layer, 0)),
                  _resident((D_MODEL, HALF_WIDTH), lambda i: (layer, 1)), tab, tab, tab,
                  pl.BlockSpec((wo_rows, D_MODEL), lambda i: (layer * steps + i, 0))],
        out_specs=(row(ATTN_WIDTH), row(KV2_WIDTH), row(KV2_WIDTH), row(POOL_WIDTH),
                   pl.BlockSpec((SSM_N_BUNDLES, IN_TM // SSM_CHUNK, SSM_BW), lambda i: (0, i, 0)),
                   pl.BlockSpec((wo_rows, D_MODEL), lambda i: (i, 0))),
        scratch_shapes=[pltpu.VMEM((D_MODEL, HALF_WIDTH), BF16), pltpu.VMEM((D_MODEL, HALF_WIDTH), BF16),
                        pltpu.VMEM((SSM_N_BUNDLES, IN_TM, LANES), F32)],
        compiler_params=_params("arbitrary"),
        name="in_proj",
    )(x, w_in2d, w_in2d, cos, sa, sb, w_out2d)


def _rope_tables():
    half = HEAD_DIM // 2
    inv = ROPE_THETA ** (-jnp.arange(half, dtype=F32) / half)
    ang = jnp.arange(SEQ, dtype=F32)[:, None] * inv[None, :]
    cos, sin = jnp.cos(ang), jnp.sin(ang)
    zero = jnp.zeros_like(sin)
    reps = LANES // HEAD_DIM
    cos_t = jnp.tile(jnp.concatenate([cos, cos], -1), (1, reps))
    sa_t = jnp.tile(jnp.concatenate([-sin, zero], -1), (1, reps))
    sb_t = jnp.tile(jnp.concatenate([zero, sin], -1), (1, reps))
    return cos_t, sa_t, sb_t


ATTN_TQ = 1024
ATTN_QB = ATTN_TQ // ATTN_BLOCK
KEYS = 2 * ATTN_BLOCK


def _attn_kernel(sink_ref, q_ref, kc_ref, vc_ref, kp_ref, vp_ref, o_ref, kbuf, vbuf):
    kbuf[0:ATTN_BLOCK, :] = kp_ref[...]
    kbuf[ATTN_BLOCK:, :] = kc_ref[...]
    vbuf[0:ATTN_BLOCK, :] = vp_ref[...]
    vbuf[ATTN_BLOCK:, :] = vc_ref[...]
    seq_start = pl.program_id(1) == 0

    row = lax.broadcasted_iota(jnp.int32, (ATTN_BLOCK, KEYS), 0)
    col = lax.broadcasted_iota(jnp.int32, (ATTN_BLOCK, KEYS), 1)
    dist = row + ATTN_BLOCK - col
    band = (dist >= 0) & (dist < ATTN_BLOCK)
    band_first = band & (jnp.logical_not(seq_start) | (col >= ATTN_BLOCK))
    lo_kv = lax.broadcasted_iota(jnp.int32, (KEYS, LANES), 1) < HEAD_DIM
    lo_out = lax.broadcasted_iota(jnp.int32, (ATTN_BLOCK, LANES), 1) < HEAD_DIM
    zero_kv = jnp.zeros((KEYS, LANES), BF16)

    def split_heads(x):
        return jnp.concatenate([jnp.where(lo_kv, x, zero_kv), jnp.where(lo_kv, zero_kv, x)], axis=0)

    for qb in range(ATTN_QB):
        valid = band_first if qb == 0 else band
        r0 = qb * ATTN_BLOCK
        for g in range(N_KV_HEADS):
            k2 = split_heads(kbuf[r0:r0 + KEYS, g * LANES:(g + 1) * LANES])
            v2 = split_heads(vbuf[r0:r0 + KEYS, g * LANES:(g + 1) * LANES])
            c0 = 2 * g * LANES
            qq = jnp.concatenate([q_ref[r0:r0 + ATTN_BLOCK, c0:c0 + LANES],
                                  q_ref[r0:r0 + ATTN_BLOCK, c0 + LANES:c0 + 2 * LANES]], axis=0)
            s4 = lax.dot_general(qq, k2, (((1,), (1,)), ((), ())), preferred_element_type=F32)
            p_rows, dens = [], []
            for pr in range(2):
                ps = []
                for hh in range(2):
                    sink = sink_ref[4 * g + 2 * pr + hh]
                    s = jnp.where(valid, s4[pr * ATTN_BLOCK:(pr + 1) * ATTN_BLOCK, hh * KEYS:(hh + 1) * KEYS], -1e30)
                    mx = jnp.maximum(s.max(-1, keepdims=True), sink)
                    p = jnp.exp(s - mx)
                    dens.append(p.sum(-1, keepdims=True) + jnp.exp(sink - mx))
                    ps.append(p.astype(BF16))
                p_rows.append(jnp.concatenate(ps, axis=1))
            o4 = jnp.dot(jnp.concatenate(p_rows, axis=0), v2, preferred_element_type=F32)
            for pr in range(2):
                o = o4[pr * ATTN_BLOCK:(pr + 1) * ATTN_BLOCK] / jnp.where(lo_out, dens[2 * pr], dens[2 * pr + 1])
                o_ref[r0:r0 + ATTN_BLOCK, c0 + pr * LANES:c0 + (pr + 1) * LANES] = o.astype(BF16)


def _attention(q, k2, v2, sinks):
    m = q.shape[0]
    nq = SEQ // ATTN_TQ
    cur = lambda b, i: (b * nq + i, 0)
    prev = lambda b, i: (jnp.maximum((b * nq + i) * ATTN_QB - 1, 0), 0)
    return pl.pallas_call(
        _attn_kernel,
        out_shape=jax.ShapeDtypeStruct((m, ATTN_WIDTH), BF16),
        grid=(m // SEQ, nq),
        in_specs=[
            pl.BlockSpec(memory_space=pltpu.SMEM),
            pl.BlockSpec((ATTN_TQ, ATTN_WIDTH), cur),
            pl.BlockSpec((ATTN_TQ, KV2_WIDTH), cur),
            pl.BlockSpec((ATTN_TQ, KV2_WIDTH), cur),
            pl.BlockSpec((ATTN_BLOCK, KV2_WIDTH), prev),
            pl.BlockSpec((ATTN_BLOCK, KV2_WIDTH), prev),
        ],
        out_specs=pl.BlockSpec((ATTN_TQ, ATTN_WIDTH), cur),
        scratch_shapes=[pltpu.VMEM((ATTN_TQ + ATTN_BLOCK, KV2_WIDTH), BF16),
                        pltpu.VMEM((ATTN_TQ + ATTN_BLOCK, KV2_WIDTH), BF16)],
        compiler_params=_params("parallel", "arbitrary"),
        name="swa_attention",
    )(sinks, q, k2, v2, k2, v2)


def _shift_rows(x, d, rows):
    return jnp.where(rows >= d, pltpu.roll(x, d, 0), 0.0)


def _pool_kernel(u_ref, w_ref, scale_ref, o_ref):
    rows = lax.broadcasted_iota(jnp.int32, (SEQ, POOL_GROUP), 0)
    t1 = lax.broadcasted_iota(jnp.int32, (SEQ, 1), 0).astype(F32) + 1.0
    for idx, w in enumerate(POOL_WINDOWS):
        cols = slice(idx * POOL_GROUP, (idx + 1) * POOL_GROUP)
        u = u_ref[:, cols].astype(F32)
        s, d = u, 1
        while d < w:
            s = s + _shift_rows(s, d, rows)
            d *= 2
        mean = s / jnp.minimum(t1, float(w))
        y = jnp.dot((mean - u).astype(BF16), w_ref[idx].astype(BF16), preferred_element_type=F32)
        o_ref[:, cols] = (y * scale_ref[:, cols]).astype(BF16)


def _pool(pu, pool_w3d, layer, pool_scale_l):
    m = pu.shape[0]
    nw = len(POOL_WINDOWS)
    return pl.pallas_call(
        _pool_kernel,
        out_shape=jax.ShapeDtypeStruct((m, POOL_WIDTH), BF16),
        grid=(m // SEQ,),
        in_specs=[
            pl.BlockSpec((SEQ, POOL_WIDTH), lambda b: (b, 0)),
            pl.BlockSpec((nw, POOL_GROUP, POOL_GROUP), lambda b: (layer, 0, 0)),
            pl.BlockSpec((1, POOL_WIDTH), lambda b: (0, 0)),
        ],
        out_specs=pl.BlockSpec((SEQ, POOL_WIDTH), lambda b: (b, 0)),
        compiler_params=_params("parallel"),
        name="multiscale_pool",
    )(pu, pool_w3d, pool_scale_l)


SSM_SCAN_STEPS = 7
SSM_COLS = 2 * LANES


def _ssm_kernel(u_ref, a_ref, eg_ref, cg_ref, x_ref, ar_ref, ai_ref, o_ref, km_ref, em_ref, cm_ref):
    T, GB, H = SSM_CHUNK, SSM_BUNDLE, SSM_GROUP
    sh = H.bit_length() - 1

    @pl.when(pl.program_id(0) == 0)
    def _():
        em_ref[...] = jnp.zeros_like(em_ref)

    row_g = lax.broadcasted_iota(jnp.int32, (LANES, LANES), 0) >> sh
    col_g = lax.broadcasted_iota(jnp.int32, (LANES, LANES), 1) >> sh
    zero_blk = jnp.zeros((LANES, LANES), BF16)
    lag_blk = [jnp.where(row_g == col_g, jnp.concatenate([a_ref[0, t]] * GB, axis=0), zero_blk) for t in range(T)]
    for i in range(T):
        for j in range(min(i | 1, T - 1) + 1):
            km_ref[j * LANES:(j + 1) * LANES, i * LANES:(i + 1) * LANES] = lag_blk[i - j] if j <= i else zero_blk

    for j in range(T):
        for g in range(GB):
            r0 = j * LANES + g * H
            em_ref[r0:r0 + H, g * LANES:(g + 1) * LANES] = eg_ref[0, g, j * H:(j + 1) * H, :]

    lane_g = (lax.broadcasted_iota(jnp.int32, (LANES, SSM_BW), 1) & (LANES - 1)) >> sh
    for g in range(GB):
        spread = jnp.dot(cg_ref[0, g], x_ref[...], preferred_element_type=F32)
        cm_ref[g * LANES:(g + 1) * LANES, :] = jnp.where(lane_g == g, spread, 0.0).astype(BF16)

    u = u_ref[0]
    e = jnp.dot(u, em_ref[...], preferred_element_type=F32)
    nchunk = SEQ // SSM_CHUNK
    rows = lax.broadcasted_iota(jnp.int32, (e.shape[0], LANES), 0) % nchunk
    zprev = []
    for g in range(SSM_BUNDLE):
        eg = e[:, g * LANES:(g + 1) * LANES]
        for k in range(SSM_SCAN_STEPS):
            s = _shift_rows(eg, 1 << k, rows)
            sl = slice(g * LANES, (g + 1) * LANES)
            eg = eg + ar_ref[0, k:k + 1, sl] * s + ai_ref[0, k:k + 1, sl] * pltpu.roll(s, SSM_STATE, 1)
        zprev.append(_shift_rows(eg, 1, rows).astype(BF16))
    zp = jnp.concatenate(zprev, axis=1)

    for mblk in range(SSM_BW // SSM_COLS):
        c0, kdim = mblk * SSM_COLS, (mblk + 1) * SSM_COLS
        y = jnp.dot(u[:, :kdim], km_ref[0:kdim, c0:c0 + SSM_COLS], preferred_element_type=F32)
        y = y + jnp.dot(zp, cm_ref[:, c0:c0 + SSM_COLS], preferred_element_type=F32)
        o_ref[0, :, c0:c0 + SSM_COLS] = jax.nn.gelu(y).astype(BF16)


def _ssm(u8, layer, a_lag, eg, cg, spread, ar, ai):
    nb, rows, _ = u8.shape
    T, GB, H, Q = SSM_CHUNK, SSM_BUNDLE, SSM_GROUP, 2 * SSM_STATE
    par = lambda *dims: pl.BlockSpec((1,) + dims, lambda i: (layer * nb + i,) + (0,) * len(dims))
    act = pl.BlockSpec((1, rows, SSM_BW), lambda i: (i, 0, 0))
    return pl.pallas_call(
        _ssm_kernel,
        out_shape=jax.ShapeDtypeStruct((nb, rows, SSM_BW), BF16),
        grid=(nb,),
        in_specs=[act, par(T, H, LANES), par(GB, T * H, Q), par(GB, Q, T * H),
                  pl.BlockSpec((T * H, SSM_BW), lambda i: (0, 0)), par(8, SSM_SW), par(8, SSM_SW)],
        out_specs=act,
        scratch_shapes=[pltpu.VMEM((SSM_BW, SSM_BW), BF16), pltpu.VMEM((SSM_BW, SSM_SW), BF16),
                        pltpu.VMEM((SSM_SW, SSM_BW), BF16)],
        compiler_params=_params("arbitrary"),
        name="s5_chunked_ssm",
    )(u8, a_lag, eg, cg, spread, ar, ai)


def _cmul(xr, xi, yr, yi):
    return xr * yr - xi * yi, xr * yi + xi * yr


def _ssm_group_kernel(cr_ref, ci_ref, br_ref, bi_ref, pr_ref, pi_ref, qr_ref, qi_ref, d_ref, a_ref, e_ref, c_ref):
    T, H, P = SSM_CHUNK, SSM_GROUP, SSM_STATE
    sh = H.bit_length() - 1
    lane = lax.broadcasted_iota(jnp.int32, (H, LANES), 1)
    row = lax.broadcasted_iota(jnp.int32, (H, LANES), 0)
    per_col = LANES // H
    for nb in range(a_ref.shape[0]):
        lags = [jnp.zeros((H, LANES), F32)] * T
        for g in range(SSM_BUNDLE):
            n = nb * SSM_BUNDLE + g
            cr, ci, br, bi = cr_ref[n], ci_ref[n], br_ref[n], bi_ref[n]
            pr, pi = pr_ref[n][:, None, :], pi_ref[n][:, None, :]
            car = cr[None] * pr - ci[None] * pi
            cai = cr[None] * pi + ci[None] * pr
            ca = jnp.concatenate([car[:T].reshape(T * H, P), cai[:T].reshape(T * H, P)], axis=1)
            bb = jnp.concatenate([br, -bi], axis=1)
            k = lax.dot_general(bb, ca, (((1,), (1,)), ((), ())),
                                precision=lax.Precision.HIGHEST, preferred_element_type=F32)
            for t in range(T):
                col = k[:, (t // per_col) * LANES:(t // per_col + 1) * LANES]
                shift = ((g - t % per_col) * H) % LANES
                moved = pltpu.roll(col, shift, 1) if shift else col
                lags[t] = jnp.where((lane >> sh) == g, moved, lags[t])
            qr, qi = qr_ref[n][:, None, :], qi_ref[n][:, None, :]
            er = (qr * br[None] - qi * bi[None]).reshape(T * H, P)
            ei = (qr * bi[None] + qi * br[None]).reshape(T * H, P)
            e_ref[n] = jnp.concatenate([er, ei], axis=1).astype(BF16)
            c_ref[n] = jnp.concatenate([car[1:].reshape(T * H, P).T, -cai[1:].reshape(T * H, P).T],
                                       axis=0).astype(BF16)
        lags[0] = lags[0] + jnp.where((lane & (H - 1)) == row, d_ref[nb], 0.0)
        for t in range(T):
            a_ref[nb, t] = lags[t].astype(BF16)


PREP_BUNDLES = 1


def _ssm_group_operators(cr, ci, brt, bit, pr, pi, qr, qi, d_lane):
    n = cr.shape[0]
    T, H, P = SSM_CHUNK, SSM_GROUP, SSM_STATE
    ng = PREP_BUNDLES * SSM_BUNDLE
    spec = lambda a, b: pl.BlockSpec((ng, a, b), lambda i: (i, 0, 0))
    return pl.pallas_call(
        _ssm_group_kernel,
        out_shape=(jax.ShapeDtypeStruct((n // SSM_BUNDLE, T, H, LANES), BF16),
                   jax.ShapeDtypeStruct((n, T * H, 2 * P), BF16), jax.ShapeDtypeStruct((n, 2 * P, T * H), BF16)),
        grid=(n // ng,),
        in_specs=[spec(H, P)] * 4 + [spec(T + 1, P)] * 2 + [spec(T, P)] * 2
                 + [pl.BlockSpec((PREP_BUNDLES, 1, LANES), lambda i: (i, 0, 0))],
        out_specs=(pl.BlockSpec((PREP_BUNDLES, T, H, LANES), lambda i: (i, 0, 0, 0)),
                   spec(T * H, 2 * P), spec(2 * P, T * H)),
        compiler_params=_params("parallel"),
        name="ssm_group_operators",
    )(cr, ci, brt, bit, pr, pi, qr, qi, d_lane)


def _ssm_operators(lam_re, lam_im, log_dt, b_re, b_im, c_re, c_im, d):
    lr, li = lam_re.astype(F32), lam_im.astype(F32)
    dt = jnp.exp(log_dt.astype(F32))[..., None]
    mag = jnp.exp(lr * dt)
    ar, ai = mag * jnp.cos(li * dt), mag * jnp.sin(li * dt)
    nr, ni = ar - 1.0, ai
    den = lr * lr + li * li
    zr = (nr * lr + ni * li) / den
    zi = (ni * lr - nr * li) / den
    br, bi = b_re.astype(F32), b_im.astype(F32)
    bbr = zr[..., None] * br - zi[..., None] * bi
    bbi = zr[..., None] * bi + zi[..., None] * br
    cr, ci = c_re.astype(F32), c_im.astype(F32)

    pr, pi = [jnp.ones_like(ar)], [jnp.zeros_like(ar)]
    for _ in range(SSM_CHUNK):
        nr_, ni_ = _cmul(pr[-1], pi[-1], ar, ai)
        pr.append(nr_)
        pi.append(ni_)
    prr, pir = jnp.stack(pr[SSM_CHUNK - 1::-1]), jnp.stack(pi[SSM_CHUNK - 1::-1])
    pr, pi = jnp.stack(pr), jnp.stack(pi)

    T, G, H, P = SSM_CHUNK, SSM_N_GROUPS, SSM_GROUP, SSM_STATE
    NB, GB = SSM_N_BUNDLES, SSM_BUNDLE
    L = lr.shape[0]
    per_group = lambda t: jnp.moveaxis(t, 0, 2).reshape(L * G, t.shape[0], P)
    a_lag, eg, cg = _ssm_group_operators(
        cr.reshape(L * G, H, P), ci.reshape(L * G, H, P),
        bbr.transpose(0, 1, 3, 2).reshape(L * G, H, P), bbi.transpose(0, 1, 3, 2).reshape(L * G, H, P),
        per_group(pr), per_group(pi), per_group(prr), per_group(pir), d.astype(F32).reshape(L * NB, 1, GB * H))
    eg = eg.reshape(L * NB, GB, T * H, 2 * P)
    cg = cg.reshape(L * NB, GB, 2 * P, T * H)

    sr, si = [pr[T]], [pi[T]]
    for _ in range(SSM_SCAN_STEPS - 1):
        nr_, ni_ = _cmul(sr[-1], si[-1], sr[-1], si[-1])
        sr.append(nr_)
        si.append(ni_)
    sr.append(jnp.zeros_like(ar))
    si.append(jnp.zeros_like(ar))
    sr, si = jnp.stack(sr, 2), jnp.stack(si, 2)
    lay = lambda t: t.reshape(L, NB, GB, 8, 2 * P).transpose(0, 1, 3, 2, 4).reshape(L * NB, 8, GB * 2 * P)
    ar_t = lay(jnp.concatenate([sr, sr], -1))
    ai_t = lay(jnp.concatenate([-si, si], -1))

    spread = np.zeros((T, H, T, GB, H), np.float32)
    for i in range(T):
        for h in range(H):
            spread[i, h, i, :, h] = 1.0
    spread = jnp.asarray(spread.reshape(T * H, T * GB * H), BF16)
    return a_lag, eg, cg, spread, ar_t, ai_t


def _deepnorm_ln(x, branch, g, b, alpha):
    y = alpha * x + branch
    mu = y.mean(-1, keepdims=True)
    yc = y - mu
    var = jnp.square(yc).mean(-1, keepdims=True)
    return yc * lax.rsqrt(var + LN_EPS) * g + b


OUT_TM = 512
OUT_SUB = 256


def _outproj_kernel(alpha, ya_ref, yp_ref, ys_ref, glu_ref, w_ref, x_ref, g_ref, b_ref, o_ref, ob_ref, tok_ref):
    for b in range(SSM_N_BUNDLES):
        for j in range(SSM_CHUNK):
            rows = ys_ref[b, :, j * LANES:(j + 1) * LANES].astype(F32)
            tok_ref[b, pl.ds(j, OUT_TM // SSM_CHUNK, stride=SSM_CHUNK), :] = rows
    ys = jnp.concatenate([tok_ref[b].astype(BF16) for b in range(SSM_N_BUNDLES)], axis=1)
    ab = jnp.dot(ys, glu_ref[...].astype(BF16), preferred_element_type=F32)
    y_ssm = (ab[:, :SSM_WIDTH] * jax.nn.sigmoid(ab[:, SSM_WIDTH:])).astype(BF16)
    o_p, o_s = ATTN_WIDTH, ATTN_WIDTH + POOL_WIDTH
    for r in range(0, OUT_TM, OUT_SUB):
        rs = slice(r, r + OUT_SUB)
        acc = jnp.dot(ya_ref[rs, :], w_ref[0:o_p, :], preferred_element_type=F32)
        acc += jnp.dot(yp_ref[rs, :], w_ref[o_p:o_s, :], preferred_element_type=F32)
        acc += jnp.dot(y_ssm[rs, :], w_ref[o_s:, :], preferred_element_type=F32)
        x1 = _deepnorm_ln(x_ref[rs, :], acc, g_ref[...], b_ref[...], alpha)
        o_ref[rs, :] = x1
        ob_ref[rs, :] = x1.astype(BF16)


def _out_proj(ya, yp, ys8, glu_w2d, layer, w_out_bf, x, g, b, alpha):
    m = x.shape[0]
    row = lambda width: pl.BlockSpec((OUT_TM, width), lambda i: (i, 0))
    vec = pl.BlockSpec((1, D_MODEL), lambda i: (0, 0))
    return pl.pallas_call(
        functools.partial(_outproj_kernel, alpha),
        out_shape=(jax.ShapeDtypeStruct((m, D_MODEL), F32), jax.ShapeDtypeStruct((m, D_MODEL), BF16)),
        grid=(m // OUT_TM,),
        in_specs=[
            row(ATTN_WIDTH), row(POOL_WIDTH),
            pl.BlockSpec((SSM_N_BUNDLES, OUT_TM // SSM_CHUNK, SSM_BW), lambda i: (0, i, 0)),
            pl.BlockSpec((SSM_WIDTH, 2 * SSM_WIDTH), lambda i: (layer, 0)),
            _resident((D_MODEL, D_MODEL), lambda i: (0, 0)),
            row(D_MODEL), vec, vec,
        ],
        out_specs=(row(D_MODEL), row(D_MODEL)),
        scratch_shapes=[pltpu.VMEM((SSM_N_BUNDLES, OUT_TM, LANES), F32)],
        compiler_params=_params("parallel"),
        name="out_proj_ln",
    )(ya, yp, ys8, glu_w2d, w_out_bf, x, g, b)


UP_TM = 1024
UP_TN = 512
UP_NJ = -(-D_FF // UP_TN)
FF_PAD = UP_NJ * UP_TN
UP_SHIFT = FF_PAD - D_FF
HALO = 8
CONV_ROWS = 8
DOWN_BLOCK_ROWS = 128
DOWN_BLOCKS = D_FF // DOWN_BLOCK_ROWS


def _ffn_up_kernel(x_ref, wv_ref, wg_ref, cv_ref, cg_ref, wd_ref, o_ref, wdb_ref, wvb_ref, wgb_ref, hv_ref, hg_ref):
    j, i = pl.program_id(0), pl.program_id(1)

    @pl.when(j * pl.num_programs(1) + i < DOWN_BLOCKS)
    def _():
        wdb_ref[...] = wd_ref[...].astype(BF16)

    @pl.when(i == 0)
    def _():
        for r in range(0, D_MODEL, CAST_ROWS):
            wvb_ref[r:r + CAST_ROWS, :] = wv_ref[r:r + CAST_ROWS, :].astype(BF16)
            wgb_ref[r:r + CAST_ROWS, :] = wg_ref[r:r + CAST_ROWS, :].astype(BF16)
        hv_ref[0:HALO, :] = jnp.zeros((HALO, UP_TN), F32)
        hg_ref[0:HALO, :] = jnp.zeros((HALO, UP_TN), F32)

    @pl.when(i != 0)
    def _():
        hv_ref[0:HALO, :] = hv_ref[UP_TM:, :]
        hg_ref[0:HALO, :] = hg_ref[UP_TM:, :]

    x = x_ref[...]
    seq_start = (i % (SEQ // UP_TM)) == 0
    rows = lax.broadcasted_iota(jnp.int32, (HALO, 1), 0)

    def conv(wb_ref, c_ref, h_ref):
        h_ref[HALO:, :] = jnp.dot(x, wb_ref[...], preferred_element_type=F32)

        def tap(shift):
            h = h_ref[HALO - shift:HALO - shift + UP_TM, :]
            head = jnp.where(seq_start & (rows < shift), 0.0, h[:HALO])
            return jnp.concatenate([head, h[HALO:]], axis=0)

        return c_ref[3:4, :] + tap(2) * c_ref[0:1, :] + tap(1) * c_ref[1:2, :] + h_ref[HALO:, :] * c_ref[2:3, :]

    gate = jax.nn.silu(conv(wgb_ref, cg_ref, hg_ref))
    act = (gate * conv(wvb_ref, cv_ref, hv_ref)).astype(BF16)
    last = pl.num_programs(0) - 1

    @pl.when(j != last)
    def _():
        o_ref[...] = act

    @pl.when(j == last)
    def _():
        o_ref[...] = jnp.concatenate([act[:, UP_SHIFT:], jnp.zeros((UP_TM, UP_SHIFT), BF16)], axis=1)


def _ffn_up(x_bf, w_up2d, layer, conv8, w_down4d):
    m = x_bf.shape[0]
    ni = m // UP_TM
    assert UP_NJ * ni >= DOWN_BLOCKS
    col = lambda c0, j: pl.multiple_of(c0 + jnp.minimum(j * UP_TN, D_FF - UP_TN), LANES)
    window = lambda rows, r0, c0: pl.BlockSpec((pl.Element(rows), pl.Element(UP_TN)),
                                               lambda j, i: (r0, col(c0, j)))
    dblk = lambda j, i: jnp.minimum(j * ni + i, DOWN_BLOCKS - 1)
    return pl.pallas_call(
        _ffn_up_kernel,
        out_shape=(jax.ShapeDtypeStruct((m, FF_PAD), BF16),
                   jax.ShapeDtypeStruct((DOWN_BLOCKS, DOWN_BLOCK_ROWS, D_MODEL), BF16)),
        grid=(UP_NJ, ni),
        in_specs=[
            pl.BlockSpec((UP_TM, D_MODEL), lambda j, i: (i, 0)),
            window(D_MODEL, layer * D_MODEL, 0), window(D_MODEL, layer * D_MODEL, D_FF),
            window(CONV_ROWS, layer * CONV_ROWS, 0), window(CONV_ROWS, layer * CONV_ROWS, D_FF),
            pl.BlockSpec((None, None, DOWN_BLOCK_ROWS, D_MODEL), lambda j, i: (layer, dblk(j, i), 0, 0)),
        ],
        out_specs=(pl.BlockSpec((UP_TM, UP_TN), lambda j, i: (i, j)),
                   pl.BlockSpec((None, DOWN_BLOCK_ROWS, D_MODEL), lambda j, i: (dblk(j, i), 0, 0))),
        scratch_shapes=[pltpu.VMEM((D_MODEL, UP_TN), BF16), pltpu.VMEM((D_MODEL, UP_TN), BF16),
                        pltpu.VMEM((HALO + UP_TM, UP_TN), F32), pltpu.VMEM((HALO + UP_TM, UP_TN), F32)],
        compiler_params=_params("arbitrary", "arbitrary"),
        name="ffn_up_conv_gate",
    )(x_bf, w_up2d, w_up2d, conv8, conv8, w_down4d)


DOWN_TM = 512
DOWN_SUB = 256


def _ffn_down_kernel(alpha, a_ref, w_ref, x_ref, g_ref, b_ref, o_ref):
    for r in range(0, DOWN_TM, DOWN_SUB):
        f = jnp.dot(a_ref[r:r + DOWN_SUB, :], w_ref[...], preferred_element_type=F32)
        o_ref[r:r + DOWN_SUB, :] = _deepnorm_ln(x_ref[r:r + DOWN_SUB, :], f, g_ref[...], b_ref[...], alpha)


def _ffn_down(act, w_down_bf, x, g, b, alpha):
    m = x.shape[0]
    vec = pl.BlockSpec((1, D_MODEL), lambda i: (0, 0))
    return pl.pallas_call(
        functools.partial(_ffn_down_kernel, alpha),
        out_shape=jax.ShapeDtypeStruct((m, D_MODEL), F32),
        grid=(m // DOWN_TM,),
        in_specs=[
            pl.BlockSpec((DOWN_TM, D_FF), lambda i: (i, 0)),
            _resident((D_FF, D_MODEL), lambda i: (0, 0)),
            pl.BlockSpec((DOWN_TM, D_MODEL), lambda i: (i, 0)),
            vec, vec,
        ],
        out_specs=pl.BlockSpec((DOWN_TM, D_MODEL), lambda i: (i, 0)),
        compiler_params=_params("parallel"),
        name="ffn_down_ln",
    )(act, w_down_bf, x, g, b)


def kernel(x, w_in, attn_sinks, pool_w, pool_scale, ssm_lam_re, ssm_lam_im, ssm_log_dt, ssm_b_re, ssm_b_im,
           ssm_c_re, ssm_c_im, ssm_d, ssm_glu_w, w_out, ln1_g, ln1_b, ffn_w_up, ffn_conv_w, ffn_conv_b,
           ffn_w_down, ln2_g, ln2_b):
    bsz, s_len, _ = x.shape
    assert s_len == SEQ and x.shape[2] == D_MODEL
    depth = w_in.shape[0]
    alpha = (2 * depth) ** 0.25
    m = bsz * s_len
    cos_t, sa_t, sb_t = _rope_tables()
    xf = x.reshape(m, D_MODEL).astype(F32)

    w_out2d = w_out.astype(F32).reshape(depth * D_MODEL, D_MODEL)
    w_up2d = ffn_w_up.astype(F32).reshape(depth * D_MODEL, 2 * D_FF)
    w_down4d = ffn_w_down.astype(F32).reshape(depth, DOWN_BLOCKS, DOWN_BLOCK_ROWS, D_MODEL)
    conv8 = jnp.concatenate([ffn_conv_w.astype(F32), ffn_conv_b.astype(F32)[:, None, :],
                             jnp.zeros((depth, CONV_ROWS - 4, 2 * D_FF), F32)], axis=1)
    conv8 = conv8.reshape(depth * CONV_ROWS, 2 * D_FF)
    w_in2d = w_in.astype(F32).reshape(depth * D_MODEL, 2 * HALF_WIDTH)
    glu_w2d = ssm_glu_w.astype(F32).reshape(depth * SSM_WIDTH, 2 * SSM_WIDTH)
    pool_w3d = pool_w.astype(F32).reshape(depth * len(POOL_WINDOWS), POOL_GROUP, POOL_GROUP)
    vec = lambda a: a.astype(F32).reshape(1, -1)
    ssm_ops = _ssm_operators(ssm_lam_re, ssm_lam_im, ssm_log_dt, ssm_b_re, ssm_b_im, ssm_c_re, ssm_c_im, ssm_d)

    for l in range(depth):
        q, k2, v2, pu, su, w_out_bf = _in_proj(xf, w_in2d, w_out2d, l, cos_t, sa_t, sb_t)
        y_attn = _attention(q, k2, v2, attn_sinks[l].astype(F32))
        y_pool = _pool(pu, pool_w3d, l, vec(pool_scale[l]))

        y8 = _ssm(su, l, *ssm_ops)

        x1, x1_bf = _out_proj(y_attn, y_pool, y8, glu_w2d, l, w_out_bf, xf,
                              vec(ln1_g[l]), vec(ln1_b[l]), alpha)
        act, w_down_bf = _ffn_up(x1_bf, w_up2d, l, conv8, w_down4d)
        xf = _ffn_down(act, w_down_bf.reshape(D_FF, D_MODEL), x1, vec(ln2_g[l]), vec(ln2_b[l]), alpha)

    return xf.reshape(bsz, s_len, D_MODEL).astype(x.dtype)
```

```python
import functools

import jax
import numpy as np
import jax.numpy as jnp
from jax import lax
from jax.experimental import pallas as pl
from jax.experimental.pallas import tpu as pltpu

F32 = jnp.float32
BF16 = jnp.bfloat16

D_MODEL = 2048
SEQ = 2048
HEAD_DIM = 64
N_Q_HEADS = 16
N_KV_HEADS = 4
ATTN_WIDTH = N_Q_HEADS * HEAD_DIM
KV_WIDTH = N_KV_HEADS * HEAD_DIM
ATTN_BLOCK = 128
ROPE_THETA = 10000.0
POOL_WINDOWS = (2, 4, 8, 16)
POOL_GROUP = 128
POOL_WIDTH = 512
SSM_WIDTH = 512
SSM_GROUP = 16
SSM_N_GROUPS = 32
SSM_STATE = 64
SSM_CHUNK = 16
LN_EPS = 1e-5
D_FF = 5504

LANES = 128
SSM_BUNDLE = LANES // SSM_GROUP
SSM_N_BUNDLES = SSM_N_GROUPS // SSM_BUNDLE
SSM_BW = SSM_CHUNK * LANES
SSM_SW = SSM_BUNDLE * 2 * SSM_STATE
HALF_WIDTH = ATTN_WIDTH + KV_WIDTH
KV2_WIDTH = 2 * KV_WIDTH
VMEM_LIMIT = 56 * 1024 * 1024


def _params(*sem):
    return pltpu.CompilerParams(dimension_semantics=sem, vmem_limit_bytes=VMEM_LIMIT)


def _resident(block, index_map):
    return pl.BlockSpec(block, index_map, pipeline_mode=pl.Buffered(1))


IN_TM = 512
CAST_ROWS = 256


def _cast_weight_once(w_ref, wbf_ref):
    @pl.when(pl.program_id(0) == 0)
    def _():
        for r in range(0, w_ref.shape[0], CAST_ROWS):
            wbf_ref[r:r + CAST_ROWS, :] = w_ref[r:r + CAST_ROWS, :].astype(BF16)


def _dup_heads(pair):
    lo = lax.broadcasted_iota(jnp.int32, pair.shape, 1) < HEAD_DIM
    swapped = pltpu.roll(pair, HEAD_DIM, 1)
    return jnp.where(lo, pair, swapped), jnp.where(lo, swapped, pair)


def _inproj_kernel(x_ref, wa_ref, wb_ref, cos_ref, sa_ref, sb_ref, wo_ref, q_ref, k_ref, v_ref, p_ref, s_ref,
                   wobf_ref, wabf_ref, wbbf_ref, tok_ref):
    _cast_weight_once(wa_ref, wabf_ref)
    _cast_weight_once(wb_ref, wbbf_ref)
    wobf_ref[...] = wo_ref[...].astype(BF16)
    xb = x_ref[...].astype(BF16)
    acc = jnp.dot(xb, wabf_ref[...], preferred_element_type=F32)
    cos, sa, sb = cos_ref[...], sa_ref[...], sb_ref[...]
    nq = ATTN_WIDTH // LANES
    for c in range(HALF_WIDTH // LANES):
        a = acc[:, c * LANES:(c + 1) * LANES]
        r = a * cos + pltpu.roll(a, LANES - 32, 1) * sa + pltpu.roll(a, 32, 1) * sb
        if c < nq:
            q_ref[:, c * LANES:(c + 1) * LANES] = (r * (HEAD_DIM ** -0.5)).astype(BF16)
        else:
            ka, kb = _dup_heads(r)
            g = 2 * (c - nq)
            k_ref[:, g * LANES:(g + 1) * LANES] = ka.astype(BF16)
            k_ref[:, (g + 1) * LANES:(g + 2) * LANES] = kb.astype(BF16)

    acc = jnp.dot(xb, wbbf_ref[...], preferred_element_type=F32)
    for c in range(KV_WIDTH // LANES):
        va, vb = _dup_heads(acc[:, c * LANES:(c + 1) * LANES])
        v_ref[:, 2 * c * LANES:(2 * c + 1) * LANES] = va.astype(BF16)
        v_ref[:, (2 * c + 1) * LANES:(2 * c + 2) * LANES] = vb.astype(BF16)
    p_ref[...] = acc[:, KV_WIDTH:KV_WIDTH + POOL_WIDTH].astype(BF16)
    for b in range(SSM_N_BUNDLES):
        c0 = KV_WIDTH + POOL_WIDTH + b * LANES
        tok_ref[b] = acc[:, c0:c0 + LANES]
        for j in range(SSM_CHUNK):
            rows = tok_ref[b, pl.ds(j, IN_TM // SSM_CHUNK, stride=SSM_CHUNK), :]
            s_ref[b, :, j * LANES:(j + 1) * LANES] = rows.astype(BF16)


def _in_proj(x, w_in2d, w_out2d, layer, cos, sa, sb):
    m = x.shape[0]
    nseq = SEQ // IN_TM
    x_spec = pl.BlockSpec((IN_TM, D_MODEL), lambda i: (i, 0))
    tab = pl.BlockSpec((IN_TM, LANES), lambda i: (i % nseq, 0))
    row = lambda width: pl.BlockSpec((IN_TM, width), lambda i: (i, 0))
    steps = m // IN_TM
    wo_rows = D_MODEL // steps
    return pl.pallas_call(
        _inproj_kernel,
        out_shape=(jax.ShapeDtypeStruct((m, ATTN_WIDTH), BF16), jax.ShapeDtypeStruct((m, KV2_WIDTH), BF16),
                   jax.ShapeDtypeStruct((m, KV2_WIDTH), BF16), jax.ShapeDtypeStruct((m, POOL_WIDTH), BF16),
                   jax.ShapeDtypeStruct((SSM_N_BUNDLES, m // SSM_CHUNK, SSM_BW), BF16),
                   jax.ShapeDtypeStruct((D_MODEL, D_MODEL), BF16)),
        grid=(steps,),
        in_specs=[x_spec, _resident((D_MODEL, HALF_WIDTH), lambda i: (layer, 0)),
                  _resident((D_MODEL, HALF_WIDTH), lambda i: (layer, 1)), tab, tab, tab,
                  pl.BlockSpec((wo_rows, D_MODEL), lambda i: (layer * steps + i, 0))],
        out_specs=(row(ATTN_WIDTH), row(KV2_WIDTH), row(KV2_WIDTH), row(POOL_WIDTH),
                   pl.BlockSpec((SSM_N_BUNDLES, IN_TM // SSM_CHUNK, SSM_BW), lambda i: (0, i, 0)),
                   pl.BlockSpec((wo_rows, D_MODEL), lambda i: (i, 0))),
        scratch_shapes=[pltpu.VMEM((D_MODEL, HALF_WIDTH), BF16), pltpu.VMEM((D_MODEL, HALF_WIDTH), BF16),
                        pltpu.VMEM((SSM_N_BUNDLES, IN_TM, LANES), F32)],
        compiler_params=_params("arbitrary"),
        name="in_proj",
    )(x, w_in2d, w_in2d, cos, sa, sb, w_out2d)


def _rope_tables():
    half = HEAD_DIM // 2
    inv = ROPE_THETA ** (-jnp.arange(half, dtype=F32) / half)
    ang = jnp.arange(SEQ, dtype=F32)[:, None] * inv[None, :]
    cos, sin = jnp.cos(ang), jnp.sin(ang)
    zero = jnp.zeros_like(sin)
    reps = LANES // HEAD_DIM
    cos_t = jnp.tile(jnp.concatenate([cos, cos], -1), (1, reps))
    sa_t = jnp.tile(jnp.concatenate([-sin, zero], -1), (1, reps))
    sb_t = jnp.tile(jnp.concatenate([zero, sin], -1), (1, reps))
    return cos_t, sa_t, sb_t


ATTN_TQ = 1024
ATTN_QB = ATTN_TQ // ATTN_BLOCK
KEYS = 2 * ATTN_BLOCK


def _attn_kernel(sink_ref, q_ref, kc_ref, vc_ref, kp_ref, vp_ref, o_ref, kbuf, vbuf):
    kbuf[0:ATTN_BLOCK, :] = kp_ref[...]
    kbuf[ATTN_BLOCK:, :] = kc_ref[...]
    vbuf[0:ATTN_BLOCK, :] = vp_ref[...]
    vbuf[ATTN_BLOCK:, :] = vc_ref[...]
    seq_start = pl.program_id(1) == 0

    row = lax.broadcasted_iota(jnp.int32, (ATTN_BLOCK, KEYS), 0)
    col = lax.broadcasted_iota(jnp.int32, (ATTN_BLOCK, KEYS), 1)
    dist = row + ATTN_BLOCK - col
    band = (dist >= 0) & (dist < ATTN_BLOCK)
    band_first = band & (jnp.logical_not(seq_start) | (col >= ATTN_BLOCK))
    lo_kv = lax.broadcasted_iota(jnp.int32, (KEYS, LANES), 1) < HEAD_DIM
    lo_out = lax.broadcasted_iota(jnp.int32, (ATTN_BLOCK, LANES), 1) < HEAD_DIM
    zero_kv = jnp.zeros((KEYS, LANES), BF16)

    def split_heads(x):
        return jnp.concatenate([jnp.where(lo_kv, x, zero_kv), jnp.where(lo_kv, zero_kv, x)], axis=0)

    for qb in range(ATTN_QB):
        valid = band_first if qb == 0 else band
        r0 = qb * ATTN_BLOCK
        for g in range(N_KV_HEADS):
            k2 = split_heads(kbuf[r0:r0 + KEYS, g * LANES:(g + 1) * LANES])
            v2 = split_heads(vbuf[r0:r0 + KEYS, g * LANES:(g + 1) * LANES])
            c0 = 2 * g * LANES
            qq = jnp.concatenate([q_ref[r0:r0 + ATTN_BLOCK, c0:c0 + LANES],
                                  q_ref[r0:r0 + ATTN_BLOCK, c0 + LANES:c0 + 2 * LANES]], axis=0)
            s4 = lax.dot_general(qq, k2, (((1,), (1,)), ((), ())), preferred_element_type=F32)
            p_rows, dens = [], []
            for pr in range(2):
                ps = []
                for hh in range(2):
                    sink = sink_ref[4 * g + 2 * pr + hh]
                    s = jnp.where(valid, s4[pr * ATTN_BLOCK:(pr + 1) * ATTN_BLOCK, hh * KEYS:(hh + 1) * KEYS], -1e30)
                    mx = jnp.maximum(s.max(-1, keepdims=True), sink)
                    p = jnp.exp(s - mx)
                    dens.append(p.sum(-1, keepdims=True) + jnp.exp(sink - mx))
                    ps.append(p.astype(BF16))
                p_rows.append(jnp.concatenate(ps, axis=1))
            o4 = jnp.dot(jnp.concatenate(p_rows, axis=0), v2, preferred_element_type=F32)
            for pr in range(2):
                o = o4[pr * ATTN_BLOCK:(pr + 1) * ATTN_BLOCK] / jnp.where(lo_out, dens[2 * pr], dens[2 * pr + 1])
                o_ref[r0:r0 + ATTN_BLOCK, c0 + pr * LANES:c0 + (pr + 1) * LANES] = o.astype(BF16)


def _attention(q, k2, v2, sinks):
    m = q.shape[0]
    nq = SEQ // ATTN_TQ
    cur = lambda b, i: (b * nq + i, 0)
    prev = lambda b, i: (jnp.maximum((b * nq + i) * ATTN_QB - 1, 0), 0)
    return pl.pallas_call(
        _attn_kernel,
        out_shape=jax.ShapeDtypeStruct((m, ATTN_WIDTH), BF16),
        grid=(m // SEQ, nq),
        in_specs=[
            pl.BlockSpec(memory_space=pltpu.SMEM),
            pl.BlockSpec((ATTN_TQ, ATTN_WIDTH), cur),
            pl.BlockSpec((ATTN_TQ, KV2_WIDTH), cur),
            pl.BlockSpec((ATTN_TQ, KV2_WIDTH), cur),
            pl.BlockSpec((ATTN_BLOCK, KV2_WIDTH), prev),
            pl.BlockSpec((ATTN_BLOCK, KV2_WIDTH), prev),
        ],
        out_specs=pl.BlockSpec((ATTN_TQ, ATTN_WIDTH), cur),
        scratch_shapes=[pltpu.VMEM((ATTN_TQ + ATTN_BLOCK, KV2_WIDTH), BF16),
                        pltpu.VMEM((ATTN_TQ + ATTN_BLOCK, KV2_WIDTH), BF16)],
        compiler_params=_params("parallel", "arbitrary"),
        name="swa_attention",
    )(sinks, q, k2, v2, k2, v2)


def _shift_rows(x, d, rows):
    return jnp.where(rows >= d, pltpu.roll(x, d, 0), 0.0)


def _pool_kernel(u_ref, w_ref, scale_ref, o_ref):
    rows = lax.broadcasted_iota(jnp.int32, (SEQ, POOL_GROUP), 0)
    t1 = lax.broadcasted_iota(jnp.int32, (SEQ, 1), 0).astype(F32) + 1.0
    for idx, w in enumerate(POOL_WINDOWS):
        cols = slice(idx * POOL_GROUP, (idx + 1) * POOL_GROUP)
        u = u_ref[:, cols].astype(F32)
        s, d = u, 1
        while d < w:
            s = s + _shift_rows(s, d, rows)
            d *= 2
        mean = s / jnp.minimum(t1, float(w))
        y = jnp.dot((mean - u).astype(BF16), w_ref[idx].astype(BF16), preferred_element_type=F32)
        o_ref[:, cols] = (y * scale_ref[:, cols]).astype(BF16)


def _pool(pu, pool_w3d, layer, pool_scale_l):
    m = pu.shape[0]
    nw = len(POOL_WINDOWS)
    return pl.pallas_call(
        _pool_kernel,
        out_shape=jax.ShapeDtypeStruct((m, POOL_WIDTH), BF16),
        grid=(m // SEQ,),
        in_specs=[
            pl.BlockSpec((SEQ, POOL_WIDTH), lambda b: (b, 0)),
            pl.BlockSpec((nw, POOL_GROUP, POOL_GROUP), lambda b: (layer, 0, 0)),
            pl.BlockSpec((1, POOL_WIDTH), lambda b: (0, 0)),
        ],
        out_specs=pl.BlockSpec((SEQ, POOL_WIDTH), lambda b: (b, 0)),
        compiler_params=_params("parallel"),
        name="multiscale_pool",
    )(pu, pool_w3d, pool_scale_l)


SSM_SCAN_STEPS = 7
SSM_COLS = 2 * LANES


def _ssm_kernel(u_ref, a_ref, eg_ref, cg_ref, x_ref, ar_ref, ai_ref, o_ref, km_ref, em_ref, cm_ref):
    T, GB, H = SSM_CHUNK, SSM_BUNDLE, SSM_GROUP
    sh = H.bit_length() - 1

    @pl.when(pl.program_id(0) == 0)
    def _():
        em_ref[...] = jnp.zeros_like(em_ref)

    row_g = lax.broadcasted_iota(jnp.int32, (LANES, LANES), 0) >> sh
    col_g = lax.broadcasted_iota(jnp.int32, (LANES, LANES), 1) >> sh
    zero_blk = jnp.zeros((LANES, LANES), BF16)
    lag_blk = [jnp.where(row_g == col_g, jnp.concatenate([a_ref[0, t]] * GB, axis=0), zero_blk) for t in range(T)]
    for i in range(T):
        for j in range(min(i | 1, T - 1) + 1):
            km_ref[j * LANES:(j + 1) * LANES, i * LANES:(i + 1) * LANES] = lag_blk[i - j] if j <= i else zero_blk

    for j in range(T):
        for g in range(GB):
            r0 = j * LANES + g * H
            em_ref[r0:r0 + H, g * LANES:(g + 1) * LANES] = eg_ref[0, g, j * H:(j + 1) * H, :]

    lane_g = (lax.broadcasted_iota(jnp.int32, (LANES, SSM_BW), 1) & (LANES - 1)) >> sh
    for g in range(GB):
        spread = jnp.dot(cg_ref[0, g], x_ref[...], preferred_element_type=F32)
        cm_ref[g * LANES:(g + 1) * LANES, :] = jnp.where(lane_g == g, spread, 0.0).astype(BF16)

    u = u_ref[0]
    e = jnp.dot(u, em_ref[...], preferred_element_type=F32)
    nchunk = SEQ // SSM_CHUNK
    rows = lax.broadcasted_iota(jnp.int32, (e.shape[0], LANES), 0) % nchunk
    zprev = []
    for g in range(SSM_BUNDLE):
        eg = e[:, g * LANES:(g + 1) * LANES]
        for k in range(SSM_SCAN_STEPS):
            s = _shift_rows(eg, 1 << k, rows)
            sl = slice(g * LANES, (g + 1) * LANES)
            eg = eg + ar_ref[0, k:k + 1, sl] * s + ai_ref[0, k:k + 1, sl] * pltpu.roll(s, SSM_STATE, 1)
        zprev.append(_shift_rows(eg, 1, rows).astype(BF16))
    zp = jnp.concatenate(zprev, axis=1)

    for mblk in range(SSM_BW // SSM_COLS):
        c0, kdim = mblk * SSM_COLS, (mblk + 1) * SSM_COLS
        y = jnp.dot(u[:, :kdim], km_ref[0:kdim, c0:c0 + SSM_COLS], preferred_element_type=F32)
        y = y + jnp.dot(zp, cm_ref[:, c0:c0 + SSM_COLS], preferred_element_type=F32)
        o_ref[0, :, c0:c0 + SSM_COLS] = jax.nn.gelu(y).astype(BF16)


def _ssm(u8, layer, a_lag, eg, cg, spread, ar, ai):
    nb, rows, _ = u8.shape
    T, GB, H, Q = SSM_CHUNK, SSM_BUNDLE, SSM_GROUP, 2 * SSM_STATE
    par = lambda *dims: pl.BlockSpec((1,) + dims, lambda i: (layer * nb + i,) + (0,) * len(dims))
    act = pl.BlockSpec((1, rows, SSM_BW), lambda i: (i, 0, 0))
    return pl.pallas_call(
        _ssm_kernel,
        out_shape=jax.ShapeDtypeStruct((nb, rows, SSM_BW), BF16),
        grid=(nb,),
        in_specs=[act, par(T, H, LANES), par(GB, T * H, Q), par(GB, Q, T * H),
                  pl.BlockSpec((T * H, SSM_BW), lambda i: (0, 0)), par(8, SSM_SW), par(8, SSM_SW)],
        out_specs=act,
        scratch_shapes=[pltpu.VMEM((SSM_BW, SSM_BW), BF16), pltpu.VMEM((SSM_BW, SSM_SW), BF16),
                        pltpu.VMEM((SSM_SW, SSM_BW), BF16)],
        compiler_params=_params("arbitrary"),
        name="s5_chunked_ssm",
    )(u8, a_lag, eg, cg, spread, ar, ai)


def _cmul(xr, xi, yr, yi):
    return xr * yr - xi * yi, xr * yi + xi * yr


def _ssm_group_kernel(cr_ref, ci_ref, br_ref, bi_ref, pr_ref, pi_ref, qr_ref, qi_ref, d_ref, a_ref, e_ref, c_ref):
    T, H, P = SSM_CHUNK, SSM_GROUP, SSM_STATE
    sh = H.bit_length() - 1
    lane = lax.broadcasted_iota(jnp.int32, (H, LANES), 1)
    row = lax.broadcasted_iota(jnp.int32, (H, LANES), 0)
    per_col = LANES // H
    for nb in range(a_ref.shape[0]):
        lags = [jnp.zeros((H, LANES), F32)] * T
        for g in range(SSM_BUNDLE):
            n = nb * SSM_BUNDLE + g
            cr, ci, br, bi = cr_ref[n], ci_ref[n], br_ref[n], bi_ref[n]
            pr, pi = pr_ref[n][:, None, :], pi_ref[n][:, None, :]
            car = cr[None] * pr - ci[None] * pi
            cai = cr[None] * pi + ci[None] * pr
            ca = jnp.concatenate([car[:T].reshape(T * H, P), cai[:T].reshape(T * H, P)], axis=1)
            bb = jnp.concatenate([br, -bi], axis=1)
            k = lax.dot_general(bb, ca, (((1,), (1,)), ((), ())),
                                precision=lax.Precision.HIGHEST, preferred_element_type=F32)
            for t in range(T):
                col = k[:, (t // per_col) * LANES:(t // per_col + 1) * LANES]
                shift = ((g - t % per_col) * H) % LANES
                moved = pltpu.roll(col, shift, 1) if shift else col
                lags[t] = jnp.where((lane >> sh) == g, moved, lags[t])
            qr, qi = qr_ref[n][:, None, :], qi_ref[n][:, None, :]
            er = (qr * br[None] - qi * bi[None]).reshape(T * H, P)
            ei = (qr * bi[None] + qi * br[None]).reshape(T * H, P)
            e_ref[n] = jnp.concatenate([er, ei], axis=1).astype(BF16)
            c_ref[n] = jnp.concatenate([car[1:].reshape(T * H, P).T, -cai[1:].reshape(T * H, P).T],
                                       axis=0).astype(BF16)
        lags[0] = lags[0] + jnp.where((lane & (H - 1)) == row, d_ref[nb], 0.0)
        for t in range(T):
            a_ref[nb, t] = lags[t].astype(BF16)


PREP_BUNDLES = 1


def _ssm_group_operators(cr, ci, brt, bit, pr, pi, qr, qi, d_lane):
    n = cr.shape[0]
    T, H, P = SSM_CHUNK, SSM_GROUP, SSM_STATE
    ng = PREP_BUNDLES * SSM_BUNDLE
    spec = lambda a, b: pl.BlockSpec((ng, a, b), lambda i: (i, 0, 0))
    return pl.pallas_call(
        _ssm_group_kernel,
        out_shape=(jax.ShapeDtypeStruct((n // SSM_BUNDLE, T, H, LANES), BF16),
                   jax.ShapeDtypeStruct((n, T * H, 2 * P), BF16), jax.ShapeDtypeStruct((n, 2 * P, T * H), BF16)),
        grid=(n // ng,),
        in_specs=[spec(H, P)] * 4 + [spec(T + 1, P)] * 2 + [spec(T, P)] * 2
                 + [pl.BlockSpec((PREP_BUNDLES, 1, LANES), lambda i: (i, 0, 0))],
        out_specs=(pl.BlockSpec((PREP_BUNDLES, T, H, LANES), lambda i: (i, 0, 0, 0)),
                   spec(T * H, 2 * P), spec(2 * P, T * H)),
        compiler_params=_params("parallel"),
        name="ssm_group_operators",
    )(cr, ci, brt, bit, pr, pi, qr, qi, d_lane)


def _ssm_operators(lam_re, lam_im, log_dt, b_re, b_im, c_re, c_im, d):
    lr, li = lam_re.astype(F32), lam_im.astype(F32)
    dt = jnp.exp(log_dt.astype(F32))[..., None]
    mag = jnp.exp(lr * dt)
    ar, ai = mag * jnp.cos(li * dt), mag * jnp.sin(li * dt)
    nr, ni = ar - 1.0, ai
    den = lr * lr + li * li
    zr = (nr * lr + ni * li) / den
    zi = (ni * lr - nr * li) / den
    br, bi = b_re.astype(F32), b_im.astype(F32)
    bbr = zr[..., None] * br - zi[..., None] * bi
    bbi = zr[..., None] * bi + zi[..., None] * br
    cr, ci = c_re.astype(F32), c_im.astype(F32)

    pr, pi = [jnp.ones_like(ar)], [jnp.zeros_like(ar)]
    for _ in range(SSM_CHUNK):
        nr_, ni_ = _cmul(pr[-1], pi[-1], ar, ai)
        pr.append(nr_)
        pi.append(ni_)
    prr, pir = jnp.stack(pr[SSM_CHUNK - 1::-1]), jnp.stack(pi[SSM_CHUNK - 1::-1])
    pr, pi = jnp.stack(pr), jnp.stack(pi)

    T, G, H, P = SSM_CHUNK, SSM_N_GROUPS, SSM_GROUP, SSM_STATE
    NB, GB = SSM_N_BUNDLES, SSM_BUNDLE
    L = lr.shape[0]
    per_group = lambda t: jnp.moveaxis(t, 0, 2).reshape(L * G, t.shape[0], P)
    a_lag, eg, cg = _ssm_group_operators(
        cr.reshape(L * G, H, P), ci.reshape(L * G, H, P),
        bbr.transpose(0, 1, 3, 2).reshape(L * G, H, P), bbi.transpose(0, 1, 3, 2).reshape(L * G, H, P),
        per_group(pr), per_group(pi), per_group(prr), per_group(pir), d.astype(F32).reshape(L * NB, 1, GB * H))
    eg = eg.reshape(L * NB, GB, T * H, 2 * P)
    cg = cg.reshape(L * NB, GB, 2 * P, T * H)

    sr, si = [pr[T]], [pi[T]]
    for _ in range(SSM_SCAN_STEPS - 1):
        nr_, ni_ = _cmul(sr[-1], si[-1], sr[-1], si[-1])
        sr.append(nr_)
        si.append(ni_)
    sr.append(jnp.zeros_like(ar))
    si.append(jnp.zeros_like(ar))
    sr, si = jnp.stack(sr, 2), jnp.stack(si, 2)
    lay = lambda t: t.reshape(L, NB, GB, 8, 2 * P).transpose(0, 1, 3, 2, 4).reshape(L * NB, 8, GB * 2 * P)
    ar_t = lay(jnp.concatenate([sr, sr], -1))
    ai_t = lay(jnp.concatenate([-si, si], -1))

    spread = np.zeros((T, H, T, GB, H), np.float32)
    for i in range(T):
        for h in range(H):
            spread[i, h, i, :, h] = 1.0
    spread = jnp.asarray(spread.reshape(T * H, T * GB * H), BF16)
    return a_lag, eg, cg, spread, ar_t, ai_t


def _deepnorm_ln(x, branch, g, b, alpha):
    y = alpha * x + branch
    mu = y.mean(-1, keepdims=True)
    yc = y - mu
    var = jnp.square(yc).mean(-1, keepdims=True)
    return yc * lax.rsqrt(var + LN_EPS) * g + b


OUT_TM = 512
OUT_SUB = 256


def _outproj_kernel(alpha, ya_ref, yp_ref, ys_ref, glu_ref, w_ref, x_ref, g_ref, b_ref, o_ref, ob_ref, tok_ref):
    for b in range(SSM_N_BUNDLES):
        for j in range(SSM_CHUNK):
            rows = ys_ref[b, :, j * LANES:(j + 1) * LANES].astype(F32)
            tok_ref[b, pl.ds(j, OUT_TM // SSM_CHUNK, stride=SSM_CHUNK), :] = rows
    ys = jnp.concatenate([tok_ref[b].astype(BF16) for b in range(SSM_N_BUNDLES)], axis=1)
    ab = jnp.dot(ys, glu_ref[...].astype(BF16), preferred_element_type=F32)
    y_ssm = (ab[:, :SSM_WIDTH] * jax.nn.sigmoid(ab[:, SSM_WIDTH:])).astype(BF16)
    o_p, o_s = ATTN_WIDTH, ATTN_WIDTH + POOL_WIDTH
    for r in range(0, OUT_TM, OUT_SUB):
        rs = slice(r, r + OUT_SUB)
        acc = jnp.dot(ya_ref[rs, :], w_ref[0:o_p, :], preferred_element_type=F32)
        acc += jnp.dot(yp_ref[rs, :], w_ref[o_p:o_s, :], preferred_element_type=F32)
        acc += jnp.dot(y_ssm[rs, :], w_ref[o_s:, :], preferred_element_type=F32)
        x1 = _deepnorm_ln(x_ref[rs, :], acc, g_ref[...], b_ref[...], alpha)
        o_ref[rs, :] = x1
        ob_ref[rs, :] = x1.astype(BF16)


def _out_proj(ya, yp, ys8, glu_w2d, layer, w_out_bf, x, g, b, alpha):
    m = x.shape[0]
    row = lambda width: pl.BlockSpec((OUT_TM, width), lambda i: (i, 0))
    vec = pl.BlockSpec((1, D_MODEL), lambda i: (0, 0))
    return pl.pallas_call(
        functools.partial(_outproj_kernel, alpha),
        out_shape=(jax.ShapeDtypeStruct((m, D_MODEL), F32), jax.ShapeDtypeStruct((m, D_MODEL), BF16)),
        grid=(m // OUT_TM,),
        in_specs=[
            row(ATTN_WIDTH), row(POOL_WIDTH),
            pl.BlockSpec((SSM_N_BUNDLES, OUT_TM // SSM_CHUNK, SSM_BW), lambda i: (0, i, 0)),
            pl.BlockSpec((SSM_WIDTH, 2 * SSM_WIDTH), lambda i: (layer, 0)),
            _resident((D_MODEL, D_MODEL), lambda i: (0, 0)),
            row(D_MODEL), vec, vec,
        ],
        out_specs=(row(D_MODEL), row(D_MODEL)),
        scratch_shapes=[pltpu.VMEM((SSM_N_BUNDLES, OUT_TM, LANES), F32)],
        compiler_params=_params("parallel"),
        name="out_proj_ln",
    )(ya, yp, ys8, glu_w2d, w_out_bf, x, g, b)


UP_TM = 1024
UP_TN = 512
UP_NJ = -(-D_FF // UP_TN)
FF_PAD = UP_NJ * UP_TN
UP_SHIFT = FF_PAD - D_FF
HALO = 8
CONV_ROWS = 8
DOWN_BLOCK_ROWS = 128
DOWN_BLOCKS = D_FF // DOWN_BLOCK_ROWS


def _ffn_up_kernel(x_ref, wv_ref, wg_ref, cv_ref, cg_ref, wd_ref, o_ref, wdb_ref, wvb_ref, wgb_ref, hv_ref, hg_ref):
    j, i = pl.program_id(0), pl.program_id(1)

    @pl.when(j * pl.num_programs(1) + i < DOWN_BLOCKS)
    def _():
        wdb_ref[...] = wd_ref[...].astype(BF16)

    @pl.when(i == 0)
    def _():
        for r in range(0, D_MODEL, CAST_ROWS):
            wvb_ref[r:r + CAST_ROWS, :] = wv_ref[r:r + CAST_ROWS, :].astype(BF16)
            wgb_ref[r:r + CAST_ROWS, :] = wg_ref[r:r + CAST_ROWS, :].astype(BF16)
        hv_ref[0:HALO, :] = jnp.zeros((HALO, UP_TN), F32)
        hg_ref[0:HALO, :] = jnp.zeros((HALO, UP_TN), F32)

    @pl.when(i != 0)
    def _():
        hv_ref[0:HALO, :] = hv_ref[UP_TM:, :]
        hg_ref[0:HALO, :] = hg_ref[UP_TM:, :]

    x = x_ref[...]
    seq_start = (i % (SEQ // UP_TM)) == 0
    rows = lax.broadcasted_iota(jnp.int32, (HALO, 1), 0)

    def conv(wb_ref, c_ref, h_ref):
        h_ref[HALO:, :] = jnp.dot(x, wb_ref[...], preferred_element_type=F32)

        def tap(shift):
            h = h_ref[HALO - shift:HALO - shift + UP_TM, :]
            head = jnp.where(seq_start & (rows < shift), 0.0, h[:HALO])
            return jnp.concatenate([head, h[HALO:]], axis=0)

        return c_ref[3:4, :] + tap(2) * c_ref[0:1, :] + tap(1) * c_ref[1:2, :] + h_ref[HALO:, :] * c_ref[2:3, :]

    gate = jax.nn.silu(conv(wgb_ref, cg_ref, hg_ref))
    act = (gate * conv(wvb_ref, cv_ref, hv_ref)).astype(BF16)
    shifted = jnp.concatenate([act[:, UP_SHIFT:], jnp.zeros((UP_TM, UP_SHIFT), BF16)], axis=1)
    o_ref[...] = jnp.where(j == pl.num_programs(0) - 1, shifted, act)


def _ffn_up(x_bf, w_up2d, layer, conv8, w_down4d):
    m = x_bf.shape[0]
    ni = m // UP_TM
    assert UP_NJ * ni >= DOWN_BLOCKS
    col = lambda c0, j: pl.multiple_of(c0 + jnp.minimum(j * UP_TN, D_FF - UP_TN), LANES)
    window = lambda rows, r0, c0: pl.BlockSpec((pl.Element(rows), pl.Element(UP_TN)),
                                               lambda j, i: (r0, col(c0, j)))
    dblk = lambda j, i: jnp.minimum(j * ni + i, DOWN_BLOCKS - 1)
    return pl.pallas_call(
        _ffn_up_kernel,
        out_shape=(jax.ShapeDtypeStruct((m, FF_PAD), BF16),
                   jax.ShapeDtypeStruct((DOWN_BLOCKS, DOWN_BLOCK_ROWS, D_MODEL), BF16)),
        grid=(UP_NJ, ni),
        in_specs=[
            pl.BlockSpec((UP_TM, D_MODEL), lambda j, i: (i, 0)),
            window(D_MODEL, layer * D_MODEL, 0), window(D_MODEL, layer * D_MODEL, D_FF),
            window(CONV_ROWS, layer * CONV_ROWS, 0), window(CONV_ROWS, layer * CONV_ROWS, D_FF),
            pl.BlockSpec((None, None, DOWN_BLOCK_ROWS, D_MODEL), lambda j, i: (layer, dblk(j, i), 0, 0)),
        ],
        out_specs=(pl.BlockSpec((UP_TM, UP_TN), lambda j, i: (i, j)),
                   pl.BlockSpec((None, DOWN_BLOCK_ROWS, D_MODEL), lambda j, i: (dblk(j, i), 0, 0))),
        scratch_shapes=[pltpu.VMEM((D_MODEL, UP_TN), BF16), pltpu.VMEM((D_MODEL, UP_TN), BF16),
                        pltpu.VMEM((HALO + UP_TM, UP_TN), F32), pltpu.VMEM((HALO + UP_TM, UP_TN), F32)],
        compiler_params=_params("arbitrary", "arbitrary"),
        name="ffn_up_conv_gate",
    )(x_bf, w_up2d, w_up2d, conv8, conv8, w_down4d)


DOWN_TM = 512
DOWN_SUB = 256


def _ffn_down_kernel(alpha, a_ref, w_ref, x_ref, g_ref, b_ref, o_ref):
    for r in range(0, DOWN_TM, DOWN_SUB):
        f = jnp.dot(a_ref[r:r + DOWN_SUB, :], w_ref[...], preferred_element_type=F32)
        o_ref[r:r + DOWN_SUB, :] = _deepnorm_ln(x_ref[r:r + DOWN_SUB, :], f, g_ref[...], b_ref[...], alpha)


def _ffn_down(act, w_down_bf, x, g, b, alpha):
    m = x.shape[0]
    vec = pl.BlockSpec((1, D_MODEL), lambda i: (0, 0))
    return pl.pallas_call(
        functools.partial(_ffn_down_kernel, alpha),
        out_shape=jax.ShapeDtypeStruct((m, D_MODEL), F32),
        grid=(m // DOWN_TM,),
        in_specs=[
            pl.BlockSpec((DOWN_TM, D_FF), lambda i: (i, 0)),
            _resident((D_FF, D_MODEL), lambda i: (0, 0)),
            pl.BlockSpec((DOWN_TM, D_MODEL), lambda i: (i, 0)),
            vec, vec,
        ],
        out_specs=pl.BlockSpec((DOWN_TM, D_MODEL), lambda i: (i, 0)),
        compiler_params=_params("parallel"),
        name="ffn_down_ln",
    )(act, w_down_bf, x, g, b)


def kernel(x, w_in, attn_sinks, pool_w, pool_scale, ssm_lam_re, ssm_lam_im, ssm_log_dt, ssm_b_re, ssm_b_im,
           ssm_c_re, ssm_c_im, ssm_d, ssm_glu_w, w_out, ln1_g, ln1_b, ffn_w_up, ffn_conv_w, ffn_conv_b,
           ffn_w_down, ln2_g, ln2_b):
    bsz, s_len, _ = x.shape
    assert s_len == SEQ and x.shape[2] == D_MODEL
    depth = w_in.shape[0]
    alpha = (2 * depth) ** 0.25
    m = bsz * s_len
    cos_t, sa_t, sb_t = _rope_tables()
    xf = x.reshape(m, D_MODEL).astype(F32)

    w_out2d = w_out.astype(F32).reshape(depth * D_MODEL, D_MODEL)
    w_up2d = ffn_w_up.astype(F32).reshape(depth * D_MODEL, 2 * D_FF)
    w_down4d = ffn_w_down.astype(F32).reshape(depth, DOWN_BLOCKS, DOWN_BLOCK_ROWS, D_MODEL)
    conv8 = jnp.concatenate([ffn_conv_w.astype(F32), ffn_conv_b.astype(F32)[:, None, :],
                             jnp.zeros((depth, CONV_ROWS - 4, 2 * D_FF), F32)], axis=1)
    conv8 = conv8.reshape(depth * CONV_ROWS, 2 * D_FF)
    w_in2d = w_in.astype(F32).reshape(depth * D_MODEL, 2 * HALF_WIDTH)
    glu_w2d = ssm_glu_w.astype(F32).reshape(depth * SSM_WIDTH, 2 * SSM_WIDTH)
    pool_w3d = pool_w.astype(F32).reshape(depth * len(POOL_WINDOWS), POOL_GROUP, POOL_GROUP)
    vec = lambda a: a.astype(F32).reshape(1, -1)
    ssm_ops = _ssm_operators(ssm_lam_re, ssm_lam_im, ssm_log_dt, ssm_b_re, ssm_b_im, ssm_c_re, ssm_c_im, ssm_d)

    for l in range(depth):
        q, k2, v2, pu, su, w_out_bf = _in_proj(xf, w_in2d, w_out2d, l, cos_t, sa_t, sb_t)
        y_attn = _attention(q, k2, v2, attn_sinks[l].astype(F32))
        y_pool = _pool(pu, pool_w3d, l, vec(pool_scale[l]))

        y8 = _ssm(su, l, *ssm_ops)

        x1, x1_bf = _out_proj(y_attn, y_pool, y8, glu_w2d, l, w_out_bf, xf,
                              vec(ln1_g[l]), vec(ln1_b[l]), alpha)
        act, w_down_bf = _ffn_up(x1_bf, w_up2d, l, conv8, w_down4d)
        xf = _ffn_down(act, w_down_bf.reshape(D_FF, D_MODEL), x1, vec(ln2_g[l]), vec(ln2_b[l]), alpha)

    return xf.reshape(bsz, s_len, D_MODEL).astype(x.dtype)
```
